```python
import math
import jax, jax.numpy as jnp
from jax import lax
import numpy as np

D_MODEL = 2048
BATCH = 4
SEQ = 2048
DEPTH = 4
DEC_BATCH = 128
DEC_SEQ = 1
PAST_LEN = 16384
PAGE_SIZE = 128

RET_HEADS = 8
RET_DK = 128
RET_DV = 128
RET_QK = RET_HEADS * RET_DK
RET_W = RET_HEADS * RET_DV
RET_CHUNK = 128
ROPE_BASE = 10000.0
S5_GROUPS = 64
S5_GROUP_CH = 16
S5_W = S5_GROUPS * S5_GROUP_CH
S5_STATE = 64
S5_DT_MIN = 0.001
S5_DT_MAX = 0.1
D_FF = 5632
IN_W = 2 * RET_QK + 2 * RET_W + S5_W + 2 * D_MODEL
NORM_EPS = 1e-6
HEAD_NORM_EPS = 1e-5

kernel_name = 'hybrid_retention_s5_macaron_step'


def rms_norm(x, g):
    xf = x.astype(jnp.float32)
    xf = xf * lax.rsqrt(jnp.mean(xf * xf, axis=-1, keepdims=True) + NORM_EPS)
    return (xf * g.astype(jnp.float32)).astype(x.dtype)


def swiglu(x, w1, w3, w2):
    return (jax.nn.silu(x @ w1) * (x @ w3)) @ w2


def rotary(x, pos):
    half = x.shape[-1] // 2
    inv = ROPE_BASE ** (-jnp.arange(half, dtype=jnp.float32) / half)
    ang = pos.astype(jnp.float32)[:, None] * inv[None, :]
    cos = jnp.cos(ang)[None, :, None, :]
    sin = jnp.sin(ang)[None, :, None, :]
    x1, x2 = x[..., :half], x[..., half:]
    return jnp.concatenate([x1 * cos - x2 * sin, x1 * sin + x2 * cos], axis=-1)


def retention(q, k, v, s0):
    bsz, L, H, _ = q.shape
    C = RET_CHUNK if L % RET_CHUNK == 0 else L
    nc = L // C
    log_g = jnp.log1p(-jnp.exp2(-5.0 - jnp.arange(H, dtype=jnp.float32)))
    i = jnp.arange(C, dtype=jnp.float32)
    diff = i[:, None] - i[None, :]
    intra = jnp.where(diff[None] >= 0,
                      jnp.exp(jnp.maximum(diff, 0.0)[None] * log_g[:, None, None]), 0.0)
    q_dec = jnp.exp((i + 1.0)[None, :] * log_g[:, None])
    k_dec = jnp.exp((C - 1.0 - i)[None, :] * log_g[:, None])
    c_dec = jnp.exp(C * log_g)

    def chunks(t):
        return t.reshape(bsz, nc, C, H, t.shape[-1]).transpose(1, 0, 3, 2, 4)

    def step(s, qkv):
        qc, kc, vc = qkv
        sc = jnp.einsum('bhid,bhjd->bhij', qc, kc) * intra[None]
        o = (jnp.einsum('bhij,bhje->bhie', sc, vc)
             + jnp.einsum('bhid,bhde->bhie', qc * q_dec[None, :, :, None], s))
        s = (s * c_dec[None, :, None, None]
             + jnp.einsum('bhjd,bhje->bhde', kc * k_dec[None, :, :, None], vc))
        return s, o

    s, o = lax.scan(step, s0, (chunks(q), chunks(k), chunks(v)))
    o = o.transpose(1, 0, 3, 2, 4).reshape(bsz, L, H, v.shape[-1])
    return o, s


def head_norm(o, g):
    mu = jnp.mean(o, axis=-1, keepdims=True)
    var = jnp.mean(jnp.square(o - mu), axis=-1, keepdims=True)
    o = (o - mu) * lax.rsqrt(var + HEAD_NORM_EPS)
    bsz, L = o.shape[:2]
    return o.reshape(bsz, L, -1) * g.astype(jnp.float32)


def s5_scan(u, h0_re, h0_im, lam_re, lam_im, b_re, b_im, c_re, c_im, d, log_step):
    f32 = jnp.float32
    uf = u.astype(f32)
    L = u.shape[1]
    lam = lax.complex(lam_re.astype(f32), lam_im.astype(f32))
    dt = jnp.exp(log_step.astype(f32))[:, None]
    lam_dt = lam * dt
    lbar = jnp.exp(lam_dt)
    bbar = ((lbar - 1.0) / lam)[..., None] * lax.complex(b_re.astype(f32), b_im.astype(f32))
    cmat = lax.complex(c_re.astype(f32), c_im.astype(f32))
    bu = jnp.einsum('gpc,blgc->blgp', bbar, uf.astype(jnp.complex64))

    def combine(e1, e2):
        a1, x1 = e1
        a2, x2 = e2
        return a1 * a2, a2 * x1 + x2

    _, h = lax.associative_scan(combine, (jnp.broadcast_to(lbar, bu.shape), bu), axis=1)
    h0 = lax.complex(h0_re.astype(f32), h0_im.astype(f32))
    t = jnp.arange(1, L + 1, dtype=f32)
    h = h + jnp.exp(lam_dt[None] * t[:, None, None])[None] * h0[:, None]
    y = jnp.einsum('gcp,blgp->blgc', cmat, h).real + d.astype(f32) * uf
    h_last = h[:, -1]
    return y, h_last.real, h_last.imag


def mixer(u, pos, s_ret0, h0_re, h0_im, p):
    bsz, L, _ = u.shape
    f32 = jnp.float32
    proj = u @ p['w_in']
    splits = [RET_QK, 2 * RET_QK, 2 * RET_QK + RET_W, 2 * RET_QK + 2 * RET_W,
              2 * RET_QK + 2 * RET_W + S5_W, 2 * RET_QK + 2 * RET_W + S5_W + D_MODEL]
    q, k, v, rg, su, gr, gs = jnp.split(proj, splits, axis=-1)
    q = rotary(q.reshape(bsz, L, RET_HEADS, RET_DK).astype(f32), pos) * (RET_DK ** -0.5)
    k = rotary(k.reshape(bsz, L, RET_HEADS, RET_DK).astype(f32), pos)
    v = v.reshape(bsz, L, RET_HEADS, RET_DV).astype(f32)
    o, s_ret = retention(q, k, v, s_ret0.astype(f32))
    o = head_norm(o, p['ret_gn']) * jax.nn.silu(rg.astype(f32))
    b_ret = o.astype(u.dtype) @ p['ret_proj']
    y, h_re, h_im = s5_scan(su.reshape(bsz, L, S5_GROUPS, S5_GROUP_CH), h0_re, h0_im,
                            p['s5_lam_re'], p['s5_lam_im'], p['s5_b_re'], p['s5_b_im'],
                            p['s5_c_re'], p['s5_c_im'], p['s5_d'], p['s5_log_step'])
    z = jax.nn.gelu(y.reshape(bsz, L, S5_W)).astype(u.dtype)
    z = z * jax.nn.sigmoid(z @ p['glu_w'] + p['glu_b'])
    b_s5 = z @ p['s5_proj']
    m = jax.nn.sigmoid(gr) * b_ret + jax.nn.sigmoid(gs) * b_s5
    return m @ p['w_out'], s_ret, h_re, h_im


def layer_forward(x, pos, s_ret0, h0_re, h0_im, p):
    x = x + 0.5 * swiglu(rms_norm(x, p['ffn1_norm']), p['ffn1_w1'], p['ffn1_w3'], p['ffn1_w2'])
    mix, s_ret, h_re, h_im = mixer(rms_norm(x, p['mix_norm']), pos, s_ret0, h0_re, h0_im, p)
    x = x + mix
    x = x + 0.5 * swiglu(rms_norm(x, p['ffn2_norm']), p['ffn2_w1'], p['ffn2_w3'], p['ffn2_w2'])
    return x, s_ret, h_re, h_im


def setup_inputs(seed: int = 0) -> dict:
    key = jax.random.key(seed)
    ks = iter(jax.random.split(key, 40))
    f32 = jnp.float32

    def nrm(shape, scale):
        return jax.random.normal(next(ks), shape, f32) * scale

    def gain(shape):
        return 1.0 + nrm(shape, 0.01)

    d = D_MODEL
    inp = {}
    inp['x_prompt'] = nrm((BATCH, SEQ, d), 1.0)
    inp['x_sample'] = nrm((DEC_BATCH, DEC_SEQ, d), 1.0)
    inp['state_ret'] = nrm((DEPTH, DEC_BATCH, RET_HEADS, RET_DK, RET_DV), 2.0)
    inp['state_s5_re'] = nrm((DEPTH, DEC_BATCH, S5_GROUPS, S5_STATE), 0.5)
    inp['state_s5_im'] = nrm((DEPTH, DEC_BATCH, S5_GROUPS, S5_STATE), 0.5)
    inp['ffn1_norm'] = gain((DEPTH, d))
    inp['ffn1_w1'] = nrm((DEPTH, d, D_FF), d ** -0.5)
    inp['ffn1_w3'] = nrm((DEPTH, d, D_FF), d ** -0.5)
    inp['ffn1_w2'] = nrm((DEPTH, D_FF, d), D_FF ** -0.5)
    inp['mix_norm'] = gain((DEPTH, d))
    inp['w_in'] = nrm((DEPTH, d, IN_W), d ** -0.5)
    inp['ret_gn'] = gain((DEPTH, RET_W))
    inp['ret_proj'] = nrm((DEPTH, RET_W, d), RET_W ** -0.5)
    inp['s5_lam_re'] = -0.5 + nrm((DEPTH, S5_GROUPS, S5_STATE), 0.01)
    inp['s5_lam_im'] = (math.pi * jnp.arange(S5_STATE, dtype=f32))[None, None, :] + nrm((DEPTH, S5_GROUPS, S5_STATE), 0.01)
    inp['s5_b_re'] = nrm((DEPTH, S5_GROUPS, S5_STATE, S5_GROUP_CH), (2 * S5_GROUP_CH) ** -0.5)
    inp['s5_b_im'] = nrm((DEPTH, S5_GROUPS, S5_STATE, S5_GROUP_CH), (2 * S5_GROUP_CH) ** -0.5)
    inp['s5_c_re'] = nrm((DEPTH, S5_GROUPS, S5_GROUP_CH, S5_STATE), (2 * S5_STATE) ** -0.5)
    inp['s5_c_im'] = nrm((DEPTH, S5_GROUPS, S5_GROUP_CH, S5_STATE), (2 * S5_STATE) ** -0.5)
    inp['s5_d'] = nrm((DEPTH, S5_GROUPS, S5_GROUP_CH), 1.0)
    inp['s5_log_step'] = jax.random.uniform(next(ks), (DEPTH, S5_GROUPS), f32,
                                            math.log(S5_DT_MIN), math.log(S5_DT_MAX))
    inp['glu_w'] = nrm((DEPTH, S5_W, S5_W), S5_W ** -0.5)
    inp['glu_b'] = nrm((DEPTH, S5_W), 0.01)
    inp['s5_proj'] = nrm((DEPTH, S5_W, d), S5_W ** -0.5)
    inp['w_out'] = nrm((DEPTH, d, d), d ** -0.5)
    inp['ffn2_norm'] = gain((DEPTH, d))
    inp['ffn2_w1'] = nrm((DEPTH, d, D_FF), d ** -0.5)
    inp['ffn2_w3'] = nrm((DEPTH, d, D_FF), d ** -0.5)
    inp['ffn2_w2'] = nrm((DEPTH, D_FF, d), D_FF ** -0.5)
    inp['final_norm'] = gain((d,))
    return inp


def reference(x_prompt, x_sample, state_ret, state_s5_re, state_s5_im,
              ffn1_norm, ffn1_w1, ffn1_w3, ffn1_w2, mix_norm, w_in, ret_gn, ret_proj,
              s5_lam_re, s5_lam_im, s5_b_re, s5_b_im, s5_c_re, s5_c_im, s5_d, s5_log_step,
              glu_w, glu_b, s5_proj, w_out, ffn2_norm, ffn2_w1, ffn2_w3, ffn2_w2, final_norm):
    f32 = jnp.float32
    bp, lp = x_prompt.shape[:2]
    ls = x_sample.shape[1]
    pos_p = jnp.arange(lp, dtype=jnp.int32)
    pos_s = PAST_LEN + jnp.arange(ls, dtype=jnp.int32)
    zero_ret = jnp.zeros((bp, RET_HEADS, RET_DK, RET_DV), f32)
    zero_s5 = jnp.zeros((bp, S5_GROUPS, S5_STATE), f32)
    yp, ys = x_prompt, x_sample
    rp, pre, pim, rs, sre, sim = [], [], [], [], [], []
    for l in range(DEPTH):
        p = dict(ffn1_norm=ffn1_norm[l], ffn1_w1=ffn1_w1[l], ffn1_w3=ffn1_w3[l], ffn1_w2=ffn1_w2[l],
                 mix_norm=mix_norm[l], w_in=w_in[l], ret_gn=ret_gn[l], ret_proj=ret_proj[l],
                 s5_lam_re=s5_lam_re[l], s5_lam_im=s5_lam_im[l], s5_b_re=s5_b_re[l], s5_b_im=s5_b_im[l],
                 s5_c_re=s5_c_re[l], s5_c_im=s5_c_im[l], s5_d=s5_d[l], s5_log_step=s5_log_step[l],
                 glu_w=glu_w[l], glu_b=glu_b[l], s5_proj=s5_proj[l], w_out=w_out[l],
                 ffn2_norm=ffn2_norm[l], ffn2_w1=ffn2_w1[l], ffn2_w3=ffn2_w3[l], ffn2_w2=ffn2_w2[l])
        yp, s_r, h_r, h_i = layer_forward(yp, pos_p, zero_ret, zero_s5, zero_s5, p)
        rp.append(s_r); pre.append(h_r); pim.append(h_i)
        ys, s_r, h_r, h_i = layer_forward(ys, pos_s, state_ret[l], state_s5_re[l], state_s5_im[l], p)
        rs.append(s_r); sre.append(h_r); sim.append(h_i)
    yp = rms_norm(yp, final_norm)
    ys = rms_norm(ys, final_norm)
    return (yp, ys, jnp.stack(rp), jnp.stack(pre), jnp.stack(pim),
            jnp.stack(rs), jnp.stack(sre), jnp.stack(sim))
```

```python
import functools
import math

import jax
import jax.numpy as jnp
from jax import lax
from jax.experimental import pallas as pl
from jax.experimental.pallas import tpu as pltpu

F32 = jnp.float32
BF16 = jnp.bfloat16

D_MODEL = 2048
BATCH = 4
SEQ = 2048
DEPTH = 4
DEC_BATCH = 128
PAST_LEN = 16384
RET_HEADS = 8
RET_DK = 128
RET_DV = 128
RET_QK = RET_HEADS * RET_DK
RET_W = RET_HEADS * RET_DV
RET_CHUNK = 128
ROPE_BASE = 10000.0
S5_GROUPS = 64
S5_GROUP_CH = 16
S5_W = S5_GROUPS * S5_GROUP_CH
S5_STATE = 64
D_FF = 5632
IN_W = 2 * RET_QK + 2 * RET_W + S5_W + 2 * D_MODEL
NORM_EPS = 1e-6
HEAD_NORM_EPS = 1e-5

M_P = BATCH * SEQ
M_S = DEC_BATCH
M_ALL = M_P + M_S

LANES = 128
SUBLANES = 8
VMEM_LIMIT = 58 * 1024 * 1024

TM = 1040
TN_FF = 512
COL = 1024

S5_TC = 64
S5_PITCH = S5_TC + 8
S5_GB = 16
S5_NBLK = S5_GROUPS // S5_GB
S5_BLK_CH = S5_GB * S5_GROUP_CH
S5_BLK_ST = S5_GB * S5_STATE
S5_HALF_ST = S5_GROUPS * S5_STATE // 2
S5_NSLAB = S5_HALF_ST // LANES
S5_NSEQ = BATCH * 2
S5_SLAB_GROUP = 4


def _cparams(*sem):
    return pltpu.CompilerParams(dimension_semantics=sem, vmem_limit_bytes=VMEM_LIMIT)


def _rms_to_bf16(x, g):
    ms = jnp.mean(x * x, axis=-1, keepdims=True)
    return ((x * lax.rsqrt(ms + NORM_EPS)) * g).astype(BF16)


def _ffn_up_kernel(x_ref, g_ref, w1_ref, w3_ref, o_ref, xn_ref):
    @pl.when(pl.program_id(1) == 0)
    def _():
        xn_ref[...] = _rms_to_bf16(x_ref[...], g_ref[...])

    xn = xn_ref[...]
    h1 = jnp.dot(xn, w1_ref[...], preferred_element_type=F32)
    h3 = jnp.dot(xn, w3_ref[...], preferred_element_type=F32)
    o_ref[...] = ((h1 * jax.nn.sigmoid(h1)) * h3).astype(BF16)


def _ffn_up(x, g, w1, w3):
    return pl.pallas_call(
        _ffn_up_kernel,
        grid=(M_ALL // TM, D_FF // TN_FF),
        in_specs=[
            pl.BlockSpec((TM, D_MODEL), lambda i, j: (i, 0)),
            pl.BlockSpec((1, D_MODEL), lambda i, j: (0, 0)),
            pl.BlockSpec((D_MODEL, TN_FF), lambda i, j: (0, j)),
            pl.BlockSpec((D_MODEL, TN_FF), lambda i, j: (0, j)),
        ],
        out_specs=pl.BlockSpec((TM, TN_FF), lambda i, j: (i, j)),
        out_shape=jax.ShapeDtypeStruct((M_ALL, D_FF), BF16),
        scratch_shapes=[pltpu.VMEM((TM, D_MODEL), BF16)],
        compiler_params=_cparams("parallel", "arbitrary"),
        name="ffn_up",
    )(x, g, w1, w3)


def _mm_res_kernel(a_ref, w_ref, x_ref, o_ref, *, scale):
    acc = jnp.dot(a_ref[...], w_ref[...], preferred_element_type=F32)
    o_ref[...] = x_ref[...] + scale * acc


def _mm_res(a, w, x, *, scale, tn, name):
    k = a.shape[1]
    n = w.shape[1]
    return pl.pallas_call(
        functools.partial(_mm_res_kernel, scale=scale),
        grid=(M_ALL // TM, n // tn),
        in_specs=[
            pl.BlockSpec((TM, k), lambda i, j: (i, 0)),
            pl.BlockSpec((k, tn), lambda i, j: (0, j)),
            pl.BlockSpec((TM, tn), lambda i, j: (i, j)),
        ],
        out_specs=pl.BlockSpec((TM, tn), lambda i, j: (i, j)),
        out_shape=jax.ShapeDtypeStruct((M_ALL, n), F32),
        compiler_params=_cparams("parallel", "arbitrary"),
        name=name,
    )(a, w, x)


def _inproj_kernel(x_ref, g_ref, w_ref, cos_ref, sin_ref, o_ref, xn_ref):
    j = pl.program_id(1)

    @pl.when(j == 0)
    def _():
        xn_ref[...] = _rms_to_bf16(x_ref[...], g_ref[...])

    acc = jnp.dot(xn_ref[...], w_ref[...], preferred_element_type=F32)

    @pl.when(j <= 1)
    def _():
        scale = jnp.where(j == 0, RET_DK ** -0.5, 1.0).astype(F32)
        cos = cos_ref[...]
        sin = sin_ref[...]
        for h in range(RET_HEADS):
            a = acc[:, h * RET_DK:(h + 1) * RET_DK]
            r = a * cos + pltpu.roll(a, RET_DK // 2, axis=1) * sin
            o_ref[:, h * RET_DK:(h + 1) * RET_DK] = r * scale

    @pl.when((j == 2) | (j == 4))
    def _():
        o_ref[...] = acc

    @pl.when(j == 3)
    def _():
        o_ref[...] = acc * jax.nn.sigmoid(acc)

    @pl.when(j >= 5)
    def _():
        o_ref[...] = jax.nn.sigmoid(acc)


def _inproj(x, g, w, cos_t, sin_t):
    return pl.pallas_call(
        _inproj_kernel,
        grid=(M_ALL // TM, IN_W // COL),
        in_specs=[
            pl.BlockSpec((TM, D_MODEL), lambda i, j: (i, 0)),
            pl.BlockSpec((1, D_MODEL), lambda i, j: (0, 0)),
            pl.BlockSpec((D_MODEL, COL), lambda i, j: (0, j)),
            pl.BlockSpec((TM, RET_DK), lambda i, j: (i, 0)),
            pl.BlockSpec((TM, RET_DK), lambda i, j: (i, 0)),
        ],
        out_specs=pl.BlockSpec((TM, COL), lambda i, j: (i, j)),
        out_shape=jax.ShapeDtypeStruct((M_ALL, IN_W), F32),
        scratch_shapes=[pltpu.VMEM((TM, D_MODEL), BF16)],
        compiler_params=_cparams("parallel", "arbitrary"),
        name="inproj",
    )(x, g, w, cos_t, sin_t)


def _head_norm(o):
    mu = jnp.mean(o, axis=-1, keepdims=True)
    d = o - mu
    var = jnp.mean(d * d, axis=-1, keepdims=True)
    return d * lax.rsqrt(var + HEAD_NORM_EPS)


def _ret_prompt_kernel(q_ref, k_ref, v_ref, rg_ref, intra_ref, qd_ref, kd_ref, cd_ref, gn_ref,
                       o_ref, s_ref):
    intra = intra_ref[0]
    qd = qd_ref[0]
    kd = kd_ref[0]
    cd = cd_ref[0]
    gn = gn_ref[0]

    def body(c, s):
        r = pl.multiple_of(c * RET_CHUNK, RET_CHUNK)
        qc = q_ref[pl.ds(r, RET_CHUNK), :]
        kc = k_ref[pl.ds(r, RET_CHUNK), :]
        vb = v_ref[pl.ds(r, RET_CHUNK), :].astype(BF16)
        sc = lax.dot_general(qc.astype(BF16), kc.astype(BF16), (((1,), (1,)), ((), ())),
                             preferred_element_type=F32) * intra
        o = (jnp.dot(sc.astype(BF16), vb, preferred_element_type=F32)
             + jnp.dot((qc * qd).astype(BF16), s.astype(BF16), preferred_element_type=F32))
        kt = jnp.transpose(kc * kd).astype(BF16)
        s = s * cd + jnp.dot(kt, vb, preferred_element_type=F32)
        on = (_head_norm(o) * gn) * rg_ref[pl.ds(r, RET_CHUNK), :]
        o_ref[pl.ds(r, RET_CHUNK), :] = on.astype(BF16)
        return s

    s = lax.fori_loop(0, SEQ // RET_CHUNK, body, jnp.zeros((RET_DK, RET_DV), F32))
    s_ref[0, 0] = s


def _ret_prompt(proj, tabs, gn):
    intra, qd, kd, cd = tabs
    tab_spec = pl.BlockSpec((1, RET_CHUNK, RET_CHUNK), lambda b, h: (h, 0, 0))
    return pl.pallas_call(
        _ret_prompt_kernel,
        grid=(BATCH, RET_HEADS),
        in_specs=[
            pl.BlockSpec((SEQ, RET_DK), lambda b, h: (b, h)),
            pl.BlockSpec((SEQ, RET_DK), lambda b, h: (b, RET_HEADS + h)),
            pl.BlockSpec((SEQ, RET_DV), lambda b, h: (b, 2 * RET_HEADS + h)),
            pl.BlockSpec((SEQ, RET_DV), lambda b, h: (b, 3 * RET_HEADS + h)),
            tab_spec, tab_spec, tab_spec, tab_spec,
            pl.BlockSpec((1, 1, RET_DV), lambda b, h: (h, 0, 0)),
        ],
        out_specs=[
            pl.BlockSpec((SEQ, RET_DV), lambda b, h: (b, h)),
            pl.BlockSpec((1, 1, RET_DK, RET_DV), lambda b, h: (b, h, 0, 0)),
        ],
        out_shape=[
            jax.ShapeDtypeStruct((M_ALL, RET_W), BF16),
            jax.ShapeDtypeStruct((BATCH, RET_HEADS, RET_DK, RET_DV), F32),
        ],
        compiler_params=_cparams("parallel", "arbitrary"),
        name="ret_prompt",
    )(proj, proj, proj, proj, intra, qd, kd, cd, gn)


RET_SB = 8


def _col_bcast(row, eye, ones3):
    hi = row.astype(BF16)
    r1 = row - hi.astype(F32)
    mid = r1.astype(BF16)
    lo = (r1 - mid.astype(F32)).astype(BF16)
    parts = [jnp.where(eye, jnp.broadcast_to(t.astype(F32), (RET_DK, RET_DK)), 0.0).astype(BF16)
             for t in (hi, mid, lo)]
    return jnp.dot(jnp.concatenate(parts, axis=1), ones3, preferred_element_type=F32)


def _ret_sample_kernel(q_ref, k_ref, v_ref, rg_ref, s_ref, g1_ref, gn_ref, o_ref, so_ref):
    eye = (lax.broadcasted_iota(jnp.int32, (RET_DK, RET_DK), 0)
           == lax.broadcasted_iota(jnp.int32, (RET_DK, RET_DK), 1))
    ones3 = jnp.ones((3 * RET_DK, RET_DV), BF16)

    def body(b, carry):
        q = q_ref[b]
        k = k_ref[b]
        v = v_ref[b]
        rows = []
        for h in range(RET_HEADS):
            kcol = _col_bcast(k[h:h + 1, :], eye, ones3)
            qcol = _col_bcast(q[h:h + 1, :], eye, ones3)
            s_new = s_ref[b, h] * g1_ref[h] + kcol * v[h:h + 1, :]
            so_ref[b, h] = s_new
            rows.append(jnp.sum(qcol * s_new, axis=0, keepdims=True))
        o = jnp.concatenate(rows, axis=0)
        o_ref[b] = (_head_norm(o) * gn_ref[...]) * rg_ref[b]
        return carry

    lax.fori_loop(0, RET_SB, body, 0)


def _ret_sample(q, k, v, rg, state, g1, gn):
    vec_spec = pl.BlockSpec((RET_SB, RET_HEADS, RET_DK), lambda i: (i, 0, 0))
    st_spec = pl.BlockSpec((RET_SB, RET_HEADS, RET_DK, RET_DV), lambda i: (i, 0, 0, 0))
    return pl.pallas_call(
        _ret_sample_kernel,
        grid=(M_S // RET_SB,),
        in_specs=[
            vec_spec, vec_spec, vec_spec, vec_spec, st_spec,
            pl.BlockSpec((RET_HEADS, 1, RET_DV), lambda i: (0, 0, 0)),
            pl.BlockSpec((RET_HEADS, RET_DV), lambda i: (0, 0)),
        ],
        out_specs=[vec_spec, st_spec],
        out_shape=[
            jax.ShapeDtypeStruct((M_S, RET_HEADS, RET_DV), F32),
            jax.ShapeDtypeStruct((M_S, RET_HEADS, RET_DK, RET_DV), F32),
        ],
        compiler_params=_cparams("parallel"),
        name="ret_sample",
    )(q, k, v, rg, state, g1, gn)


def _s5_disc_kernel(lr_ref, li_ref, ls_ref, br_ref, bi_ref, lbr_ref, lbi_ref, bbr_ref, bbi_ref):
    lam_r = lr_ref[...]
    lam_i = li_ref[...]
    dt = jnp.exp(ls_ref[...])
    ea = jnp.exp(lam_r * dt)
    lb_r = ea * jnp.cos(lam_i * dt)
    lb_i = ea * jnp.sin(lam_i * dt)
    lbr_ref[...] = lb_r
    lbi_ref[...] = lb_i
    nr = lb_r - 1.0
    den = lam_r * lam_r + lam_i * lam_i
    cr = (nr * lam_r + lb_i * lam_i) / den
    ci = (lb_i * lam_r - nr * lam_i) / den
    b_r = br_ref[...]
    b_i = bi_ref[...]
    bbr_ref[...] = cr * b_r - ci * b_i
    bbi_ref[...] = cr * b_i + ci * b_r


def _s5_discretize(lam_re, lam_im, log_step, b_re, b_im):
    rep = lambda a: jnp.repeat(a, S5_GROUP_CH, axis=0)
    rows = S5_GROUPS * S5_GROUP_CH
    tr = lambda a: jnp.transpose(a, (0, 2, 1)).reshape(rows, S5_STATE)
    shp = jax.ShapeDtypeStruct((rows, S5_STATE), F32)
    ls = jnp.broadcast_to(log_step[:, None], (S5_GROUPS, S5_STATE))
    return pl.pallas_call(
        _s5_disc_kernel,
        out_shape=[shp, shp, shp, shp],
        name="s5_disc",
    )(rep(lam_re), rep(lam_im), rep(ls), tr(b_re), tr(b_im))


def _gelu_tanh(y):
    c = math.sqrt(2.0 / math.pi)
    return 0.5 * y * (1.0 + jnp.tanh(c * (y + 0.044715 * (y * y * y))))


def _s5_prompt_kernel(u0_ref, u1_ref, u2_ref, u3_ref, w_ref, cw_ref, lamr_ref, lami_ref, d_ref,
                      z_ref, hre_ref, him_ref, bre_ref, bim_ref):
    j = pl.program_id(0)
    u_refs = (u0_ref, u1_ref, u2_ref, u3_ref)

    @pl.when(j == 0)
    def _():
        hre_ref[...] = jnp.zeros_like(hre_ref)
        him_ref[...] = jnp.zeros_like(him_ref)

    for gh in range(2):
        for kb in range(2):
            blk = gh * 2 + kb
            c0 = blk * S5_BLK_CH
            x = jnp.concatenate([u[:, c0:c0 + S5_BLK_CH] for u in u_refs], axis=0).astype(BF16)
            res = jnp.dot(x, w_ref[blk], preferred_element_type=F32)
            for b in range(BATCH):
                r0 = (b * 2 + gh) * S5_PITCH
                part = res[b * S5_TC:(b + 1) * S5_TC]
                for s in range(S5_BLK_ST // LANES):
                    slab = kb * (S5_BLK_ST // LANES) + s
                    bre_ref[slab, r0:r0 + S5_TC, :] = part[:, s * LANES:(s + 1) * LANES]
                    bim_ref[slab, r0:r0 + S5_TC, :] = part[:, S5_BLK_ST + s * LANES:S5_BLK_ST + (s + 1) * LANES]

    for sg in range(S5_NSLAB // S5_SLAB_GROUP):
        slabs = [sg * S5_SLAB_GROUP + i for i in range(S5_SLAB_GROUP)]
        lam_r = [lamr_ref[s] for s in slabs]
        lam_i = [lami_ref[s] for s in slabs]

        def step(t, carry, slabs=slabs, lam_r=lam_r, lam_i=lam_i):
            hr, hi = carry
            nr_all = []
            ni_all = []
            for i, s in enumerate(slabs):
                idx = pl.ds(t, S5_NSEQ, stride=S5_PITCH)
                br = bre_ref[s, idx, :]
                bi = bim_ref[s, idx, :]
                nr = lam_r[i] * hr[i] - lam_i[i] * hi[i] + br
                ni = lam_r[i] * hi[i] + lam_i[i] * hr[i] + bi
                bre_ref[s, idx, :] = nr
                bim_ref[s, idx, :] = ni
                nr_all.append(nr)
                ni_all.append(ni)
            return tuple(nr_all), tuple(ni_all)

        init = (tuple(hre_ref[s] for s in slabs), tuple(him_ref[s] for s in slabs))
        hr, hi = lax.fori_loop(0, S5_TC, step, init, unroll=2)
        for i, s in enumerate(slabs):
            hre_ref[s] = hr[i]
            him_ref[s] = hi[i]

    nsl = S5_BLK_ST // LANES
    for gh in range(2):
        for kb in range(2):
            blk = gh * 2 + kb
            c0 = blk * S5_BLK_CH
            pieces = []
            for src in (bre_ref, bim_ref):
                for s in range(nsl):
                    slab = kb * nsl + s
                    pieces.append(jnp.concatenate(
                        [src[slab, (b * 2 + gh) * S5_PITCH:(b * 2 + gh) * S5_PITCH + S5_TC, :]
                         for b in range(BATCH)], axis=0))
            hcat = jnp.concatenate(pieces, axis=1).astype(BF16)
            y = jnp.dot(hcat, cw_ref[blk], preferred_element_type=F32)
            dvec = d_ref[:, c0:c0 + S5_BLK_CH]
            for b in range(BATCH):
                yb = y[b * S5_TC:(b + 1) * S5_TC] + dvec * u_refs[b][:, c0:c0 + S5_BLK_CH]
                z_ref[b, :, c0:c0 + S5_BLK_CH] = _gelu_tanh(yb)


def _s5_prompt(proj, w_blk, cw_blk, lamr_t, lami_t, dvec):
    nsteps = SEQ // S5_TC
    su_col = (2 * RET_QK + 2 * RET_W) // S5_W

    def u_spec(b):
        return pl.BlockSpec((S5_TC, S5_W), lambda j, b=b: (b * nsteps + j, su_col))

    full3 = lambda shape: pl.BlockSpec(shape, lambda j: (0, 0, 0))
    st_shape = (S5_NSLAB, S5_NSEQ, LANES)
    return pl.pallas_call(
        _s5_prompt_kernel,
        grid=(nsteps,),
        in_specs=[
            u_spec(0), u_spec(1), u_spec(2), u_spec(3),
            full3((S5_NBLK, S5_BLK_CH, 2 * S5_BLK_ST)),
            full3((S5_NBLK, 2 * S5_BLK_ST, S5_BLK_CH)),
            full3(st_shape), full3(st_shape),
            pl.BlockSpec((1, S5_W), lambda j: (0, 0)),
        ],
        out_specs=[
            pl.BlockSpec((BATCH, S5_TC, S5_W), lambda j: (0, j, 0)),
            full3(st_shape), full3(st_shape),
        ],
        out_shape=[
            jax.ShapeDtypeStruct((BATCH, SEQ, S5_W), F32),
            jax.ShapeDtypeStruct(st_shape, F32),
            jax.ShapeDtypeStruct(st_shape, F32),
        ],
        scratch_shapes=[
            pltpu.VMEM((S5_NSLAB, S5_NSEQ * S5_PITCH, LANES), F32),
            pltpu.VMEM((S5_NSLAB, S5_NSEQ * S5_PITCH, LANES), F32),
        ],
        compiler_params=_cparams("arbitrary"),
        name="s5_prompt",
    )(proj, proj, proj, proj, w_blk, cw_blk, lamr_t, lami_t, dvec)


def _s5_sample_kernel(u_ref, h0r_ref, h0i_ref, w_ref, cw_ref, lbr_ref, lbi_ref, d_ref,
                      z_ref, hr_ref, hi_ref):
    for blk in range(S5_NBLK):
        c0 = blk * S5_BLK_CH
        s0 = blk * S5_BLK_ST
        u = u_ref[:, c0:c0 + S5_BLK_CH]
        res = jnp.dot(u.astype(BF16), w_ref[blk], preferred_element_type=F32)
        lr = lbr_ref[:, s0:s0 + S5_BLK_ST]
        li = lbi_ref[:, s0:s0 + S5_BLK_ST]
        h0r = h0r_ref[:, s0:s0 + S5_BLK_ST]
        h0i = h0i_ref[:, s0:s0 + S5_BLK_ST]
        hr = lr * h0r - li * h0i + res[:, :S5_BLK_ST]
        hi = lr * h0i + li * h0r + res[:, S5_BLK_ST:]
        hr_ref[:, s0:s0 + S5_BLK_ST] = hr
        hi_ref[:, s0:s0 + S5_BLK_ST] = hi
        hcat = jnp.concatenate([hr, hi], axis=1).astype(BF16)
        y = jnp.dot(hcat, cw_ref[blk], preferred_element_type=F32) + d_ref[:, c0:c0 + S5_BLK_CH] * u
        z_ref[:, c0:c0 + S5_BLK_CH] = _gelu_tanh(y)


def _s5_sample(u, h0r, h0i, w_blk, cw_blk, lbr, lbi, dvec):
    nst = S5_GROUPS * S5_STATE
    return pl.pallas_call(
        _s5_sample_kernel,
        out_shape=[
            jax.ShapeDtypeStruct((M_S, S5_W), F32),
            jax.ShapeDtypeStruct((M_S, nst), F32),
            jax.ShapeDtypeStruct((M_S, nst), F32),
        ],
        compiler_params=pltpu.CompilerParams(vmem_limit_bytes=VMEM_LIMIT),
        name="s5_sample",
    )(u, h0r, h0i, w_blk, cw_blk, lbr, lbi, dvec)


TM_MERGE = 520


def _merge_kernel(z_ref, o_ref, gr_ref, gs_ref, gluw_ref, glub_ref, s5p_ref, retp_ref, m_ref, z2_ref):
    @pl.when(pl.program_id(1) == 0)
    def _():
        z = z_ref[...]
        t = jnp.dot(z.astype(BF16), gluw_ref[...], preferred_element_type=F32) + glub_ref[...]
        z2_ref[...] = (z * jax.nn.sigmoid(t)).astype(BF16)

    b_s5 = jnp.dot(z2_ref[...], s5p_ref[...], preferred_element_type=F32)
    b_ret = jnp.dot(o_ref[...], retp_ref[...], preferred_element_type=F32)
    m_ref[...] = (gr_ref[...] * b_ret + gs_ref[...] * b_s5).astype(BF16)


def _merge(z, o, proj, glu_w, glu_b, s5_proj, ret_proj):
    gr_col = (2 * RET_QK + 2 * RET_W + S5_W) // COL
    gs_col = gr_col + D_MODEL // COL
    return pl.pallas_call(
        _merge_kernel,
        grid=(M_ALL // TM_MERGE, D_MODEL // COL),
        in_specs=[
            pl.BlockSpec((TM_MERGE, S5_W), lambda i, j: (i, 0)),
            pl.BlockSpec((TM_MERGE, RET_W), lambda i, j: (i, 0)),
            pl.BlockSpec((TM_MERGE, COL), lambda i, j: (i, gr_col + j)),
            pl.BlockSpec((TM_MERGE, COL), lambda i, j: (i, gs_col + j)),
            pl.BlockSpec((S5_W, S5_W), lambda i, j: (0, 0)),
            pl.BlockSpec((1, S5_W), lambda i, j: (0, 0)),
            pl.BlockSpec((S5_W, COL), lambda i, j: (0, j)),
            pl.BlockSpec((RET_W, COL), lambda i, j: (0, j)),
        ],
        out_specs=pl.BlockSpec((TM_MERGE, COL), lambda i, j: (i, j)),
        out_shape=jax.ShapeDtypeStruct((M_ALL, D_MODEL), BF16),
        scratch_shapes=[pltpu.VMEM((TM_MERGE, S5_W), BF16)],
        compiler_params=_cparams("parallel", "arbitrary"),
        name="merge",
    )(z, o, proj, proj, glu_w, glu_b, s5_proj, ret_proj)


def _final_norm_kernel(x_ref, g_ref, o_ref):
    x = x_ref[...]
    ms = jnp.mean(x * x, axis=-1, keepdims=True)
    o_ref[...] = (x * lax.rsqrt(ms + NORM_EPS)) * g_ref[...]


def _final_norm(x, g):
    return pl.pallas_call(
        _final_norm_kernel,
        grid=(M_ALL // TM,),
        in_specs=[pl.BlockSpec((TM, D_MODEL), lambda i: (i, 0)),
                  pl.BlockSpec((1, D_MODEL), lambda i: (0, 0))],
        out_specs=pl.BlockSpec((TM, D_MODEL), lambda i: (i, 0)),
        out_shape=jax.ShapeDtypeStruct((M_ALL, D_MODEL), F32),
        compiler_params=_cparams("parallel"),
        name="final_norm",
    )(x, g)


def _rope_tables():
    half = RET_DK // 2
    inv = ROPE_BASE ** (-jnp.arange(half, dtype=F32) / half)
    pos = jnp.concatenate([jnp.tile(jnp.arange(SEQ, dtype=jnp.int32), BATCH),
                           jnp.full((M_S,), PAST_LEN, jnp.int32)])
    ang = pos.astype(F32)[:, None] * inv[None, :]
    cos = jnp.cos(ang)
    sin = jnp.sin(ang)
    return jnp.concatenate([cos, cos], axis=1), jnp.concatenate([-sin, sin], axis=1)


def _retention_tables():
    c = RET_CHUNK
    log_g = jnp.log1p(-jnp.exp2(-5.0 - jnp.arange(RET_HEADS, dtype=F32)))
    i = jnp.arange(c, dtype=F32)
    diff = i[:, None] - i[None, :]
    intra = jnp.where(diff[None] >= 0,
                      jnp.exp(jnp.maximum(diff, 0.0)[None] * log_g[:, None, None]), 0.0)
    q_dec = jnp.exp((i + 1.0)[None, :] * log_g[:, None])
    k_dec = jnp.exp((c - 1.0 - i)[None, :] * log_g[:, None])
    c_dec = jnp.exp(c * log_g)
    full = lambda a: jnp.broadcast_to(a, (RET_HEADS, c, RET_DV))
    g1 = jnp.broadcast_to(jnp.exp(1.0 * log_g)[:, None, None], (RET_HEADS, 1, RET_DV))
    return (intra, full(q_dec[:, :, None]), full(k_dec[:, :, None]), full(c_dec[:, None, None])), g1


def _s5_block_weights(bbr_t, bbi_t, c_re, c_im):
    eye = jnp.eye(S5_GB, dtype=F32)

    def drive(bt):
        a = bt.reshape(S5_NBLK, S5_GB, S5_GROUP_CH, S5_STATE)
        return jnp.einsum('ngcp,gh->ngchp', a, eye).reshape(S5_NBLK, S5_BLK_CH, S5_BLK_ST)

    def readout(cm):
        a = cm.reshape(S5_NBLK, S5_GB, S5_GROUP_CH, S5_STATE)
        return jnp.einsum('ngcp,gh->ngphc', a, eye).reshape(S5_NBLK, S5_BLK_ST, S5_BLK_CH)

    w_blk = jnp.concatenate([drive(bbr_t), drive(bbi_t)], axis=2).astype(BF16)
    cw_blk = jnp.concatenate([readout(c_re), readout(-c_im)], axis=1).astype(BF16)
    return w_blk, cw_blk


def _slab_tiles(lb):
    a = lb.reshape(2, S5_NSLAB, LANES).transpose(1, 0, 2)
    return jnp.tile(a, (1, BATCH, 1))


def kernel(x_prompt, x_sample, state_ret, state_s5_re, state_s5_im, ffn1_norm, ffn1_w1, ffn1_w3, ffn1_w2, mix_norm, w_in, ret_gn, ret_proj, s5_lam_re, s5_lam_im, s5_b_re, s5_b_im, s5_c_re, s5_c_im, s5_d, s5_log_step, glu_w, glu_b, s5_proj, w_out, ffn2_norm, ffn2_w1, ffn2_w3, ffn2_w2, final_norm):
    cos_t, sin_t = _rope_tables()
    ret_tabs, g1 = _retention_tables()
    x = jnp.concatenate([x_prompt.reshape(M_P, D_MODEL), x_sample.reshape(M_S, D_MODEL)], axis=0)

    ret_p, s5r_p, s5i_p, ret_s, s5r_s, s5i_s = [], [], [], [], [], []
    nst = S5_GROUPS * S5_STATE
    for l in range(DEPTH):
        bf = lambda a: a[l].astype(BF16)
        row = lambda a: a[l].reshape(1, -1)

        hid = _ffn_up(x, row(ffn1_norm), bf(ffn1_w1), bf(ffn1_w3))
        x = _mm_res(hid, bf(ffn1_w2), x, scale=0.5, tn=TN_FF, name="ffn_down")

        proj = _inproj(x, row(mix_norm), bf(w_in), cos_t, sin_t)

        o_all, s_p = _ret_prompt(proj, ret_tabs, ret_gn[l].reshape(RET_HEADS, 1, RET_DV))
        samp = lambda c0: proj[M_P:, c0:c0 + RET_QK].reshape(M_S, RET_HEADS, RET_DK)
        o_s, s_s = _ret_sample(samp(0), samp(RET_QK), samp(2 * RET_QK), samp(2 * RET_QK + RET_W),
                               state_ret[l], g1, ret_gn[l].reshape(RET_HEADS, RET_DV))
        o_all = lax.dynamic_update_slice(o_all, o_s.reshape(M_S, RET_W).astype(BF16), (M_P, 0))

        lbr_x, lbi_x, bbr_t, bbi_t = _s5_discretize(s5_lam_re[l], s5_lam_im[l], s5_log_step[l],
                                                    s5_b_re[l], s5_b_im[l])
        lbr = lbr_x[::S5_GROUP_CH]
        lbi = lbi_x[::S5_GROUP_CH]
        w_blk, cw_blk = _s5_block_weights(bbr_t, bbi_t, s5_c_re[l], s5_c_im[l])
        dvec = s5_d[l].reshape(1, S5_W)
        z_p, hre_p, him_p = _s5_prompt(proj, w_blk, cw_blk, _slab_tiles(lbr), _slab_tiles(lbi), dvec)
        su_s = proj[M_P:, 2 * RET_QK + 2 * RET_W:2 * RET_QK + 2 * RET_W + S5_W]
        z_s, hre_s, him_s = _s5_sample(su_s, state_s5_re[l].reshape(M_S, nst),
                                       state_s5_im[l].reshape(M_S, nst), w_blk, cw_blk,
                                       lbr.reshape(1, nst), lbi.reshape(1, nst), dvec)
        z = jnp.concatenate([z_p.reshape(M_P, S5_W), z_s], axis=0)

        m = _merge(z, o_all, proj, bf(glu_w), row(glu_b), bf(s5_proj), bf(ret_proj))
        x = _mm_res(m, bf(w_out), x, scale=1.0, tn=COL, name="out_proj")

        hid = _ffn_up(x, row(ffn2_norm), bf(ffn2_w1), bf(ffn2_w3))
        x = _mm_res(hid, bf(ffn2_w2), x, scale=0.5, tn=TN_FF, name="ffn_down")

        unslab = lambda a: (a.transpose(1, 0, 2).reshape(BATCH, 2, S5_HALF_ST)
                            .reshape(BATCH, S5_GROUPS, S5_STATE))
        ret_p.append(s_p)
        s5r_p.append(unslab(hre_p))
        s5i_p.append(unslab(him_p))
        ret_s.append(s_s)
        s5r_s.append(hre_s.reshape(M_S, S5_GROUPS, S5_STATE))
        s5i_s.append(him_s.reshape(M_S, S5_GROUPS, S5_STATE))

    y = _final_norm(x, final_norm.reshape(1, D_MODEL))
    return (y[:M_P].reshape(BATCH, SEQ, D_MODEL), y[M_P:].reshape(M_S, 1, D_MODEL),
            jnp.stack(ret_p), jnp.stack(s5r_p), jnp.stack(s5i_p),
            jnp.stack(ret_s), jnp.stack(s5r_s), jnp.stack(s5i_s))
```

```python
import functools
import math

import jax
import jax.numpy as jnp
from jax import lax
from jax.experimental import pallas as pl
from jax.experimental.pallas import tpu as pltpu

F32 = jnp.float32
BF16 = jnp.bfloat16

D_MODEL = 2048
BATCH = 4
SEQ = 2048
DEPTH = 4
DEC_BATCH = 128
PAST_LEN = 16384
RET_HEADS = 8
RET_DK = 128
RET_DV = 128
RET_QK = RET_HEADS * RET_DK
RET_W = RET_HEADS * RET_DV
RET_CHUNK = 128
ROPE_BASE = 10000.0
S5_GROUPS = 64
S5_GROUP_CH = 16
S5_W = S5_GROUPS * S5_GROUP_CH
S5_STATE = 64
D_FF = 5632
IN_W = 2 * RET_QK + 2 * RET_W + S5_W + 2 * D_MODEL
NORM_EPS = 1e-6
HEAD_NORM_EPS = 1e-5

M_P = BATCH * SEQ
M_S = DEC_BATCH
M_ALL = M_P + M_S

LANES = 128
SUBLANES = 8
VMEM_LIMIT = 58 * 1024 * 1024

TM = 1040
TN_FF = 512
TN_DOWN = 256
COL = 1024
TN_IN = 512

S5_TC = 64
S5_PITCH = S5_TC + 8
S5_GB = 16
S5_NBLK = S5_GROUPS // S5_GB
S5_BLK_CH = S5_GB * S5_GROUP_CH
S5_BLK_ST = S5_GB * S5_STATE
S5_HALF_ST = S5_GROUPS * S5_STATE // 2
S5_NSLAB = S5_HALF_ST // LANES
S5_NSEQ = BATCH * 2
S5_SLAB_GROUP = 4


def _cparams(*sem):
    return pltpu.CompilerParams(dimension_semantics=sem, vmem_limit_bytes=VMEM_LIMIT)


def _rms_to_bf16(x, g):
    ms = jnp.mean(x * x, axis=-1, keepdims=True)
    return ((x * lax.rsqrt(ms + NORM_EPS)) * g).astype(BF16)


def _ffn_up_kernel(x_ref, g_ref, w1_ref, w3_ref, o_ref, xn_ref):
    @pl.when(pl.program_id(1) == 0)
    def _():
        xn_ref[...] = _rms_to_bf16(x_ref[...], g_ref[...])

    xn = xn_ref[...]
    h1 = jnp.dot(xn, w1_ref[...].astype(BF16), preferred_element_type=F32)
    h3 = jnp.dot(xn, w3_ref[...].astype(BF16), preferred_element_type=F32)
    o_ref[...] = ((h1 * jax.nn.sigmoid(h1)) * h3).astype(BF16)


def _layer_spec(l, k, tn):
    return pl.BlockSpec((None, k, tn), lambda i, j: (l, 0, j))


def _layer_row_spec(l, n):
    return pl.BlockSpec((None, 1, n), lambda *_: (l, 0, 0))


def _ffn_up(x, g, w1, w3, l):
    return pl.pallas_call(
        _ffn_up_kernel,
        grid=(M_ALL // TM, D_FF // TN_FF),
        in_specs=[
            pl.BlockSpec((TM, D_MODEL), lambda i, j: (i, 0)),
            _layer_row_spec(l, D_MODEL),
            _layer_spec(l, D_MODEL, TN_FF),
            _layer_spec(l, D_MODEL, TN_FF),
        ],
        out_specs=pl.BlockSpec((TM, TN_FF), lambda i, j: (i, j)),
        out_shape=jax.ShapeDtypeStruct((M_ALL, D_FF), BF16),
        scratch_shapes=[pltpu.VMEM((TM, D_MODEL), BF16)],
        compiler_params=_cparams("parallel", "arbitrary"),
        name="ffn_up",
    )(x, g, w1, w3)


def _mm_res_kernel(a_ref, w_ref, x_ref, o_ref, *, scale):
    acc = jnp.dot(a_ref[...], w_ref[...].astype(BF16), preferred_element_type=F32)
    o_ref[...] = x_ref[...] + scale * acc


def _mm_res(a, w, x, l, *, scale, tn, name):
    k = a.shape[1]
    n = w.shape[2]
    return pl.pallas_call(
        functools.partial(_mm_res_kernel, scale=scale),
        grid=(M_ALL // TM, n // tn),
        in_specs=[
            pl.BlockSpec((TM, k), lambda i, j: (i, 0)),
            _layer_spec(l, k, tn),
            pl.BlockSpec((TM, tn), lambda i, j: (i, j)),
        ],
        out_specs=pl.BlockSpec((TM, tn), lambda i, j: (i, j)),
        out_shape=jax.ShapeDtypeStruct((M_ALL, n), F32),
        compiler_params=_cparams("parallel", "arbitrary"),
        name=name,
    )(a, w, x)


def _inproj_kernel(x_ref, g_ref, w_ref, cos_ref, sin_ref, o_ref, xn_ref):
    j = pl.program_id(1)

    @pl.when(j == 0)
    def _():
        xn_ref[...] = _rms_to_bf16(x_ref[...], g_ref[...])

    acc = jnp.dot(xn_ref[...], w_ref[...].astype(BF16), preferred_element_type=F32)
    seg = j // (COL // TN_IN)

    @pl.when(seg <= 1)
    def _():
        scale = jnp.where(seg == 0, RET_DK ** -0.5, 1.0).astype(F32)
        cos = cos_ref[...]
        sin = sin_ref[...]
        for h in range(TN_IN // RET_DK):
            a = acc[:, h * RET_DK:(h + 1) * RET_DK]
            r = a * cos + pltpu.roll(a, RET_DK // 2, axis=1) * sin
            o_ref[:, h * RET_DK:(h + 1) * RET_DK] = r * scale

    @pl.when((seg == 2) | (seg == 4))
    def _():
        o_ref[...] = acc

    @pl.when(seg == 3)
    def _():
        o_ref[...] = acc * jax.nn.sigmoid(acc)

    @pl.when(seg >= 5)
    def _():
        o_ref[...] = jax.nn.sigmoid(acc)


def _inproj(x, g, w, cos_t, sin_t, l):
    return pl.pallas_call(
        _inproj_kernel,
        grid=(M_ALL // TM, IN_W // TN_IN),
        in_specs=[
            pl.BlockSpec((TM, D_MODEL), lambda i, j: (i, 0)),
            _layer_row_spec(l, D_MODEL),
            _layer_spec(l, D_MODEL, TN_IN),
            pl.BlockSpec((TM, RET_DK), lambda i, j: (i, 0)),
            pl.BlockSpec((TM, RET_DK), lambda i, j: (i, 0)),
        ],
        out_specs=pl.BlockSpec((TM, TN_IN), lambda i, j: (i, j)),
        out_shape=jax.ShapeDtypeStruct((M_ALL, IN_W), F32),
        scratch_shapes=[pltpu.VMEM((TM, D_MODEL), BF16)],
        compiler_params=_cparams("parallel", "arbitrary"),
        name="inproj",
    )(x, g, w, cos_t, sin_t)


def _head_norm(o):
    mu = jnp.mean(o, axis=-1, keepdims=True)
    d = o - mu
    var = jnp.mean(d * d, axis=-1, keepdims=True)
    return d * lax.rsqrt(var + HEAD_NORM_EPS)


RET_RC = 2 * RET_CHUNK


def _ret_prompt_kernel(q_ref, k_ref, v_ref, rg_ref, intra_ref, qd_ref, kd_ref, cd_ref, gn_ref,
                       o_ref, s_ref, st_ref):
    step = pl.program_id(1)

    @pl.when(step == 0)
    def _():
        st_ref[...] = jnp.zeros_like(st_ref)

    for c in range(RET_RC // RET_CHUNK):
        rows = slice(c * RET_CHUNK, (c + 1) * RET_CHUNK)
        for h in range(RET_HEADS):
            cols = slice(h * RET_DK, (h + 1) * RET_DK)
            qc = q_ref[rows, cols]
            kc = k_ref[rows, cols]
            vb = v_ref[rows, cols].astype(BF16)
            s = st_ref[h]
            sc = lax.dot_general(qc.astype(BF16), kc.astype(BF16), (((1,), (1,)), ((), ())),
                                 preferred_element_type=F32) * intra_ref[h]
            o = (jnp.dot(sc.astype(BF16), vb, preferred_element_type=F32)
                 + jnp.dot((qc * qd_ref[h]).astype(BF16), s.astype(BF16),
                           preferred_element_type=F32))
            kt = jnp.transpose(kc * kd_ref[h]).astype(BF16)
            st_ref[h] = s * cd_ref[h] + jnp.dot(kt, vb, preferred_element_type=F32)
            on = (_head_norm(o) * gn_ref[h]) * rg_ref[rows, cols]
            o_ref[rows, cols] = on.astype(BF16)

    @pl.when(step == pl.num_programs(1) - 1)
    def _():
        s_ref[...] = st_ref[...]


def _ret_prompt(proj, tabs, gn, l):
    intra, qd, kd, cd = tabs
    nsteps = SEQ // RET_RC
    tab_spec = pl.BlockSpec((RET_HEADS, RET_CHUNK, RET_CHUNK), lambda b, s: (0, 0, 0))

    def seg_spec(seg):
        return pl.BlockSpec((RET_RC, COL), lambda b, s, seg=seg: (b * nsteps + s, seg))

    return pl.pallas_call(
        _ret_prompt_kernel,
        grid=(BATCH, nsteps),
        in_specs=[
            seg_spec(0), seg_spec(1), seg_spec(2), seg_spec(3),
            tab_spec, tab_spec, tab_spec, tab_spec,
            pl.BlockSpec((None, RET_HEADS, 1, RET_DV), lambda b, s: (l, 0, 0, 0)),
        ],
        out_specs=[
            pl.BlockSpec((RET_RC, RET_W), lambda b, s: (b * nsteps + s, 0)),
            pl.BlockSpec((None, RET_HEADS, RET_DK, RET_DV), lambda b, s: (b, 0, 0, 0)),
        ],
        out_shape=[
            jax.ShapeDtypeStruct((M_ALL, RET_W), BF16),
            jax.ShapeDtypeStruct((BATCH, RET_HEADS, RET_DK, RET_DV), F32),
        ],
        scratch_shapes=[pltpu.VMEM((RET_HEADS, RET_DK, RET_DV), F32)],
        compiler_params=_cparams("parallel", "arbitrary"),
        name="ret_prompt",
    )(proj, proj, proj, proj, intra, qd, kd, cd, gn)


RET_SB = 8


def _col_bcast(row, eye, ones3):
    hi = row.astype(BF16)
    r1 = row - hi.astype(F32)
    mid = r1.astype(BF16)
    lo = (r1 - mid.astype(F32)).astype(BF16)
    parts = [jnp.where(eye, jnp.broadcast_to(t.astype(F32), (RET_DK, RET_DK)), 0.0).astype(BF16)
             for t in (hi, mid, lo)]
    return jnp.dot(jnp.concatenate(parts, axis=1), ones3, preferred_element_type=F32)


def _ret_sample_kernel(q_ref, k_ref, v_ref, rg_ref, s_ref, g1_ref, gn_ref, *rest):
    o_ref, so_ref = rest[-2:]
    eye = (lax.broadcasted_iota(jnp.int32, (RET_DK, RET_DK), 0)
           == lax.broadcasted_iota(jnp.int32, (RET_DK, RET_DK), 1))
    ones3 = jnp.ones((3 * RET_DK, RET_DV), BF16)

    for b in range(RET_SB):
        row = slice(b, b + 1)
        for h in range(RET_HEADS):
            cols = slice(h * RET_DK, (h + 1) * RET_DK)
            kcol = _col_bcast(k_ref[row, cols], eye, ones3)
            qcol = _col_bcast(q_ref[row, cols], eye, ones3)
            s_new = s_ref[b, h] * g1_ref[h] + kcol * v_ref[row, cols]
            so_ref[b, h] = s_new
            o = jnp.sum(qcol * s_new, axis=0, keepdims=True)
            o_ref[row, cols] = (_head_norm(o) * gn_ref[h]) * rg_ref[row, cols]


def _ret_sample(proj, state, g1, gn, l, prev_states):
    row0 = M_P // RET_SB

    def seg_spec(seg):
        return pl.BlockSpec((RET_SB, COL), lambda i, seg=seg: (row0 + i, seg))

    st_spec = pl.BlockSpec((None, RET_SB, RET_HEADS, RET_DK, RET_DV), lambda i: (l, i, 0, 0, 0))
    in_specs = [
        seg_spec(0), seg_spec(1), seg_spec(2), seg_spec(3), st_spec,
        pl.BlockSpec((RET_HEADS, 1, RET_DV), lambda i: (0, 0, 0)),
        pl.BlockSpec((None, RET_HEADS, 1, RET_DV), lambda i: (l, 0, 0, 0)),
    ]
    args = [proj, proj, proj, proj, state, g1, gn]
    aliases = {}
    if prev_states is not None:
        in_specs.append(pl.BlockSpec(memory_space=pl.ANY))
        args.append(prev_states)
        aliases = {len(args) - 1: 1}
    return pl.pallas_call(
        _ret_sample_kernel,
        grid=(M_S // RET_SB,),
        in_specs=in_specs,
        out_specs=[pl.BlockSpec((RET_SB, RET_W), lambda i: (i, 0)), st_spec],
        out_shape=[
            jax.ShapeDtypeStruct((M_S, RET_W), F32),
            jax.ShapeDtypeStruct((DEPTH, M_S, RET_HEADS, RET_DK, RET_DV), F32),
        ],
        input_output_aliases=aliases,
        compiler_params=_cparams("parallel"),
        name="ret_sample",
    )(*args)


def _s5_disc_kernel(lr_ref, li_ref, ls_ref, br_ref, bi_ref, lbr_ref, lbi_ref, bbr_ref, bbi_ref):
    lam_r = lr_ref[...]
    lam_i = li_ref[...]
    dt = jnp.exp(ls_ref[...])
    ea = jnp.exp(lam_r * dt)
    lb_r = ea * jnp.cos(lam_i * dt)
    lb_i = ea * jnp.sin(lam_i * dt)
    lbr_ref[...] = lb_r
    lbi_ref[...] = lb_i
    nr = lb_r - 1.0
    den = lam_r * lam_r + lam_i * lam_i
    cr = (nr * lam_r + lb_i * lam_i) / den
    ci = (lb_i * lam_r - nr * lam_i) / den
    b_r = br_ref[...]
    b_i = bi_ref[...]
    bbr_ref[...] = cr * b_r - ci * b_i
    bbi_ref[...] = cr * b_i + ci * b_r


def _s5_discretize(lam_re, lam_im, log_step, b_re, b_im):
    rep = lambda a: jnp.repeat(a, S5_GROUP_CH, axis=0)
    rows = S5_GROUPS * S5_GROUP_CH
    tr = lambda a: jnp.transpose(a, (0, 2, 1)).reshape(rows, S5_STATE)
    shp = jax.ShapeDtypeStruct((rows, S5_STATE), F32)
    ls = jnp.broadcast_to(log_step[:, None], (S5_GROUPS, S5_STATE))
    return pl.pallas_call(
        _s5_disc_kernel,
        out_shape=[shp, shp, shp, shp],
        name="s5_disc",
    )(rep(lam_re), rep(lam_im), rep(ls), tr(b_re), tr(b_im))


def _gelu_tanh(y):
    c = math.sqrt(2.0 / math.pi)
    return 0.5 * y * (1.0 + jnp.tanh(c * (y + 0.044715 * (y * y * y))))


def _s5_prompt_kernel(u0_ref, u1_ref, u2_ref, u3_ref, w_ref, cw_ref, lamr_ref, lami_ref, d_ref,
                      z_ref, hre_ref, him_ref, bre_ref, bim_ref):
    j = pl.program_id(0)
    u_refs = (u0_ref, u1_ref, u2_ref, u3_ref)

    @pl.when(j == 0)
    def _():
        hre_ref[...] = jnp.zeros_like(hre_ref)
        him_ref[...] = jnp.zeros_like(him_ref)

    for gh in range(2):
        for kb in range(2):
            blk = gh * 2 + kb
            c0 = blk * S5_BLK_CH
            x = jnp.concatenate([u[:, c0:c0 + S5_BLK_CH] for u in u_refs], axis=0).astype(BF16)
            res = jnp.dot(x, w_ref[blk], preferred_element_type=F32)
            for b in range(BATCH):
                r0 = (b * 2 + gh) * S5_PITCH
                part = res[b * S5_TC:(b + 1) * S5_TC]
                for s in range(S5_BLK_ST // LANES):
                    slab = kb * (S5_BLK_ST // LANES) + s
                    bre_ref[slab, r0:r0 + S5_TC, :] = part[:, s * LANES:(s + 1) * LANES]
                    bim_ref[slab, r0:r0 + S5_TC, :] = part[:, S5_BLK_ST + s * LANES:S5_BLK_ST + (s + 1) * LANES]

    for sg in range(S5_NSLAB // S5_SLAB_GROUP):
        slabs = [sg * S5_SLAB_GROUP + i for i in range(S5_SLAB_GROUP)]
        lam_r = [lamr_ref[s] for s in slabs]
        lam_i = [lami_ref[s] for s in slabs]

        def step(t, carry, slabs=slabs, lam_r=lam_r, lam_i=lam_i):
            hr, hi = carry
            nr_all = []
            ni_all = []
            for i, s in enumerate(slabs):
                idx = pl.ds(t, S5_NSEQ, stride=S5_PITCH)
                br = bre_ref[s, idx, :]
                bi = bim_ref[s, idx, :]
                nr = lam_r[i] * hr[i] - lam_i[i] * hi[i] + br
                ni = lam_r[i] * hi[i] + lam_i[i] * hr[i] + bi
                bre_ref[s, idx, :] = nr
                bim_ref[s, idx, :] = ni
                nr_all.append(nr)
                ni_all.append(ni)
            return tuple(nr_all), tuple(ni_all)

        init = (tuple(hre_ref[s] for s in slabs), tuple(him_ref[s] for s in slabs))
        hr, hi = lax.fori_loop(0, S5_TC, step, init, unroll=2)
        for i, s in enumerate(slabs):
            hre_ref[s] = hr[i]
            him_ref[s] = hi[i]

    nsl = S5_BLK_ST // LANES
    for gh in range(2):
        for kb in range(2):
            blk = gh * 2 + kb
            c0 = blk * S5_BLK_CH
            pieces = []
            for src in (bre_ref, bim_ref):
                for s in range(nsl):
                    slab = kb * nsl + s
                    pieces.append(jnp.concatenate(
                        [src[slab, (b * 2 + gh) * S5_PITCH:(b * 2 + gh) * S5_PITCH + S5_TC, :]
                         for b in range(BATCH)], axis=0))
            hcat = jnp.concatenate(pieces, axis=1).astype(BF16)
            y = jnp.dot(hcat, cw_ref[blk], preferred_element_type=F32)
            dvec = d_ref[:, c0:c0 + S5_BLK_CH]
            for b in range(BATCH):
                yb = y[b * S5_TC:(b + 1) * S5_TC] + dvec * u_refs[b][:, c0:c0 + S5_BLK_CH]
                z_ref[b, :, c0:c0 + S5_BLK_CH] = _gelu_tanh(yb)


def _s5_prompt(proj, w_blk, cw_blk, lamr_t, lami_t, dvec):
    nsteps = SEQ // S5_TC
    su_col = (2 * RET_QK + 2 * RET_W) // S5_W

    def u_spec(b):
        return pl.BlockSpec((S5_TC, S5_W), lambda j, b=b: (b * nsteps + j, su_col))

    full3 = lambda shape: pl.BlockSpec(shape, lambda j: (0, 0, 0))
    st_shape = (S5_NSLAB, S5_NSEQ, LANES)
    return pl.pallas_call(
        _s5_prompt_kernel,
        grid=(nsteps,),
        in_specs=[
            u_spec(0), u_spec(1), u_spec(2), u_spec(3),
            full3((S5_NBLK, S5_BLK_CH, 2 * S5_BLK_ST)),
            full3((S5_NBLK, 2 * S5_BLK_ST, S5_BLK_CH)),
            full3(st_shape), full3(st_shape),
            pl.BlockSpec((1, S5_W), lambda j: (0, 0)),
        ],
        out_specs=[
            pl.BlockSpec((BATCH, S5_TC, S5_W), lambda j: (0, j, 0)),
            full3(st_shape), full3(st_shape),
        ],
        out_shape=[
            jax.ShapeDtypeStruct((BATCH, SEQ, S5_W), F32),
            jax.ShapeDtypeStruct(st_shape, F32),
            jax.ShapeDtypeStruct(st_shape, F32),
        ],
        scratch_shapes=[
            pltpu.VMEM((S5_NSLAB, S5_NSEQ * S5_PITCH, LANES), F32),
            pltpu.VMEM((S5_NSLAB, S5_NSEQ * S5_PITCH, LANES), F32),
        ],
        compiler_params=_cparams("arbitrary"),
        name="s5_prompt",
    )(proj, proj, proj, proj, w_blk, cw_blk, lamr_t, lami_t, dvec)


def _s5_sample_kernel(u_ref, h0r_ref, h0i_ref, w_ref, cw_ref, lbr_ref, lbi_ref, d_ref,
                      z_ref, hr_ref, hi_ref):
    for blk in range(S5_NBLK):
        c0 = blk * S5_BLK_CH
        s0 = blk * S5_BLK_ST
        u = u_ref[:, c0:c0 + S5_BLK_CH]
        res = jnp.dot(u.astype(BF16), w_ref[blk], preferred_element_type=F32)
        lr = lbr_ref[:, s0:s0 + S5_BLK_ST]
        li = lbi_ref[:, s0:s0 + S5_BLK_ST]
        h0r = h0r_ref[:, s0:s0 + S5_BLK_ST]
        h0i = h0i_ref[:, s0:s0 + S5_BLK_ST]
        hr = lr * h0r - li * h0i + res[:, :S5_BLK_ST]
        hi = lr * h0i + li * h0r + res[:, S5_BLK_ST:]
        hr_ref[:, s0:s0 + S5_BLK_ST] = hr
        hi_ref[:, s0:s0 + S5_BLK_ST] = hi
        hcat = jnp.concatenate([hr, hi], axis=1).astype(BF16)
        y = jnp.dot(hcat, cw_ref[blk], preferred_element_type=F32) + d_ref[:, c0:c0 + S5_BLK_CH] * u
        z_ref[:, c0:c0 + S5_BLK_CH] = _gelu_tanh(y)


def _s5_sample(u, h0r, h0i, w_blk, cw_blk, lbr, lbi, dvec):
    nst = S5_GROUPS * S5_STATE
    return pl.pallas_call(
        _s5_sample_kernel,
        out_shape=[
            jax.ShapeDtypeStruct((M_S, S5_W), F32),
            jax.ShapeDtypeStruct((M_S, nst), F32),
            jax.ShapeDtypeStruct((M_S, nst), F32),
        ],
        compiler_params=pltpu.CompilerParams(vmem_limit_bytes=VMEM_LIMIT),
        name="s5_sample",
    )(u, h0r, h0i, w_blk, cw_blk, lbr, lbi, dvec)


TM_MERGE = 520


def _merge_kernel(z_ref, o_ref, gr_ref, gs_ref, gluw_ref, glub_ref, s5p_ref, retp_ref, m_ref, z2_ref):
    @pl.when(pl.program_id(1) == 0)
    def _():
        z = z_ref[...]
        t = (jnp.dot(z.astype(BF16), gluw_ref[...].astype(BF16), preferred_element_type=F32)
             + glub_ref[...])
        z2_ref[...] = (z * jax.nn.sigmoid(t)).astype(BF16)

    b_s5 = jnp.dot(z2_ref[...], s5p_ref[...].astype(BF16), preferred_element_type=F32)
    b_ret = jnp.dot(o_ref[...], retp_ref[...].astype(BF16), preferred_element_type=F32)
    m_ref[...] = (gr_ref[...] * b_ret + gs_ref[...] * b_s5).astype(BF16)


def _merge(z, o, proj, glu_w, glu_b, s5_proj, ret_proj, l):
    gr_col = (2 * RET_QK + 2 * RET_W + S5_W) // COL
    gs_col = gr_col + D_MODEL // COL
    return pl.pallas_call(
        _merge_kernel,
        grid=(M_ALL // TM_MERGE, D_MODEL // COL),
        in_specs=[
            pl.BlockSpec((TM_MERGE, S5_W), lambda i, j: (i, 0)),
            pl.BlockSpec((TM_MERGE, RET_W), lambda i, j: (i, 0)),
            pl.BlockSpec((TM_MERGE, COL), lambda i, j: (i, gr_col + j)),
            pl.BlockSpec((TM_MERGE, COL), lambda i, j: (i, gs_col + j)),
            pl.BlockSpec((None, S5_W, S5_W), lambda i, j: (l, 0, 0)),
            _layer_row_spec(l, S5_W),
            _layer_spec(l, S5_W, COL),
            _layer_spec(l, RET_W, COL),
        ],
        out_specs=pl.BlockSpec((TM_MERGE, COL), lambda i, j: (i, j)),
        out_shape=jax.ShapeDtypeStruct((M_ALL, D_MODEL), BF16),
        scratch_shapes=[pltpu.VMEM((TM_MERGE, S5_W), BF16)],
        compiler_params=_cparams("parallel", "arbitrary"),
        name="merge",
    )(z, o, proj, proj, glu_w, glu_b, s5_proj, ret_proj)


def _final_norm_kernel(x_ref, g_ref, o_ref):
    x = x_ref[...]
    ms = jnp.mean(x * x, axis=-1, keepdims=True)
    o_ref[...] = (x * lax.rsqrt(ms + NORM_EPS)) * g_ref[...]


def _final_norm(x, g, rows, tm, block0):
    return pl.pallas_call(
        _final_norm_kernel,
        grid=(rows // tm,),
        in_specs=[pl.BlockSpec((tm, D_MODEL), lambda i: (block0 + i, 0)),
                  pl.BlockSpec((1, D_MODEL), lambda i: (0, 0))],
        out_specs=pl.BlockSpec((tm, D_MODEL), lambda i: (i, 0)),
        out_shape=jax.ShapeDtypeStruct((rows, D_MODEL), F32),
        compiler_params=_cparams("parallel"),
        name="final_norm",
    )(x, g)


def _rope_tables():
    half = RET_DK // 2
    inv = ROPE_BASE ** (-jnp.arange(half, dtype=F32) / half)
    pos = jnp.concatenate([jnp.tile(jnp.arange(SEQ, dtype=jnp.int32), BATCH),
                           jnp.full((M_S,), PAST_LEN, jnp.int32)])
    ang = pos.astype(F32)[:, None] * inv[None, :]
    cos = jnp.cos(ang)
    sin = jnp.sin(ang)
    return jnp.concatenate([cos, cos], axis=1), jnp.concatenate([-sin, sin], axis=1)


def _retention_tables():
    c = RET_CHUNK
    log_g = jnp.log1p(-jnp.exp2(-5.0 - jnp.arange(RET_HEADS, dtype=F32)))
    i = jnp.arange(c, dtype=F32)
    diff = i[:, None] - i[None, :]
    intra = jnp.where(diff[None] >= 0,
                      jnp.exp(jnp.maximum(diff, 0.0)[None] * log_g[:, None, None]), 0.0)
    q_dec = jnp.exp((i + 1.0)[None, :] * log_g[:, None])
    k_dec = jnp.exp((c - 1.0 - i)[None, :] * log_g[:, None])
    c_dec = jnp.exp(c * log_g)
    full = lambda a: jnp.broadcast_to(a, (RET_HEADS, c, RET_DV))
    g1 = jnp.broadcast_to(jnp.exp(1.0 * log_g)[:, None, None], (RET_HEADS, 1, RET_DV))
    return (intra, full(q_dec[:, :, None]), full(k_dec[:, :, None]), full(c_dec[:, None, None])), g1


def _s5_block_weights(bbr_t, bbi_t, c_re, c_im):
    eye = jnp.eye(S5_GB, dtype=F32)

    def drive(bt):
        a = bt.reshape(S5_NBLK, S5_GB, S5_GROUP_CH, S5_STATE)
        return jnp.einsum('ngcp,gh->ngchp', a, eye).reshape(S5_NBLK, S5_BLK_CH, S5_BLK_ST)

    def readout(cm):
        a = cm.reshape(S5_NBLK, S5_GB, S5_GROUP_CH, S5_STATE)
        return jnp.einsum('ngcp,gh->ngphc', a, eye).reshape(S5_NBLK, S5_BLK_ST, S5_BLK_CH)

    w_blk = jnp.concatenate([drive(bbr_t), drive(bbi_t)], axis=2).astype(BF16)
    cw_blk = jnp.concatenate([readout(c_re), readout(-c_im)], axis=1).astype(BF16)
    return w_blk, cw_blk


def _slab_tiles(lb):
    a = lb.reshape(2, S5_NSLAB, LANES).transpose(1, 0, 2)
    return jnp.tile(a, (1, BATCH, 1))


def kernel(x_prompt, x_sample, state_ret, state_s5_re, state_s5_im, ffn1_norm, ffn1_w1, ffn1_w3, ffn1_w2, mix_norm, w_in, ret_gn, ret_proj, s5_lam_re, s5_lam_im, s5_b_re, s5_b_im, s5_c_re, s5_c_im, s5_d, s5_log_step, glu_w, glu_b, s5_proj, w_out, ffn2_norm, ffn2_w1, ffn2_w3, ffn2_w2, final_norm):
    cos_t, sin_t = _rope_tables()
    ret_tabs, g1 = _retention_tables()
    x = jnp.concatenate([x_prompt.reshape(M_P, D_MODEL), x_sample.reshape(M_S, D_MODEL)], axis=0)

    ret_p, s5r_p, s5i_p, s5r_s, s5i_s = [], [], [], [], []
    ret_s = None
    nst = S5_GROUPS * S5_STATE
    rows3 = lambda a: a.reshape(DEPTH, 1, -1)
    ffn1_g, mix_g, ffn2_g, glu_b3 = rows3(ffn1_norm), rows3(mix_norm), rows3(ffn2_norm), rows3(glu_b)
    gn4 = ret_gn.reshape(DEPTH, RET_HEADS, 1, RET_DV)
    for l in range(DEPTH):
        hid = _ffn_up(x, ffn1_g, ffn1_w1, ffn1_w3, l)
        x = _mm_res(hid, ffn1_w2, x, l, scale=0.5, tn=TN_DOWN, name="ffn_down")

        proj = _inproj(x, mix_g, w_in, cos_t, sin_t, l)

        o_all, s_p = _ret_prompt(proj, ret_tabs, gn4, l)
        o_s, ret_s = _ret_sample(proj, state_ret, g1, gn4, l, ret_s)
        o_all = lax.dynamic_update_slice(o_all, o_s.astype(BF16), (M_P, 0))

        lbr_x, lbi_x, bbr_t, bbi_t = _s5_discretize(s5_lam_re[l], s5_lam_im[l], s5_log_step[l],
                                                    s5_b_re[l], s5_b_im[l])
        lbr = lbr_x[::S5_GROUP_CH]
        lbi = lbi_x[::S5_GROUP_CH]
        w_blk, cw_blk = _s5_block_weights(bbr_t, bbi_t, s5_c_re[l], s5_c_im[l])
        dvec = s5_d[l].reshape(1, S5_W)
        z_p, hre_p, him_p = _s5_prompt(proj, w_blk, cw_blk, _slab_tiles(lbr), _slab_tiles(lbi), dvec)
        su_s = proj[M_P:, 2 * RET_QK + 2 * RET_W:2 * RET_QK + 2 * RET_W + S5_W]
        z_s, hre_s, him_s = _s5_sample(su_s, state_s5_re[l].reshape(M_S, nst),
                                       state_s5_im[l].reshape(M_S, nst), w_blk, cw_blk,
                                       lbr.reshape(1, nst), lbi.reshape(1, nst), dvec)
        z = jnp.concatenate([z_p.reshape(M_P, S5_W), z_s], axis=0)

        m = _merge(z, o_all, proj, glu_w, glu_b3, s5_proj, ret_proj, l)
        x = _mm_res(m, w_out, x, l, scale=1.0, tn=TN_IN, name="out_proj")

        hid = _ffn_up(x, ffn2_g, ffn2_w1, ffn2_w3, l)
        x = _mm_res(hid, ffn2_w2, x, l, scale=0.5, tn=TN_DOWN, name="ffn_down")

        unslab = lambda a: (a.transpose(1, 0, 2).reshape(BATCH, 2, S5_HALF_ST)
                            .reshape(BATCH, S5_GROUPS, S5_STATE))
        ret_p.append(s_p)
        s5r_p.append(unslab(hre_p))
        s5i_p.append(unslab(him_p))
        s5r_s.append(hre_s.reshape(M_S, S5_GROUPS, S5_STATE))
        s5i_s.append(him_s.reshape(M_S, S5_GROUPS, S5_STATE))

    g_fin = final_norm.reshape(1, D_MODEL)
    y_p = _final_norm(x, g_fin, M_P, 1024, 0)
    y_s = _final_norm(x, g_fin, M_S, M_S, M_P // M_S)
    return (y_p.reshape(BATCH, SEQ, D_MODEL), y_s.reshape(M_S, 1, D_MODEL),
            jnp.stack(ret_p), jnp.stack(s5r_p), jnp.stack(s5i_p),
            ret_s, jnp.stack(s5r_s), jnp.stack(s5i_s))
```

```python
import functools
import math

import jax
import jax.numpy as jnp
from jax import lax
from jax.experimental import pallas as pl
from jax.experimental.pallas import tpu as pltpu

F32 = jnp.float32
BF16 = jnp.bfloat16

D_MODEL = 2048
BATCH = 4
SEQ = 2048
DEPTH = 4
DEC_BATCH = 128
PAST_LEN = 16384
RET_HEADS = 8
RET_DK = 128
RET_DV = 128
RET_QK = RET_HEADS * RET_DK
RET_W = RET_HEADS * RET_DV
RET_CHUNK = 128
ROPE_BASE = 10000.0
S5_GROUPS = 64
S5_GROUP_CH = 16
S5_W = S5_GROUPS * S5_GROUP_CH
S5_STATE = 64
D_FF = 5632
IN_W = 2 * RET_QK + 2 * RET_W + S5_W + 2 * D_MODEL
NORM_EPS = 1e-6
HEAD_NORM_EPS = 1e-5

M_P = BATCH * SEQ
M_S = DEC_BATCH
M_ALL = M_P + M_S

LANES = 128
SUBLANES = 8
VMEM_LIMIT = 58 * 1024 * 1024

TM = 1040
TN_FF = 512
TM_DOWN = 832
TN_DOWN = 512
COL = 1024
TN_IN = 1024

S5_TC = 64
S5_PAD = SUBLANES // 2
S5_PITCH = S5_TC + S5_PAD
S5_GB = 16
S5_NBLK = S5_GROUPS // S5_GB
S5_BLK_CH = S5_GB * S5_GROUP_CH
S5_BLK_ST = S5_GB * S5_STATE
S5_HALF_ST = S5_GROUPS * S5_STATE // 2
S5_NSLAB = S5_HALF_ST // LANES
S5_NSEQ = BATCH * 2
S5_SLAB_GROUP = 4


def _cparams(*sem):
    return pltpu.CompilerParams(dimension_semantics=sem, vmem_limit_bytes=VMEM_LIMIT)


def _rms_to_bf16(x, g):
    ms = jnp.mean(x * x, axis=-1, keepdims=True)
    return ((x * lax.rsqrt(ms + NORM_EPS)) * g).astype(BF16)


def _ffn_up_kernel(x_ref, g_ref, w1_ref, w3_ref, o_ref, xn_ref):
    @pl.when(pl.program_id(1) == 0)
    def _():
        xn_ref[...] = _rms_to_bf16(x_ref[...], g_ref[...])

    xn = xn_ref[...]
    h1 = jnp.dot(xn, w1_ref[...].astype(BF16), preferred_element_type=F32)
    h3 = jnp.dot(xn, w3_ref[...].astype(BF16), preferred_element_type=F32)
    o_ref[...] = ((h1 * jax.nn.sigmoid(h1)) * h3).astype(BF16)


def _layer_spec(l, k, tn):
    return pl.BlockSpec((None, k, tn), lambda i, j: (l, 0, j))


def _layer_row_spec(l, n):
    return pl.BlockSpec((None, 1, n), lambda *_: (l, 0, 0))


def _ffn_up(x, g, w1, w3, l):
    return pl.pallas_call(
        _ffn_up_kernel,
        grid=(M_ALL // TM, D_FF // TN_FF),
        in_specs=[
            pl.BlockSpec((TM, D_MODEL), lambda i, j: (i, 0)),
            _layer_row_spec(l, D_MODEL),
            _layer_spec(l, D_MODEL, TN_FF),
            _layer_spec(l, D_MODEL, TN_FF),
        ],
        out_specs=pl.BlockSpec((TM, TN_FF), lambda i, j: (i, j)),
        out_shape=jax.ShapeDtypeStruct((M_ALL, D_FF), BF16),
        scratch_shapes=[pltpu.VMEM((TM, D_MODEL), BF16)],
        compiler_params=_cparams("parallel", "arbitrary"),
        name="ffn_up",
    )(x, g, w1, w3)


def _mm_res_kernel(a_ref, w_ref, x_ref, o_ref, *, scale):
    acc = jnp.dot(a_ref[...], w_ref[...].astype(BF16), preferred_element_type=F32)
    o_ref[...] = x_ref[...] + scale * acc


def _mm_res(a, w, x, l, *, scale, tm, tn, name):
    k = a.shape[1]
    n = w.shape[2]
    return pl.pallas_call(
        functools.partial(_mm_res_kernel, scale=scale),
        grid=(M_ALL // tm, n // tn),
        in_specs=[
            pl.BlockSpec((tm, k), lambda i, j: (i, 0)),
            _layer_spec(l, k, tn),
            pl.BlockSpec((tm, tn), lambda i, j: (i, j)),
        ],
        out_specs=pl.BlockSpec((tm, tn), lambda i, j: (i, j)),
        out_shape=jax.ShapeDtypeStruct((M_ALL, n), F32),
        compiler_params=_cparams("parallel", "arbitrary"),
        name=name,
    )(a, w, x)


def _inproj_kernel(x_ref, g_ref, w_ref, cos_ref, sin_ref, o_ref, xn_ref):
    j = pl.program_id(1)

    @pl.when(j == 0)
    def _():
        xn_ref[...] = _rms_to_bf16(x_ref[...], g_ref[...])

    acc = jnp.dot(xn_ref[...], w_ref[...].astype(BF16), preferred_element_type=F32)
    seg = j // (COL // TN_IN)

    @pl.when(seg <= 1)
    def _():
        scale = jnp.where(seg == 0, RET_DK ** -0.5, 1.0).astype(F32)
        cos = cos_ref[...]
        sin = sin_ref[...]
        for h in range(TN_IN // RET_DK):
            a = acc[:, h * RET_DK:(h + 1) * RET_DK]
            r = a * cos + pltpu.roll(a, RET_DK // 2, axis=1) * sin
            o_ref[:, h * RET_DK:(h + 1) * RET_DK] = r * scale

    @pl.when((seg == 2) | (seg == 4))
    def _():
        o_ref[...] = acc

    @pl.when(seg == 3)
    def _():
        o_ref[...] = acc * jax.nn.sigmoid(acc)

    @pl.when(seg >= 5)
    def _():
        o_ref[...] = jax.nn.sigmoid(acc)


def _inproj(x, g, w, cos_t, sin_t, l):
    return pl.pallas_call(
        _inproj_kernel,
        grid=(M_ALL // TM, IN_W // TN_IN),
        in_specs=[
            pl.BlockSpec((TM, D_MODEL), lambda i, j: (i, 0), pipeline_mode=pl.Buffered(1)),
            _layer_row_spec(l, D_MODEL),
            _layer_spec(l, D_MODEL, TN_IN),
            pl.BlockSpec((TM, RET_DK), lambda i, j: (i, 0)),
            pl.BlockSpec((TM, RET_DK), lambda i, j: (i, 0)),
        ],
        out_specs=pl.BlockSpec((TM, TN_IN), lambda i, j: (i, j)),
        out_shape=jax.ShapeDtypeStruct((M_ALL, IN_W), F32),
        scratch_shapes=[pltpu.VMEM((TM, D_MODEL), BF16)],
        compiler_params=_cparams("parallel", "arbitrary"),
        name="inproj",
    )(x, g, w, cos_t, sin_t)


def _head_norm(o):
    mu = jnp.mean(o, axis=-1, keepdims=True)
    d = o - mu
    var = jnp.mean(d * d, axis=-1, keepdims=True)
    return d * lax.rsqrt(var + HEAD_NORM_EPS)


RET_RC = 4 * RET_CHUNK


def _ret_prompt_kernel(q_ref, k_ref, v_ref, rg_ref, intra_ref, qd_ref, kd_ref, cd_ref, gn_ref,
                       o_ref, s_ref, st_ref):
    step = pl.program_id(1)

    @pl.when(step == 0)
    def _():
        st_ref[...] = jnp.zeros_like(st_ref)

    for c in range(RET_RC // RET_CHUNK):
        rows = slice(c * RET_CHUNK, (c + 1) * RET_CHUNK)
        for h in range(RET_HEADS):
            cols = slice(h * RET_DK, (h + 1) * RET_DK)
            qc = q_ref[rows, cols]
            kc = k_ref[rows, cols]
            vb = v_ref[rows, cols].astype(BF16)
            s = st_ref[h]
            sc = lax.dot_general(qc.astype(BF16), kc.astype(BF16), (((1,), (1,)), ((), ())),
                                 preferred_element_type=F32) * intra_ref[h]
            o = (jnp.dot(sc.astype(BF16), vb, preferred_element_type=F32)
                 + jnp.dot((qc * qd_ref[h]).astype(BF16), s.astype(BF16),
                           preferred_element_type=F32))
            kt = jnp.transpose(kc * kd_ref[h]).astype(BF16)
            st_ref[h] = s * cd_ref[h] + jnp.dot(kt, vb, preferred_element_type=F32)
            on = (_head_norm(o) * gn_ref[h]) * rg_ref[rows, cols]
            o_ref[rows, cols] = on.astype(BF16)

    @pl.when(step == pl.num_programs(1) - 1)
    def _():
        s_ref[...] = st_ref[...]


def _ret_prompt(proj, tabs, gn, l):
    intra, qd, kd, cd = tabs
    nsteps = SEQ // RET_RC
    tab_spec = pl.BlockSpec((RET_HEADS, RET_CHUNK, RET_CHUNK), lambda b, s: (0, 0, 0))

    def seg_spec(seg):
        return pl.BlockSpec((RET_RC, COL), lambda b, s, seg=seg: (b * nsteps + s, seg))

    return pl.pallas_call(
        _ret_prompt_kernel,
        grid=(BATCH, nsteps),
        in_specs=[
            seg_spec(0), seg_spec(1), seg_spec(2), seg_spec(3),
            tab_spec, tab_spec, tab_spec, tab_spec,
            pl.BlockSpec((None, RET_HEADS, 1, RET_DV), lambda b, s: (l, 0, 0, 0)),
        ],
        out_specs=[
            pl.BlockSpec((RET_RC, RET_W), lambda b, s: (b * nsteps + s, 0)),
            pl.BlockSpec((None, RET_HEADS, RET_DK, RET_DV), lambda b, s: (b, 0, 0, 0)),
        ],
        out_shape=[
            jax.ShapeDtypeStruct((M_ALL, RET_W), BF16),
            jax.ShapeDtypeStruct((BATCH, RET_HEADS, RET_DK, RET_DV), F32),
        ],
        scratch_shapes=[pltpu.VMEM((RET_HEADS, RET_DK, RET_DV), F32)],
        compiler_params=_cparams("parallel", "arbitrary"),
        name="ret_prompt",
    )(proj, proj, proj, proj, intra, qd, kd, cd, gn)


RET_SB = 8


def _col_bcast(row, eye, ones3):
    hi = row.astype(BF16)
    r1 = row - hi.astype(F32)
    mid = r1.astype(BF16)
    lo = (r1 - mid.astype(F32)).astype(BF16)
    parts = [jnp.where(eye, jnp.broadcast_to(t.astype(F32), (RET_DK, RET_DK)), 0.0).astype(BF16)
             for t in (hi, mid, lo)]
    return jnp.dot(jnp.concatenate(parts, axis=1), ones3, preferred_element_type=F32)


def _ret_sample_kernel(q_ref, k_ref, v_ref, rg_ref, s_ref, g1_ref, gn_ref, *rest):
    o_ref, so_ref = rest[-2:]
    eye = (lax.broadcasted_iota(jnp.int32, (RET_DK, RET_DK), 0)
           == lax.broadcasted_iota(jnp.int32, (RET_DK, RET_DK), 1))
    ones3 = jnp.ones((3 * RET_DK, RET_DV), BF16)

    for b in range(RET_SB):
        row = slice(b, b + 1)
        for h in range(RET_HEADS):
            cols = slice(h * RET_DK, (h + 1) * RET_DK)
            kcol = _col_bcast(k_ref[row, cols], eye, ones3)
            qcol = _col_bcast(q_ref[row, cols], eye, ones3)
            s_new = s_ref[b, h] * g1_ref[h] + kcol * v_ref[row, cols]
            so_ref[b, h] = s_new
            o = jnp.sum(qcol * s_new, axis=0, keepdims=True)
            o_ref[row, cols] = (_head_norm(o) * gn_ref[h]) * rg_ref[row, cols]


def _ret_sample(proj, state, g1, gn, l, prev_states):
    row0 = M_P // RET_SB

    def seg_spec(seg):
        return pl.BlockSpec((RET_SB, COL), lambda i, seg=seg: (row0 + i, seg))

    st_spec = pl.BlockSpec((None, RET_SB, RET_HEADS, RET_DK, RET_DV), lambda i: (l, i, 0, 0, 0))
    in_specs = [
        seg_spec(0), seg_spec(1), seg_spec(2), seg_spec(3), st_spec,
        pl.BlockSpec((RET_HEADS, 1, RET_DV), lambda i: (0, 0, 0)),
        pl.BlockSpec((None, RET_HEADS, 1, RET_DV), lambda i: (l, 0, 0, 0)),
    ]
    args = [proj, proj, proj, proj, state, g1, gn]
    aliases = {}
    if prev_states is not None:
        in_specs.append(pl.BlockSpec(memory_space=pl.ANY))
        args.append(prev_states)
        aliases = {len(args) - 1: 1}
    return pl.pallas_call(
        _ret_sample_kernel,
        grid=(M_S // RET_SB,),
        in_specs=in_specs,
        out_specs=[pl.BlockSpec((RET_SB, RET_W), lambda i: (i, 0)), st_spec],
        out_shape=[
            jax.ShapeDtypeStruct((M_S, RET_W), F32),
            jax.ShapeDtypeStruct((DEPTH, M_S, RET_HEADS, RET_DK, RET_DV), F32),
        ],
        input_output_aliases=aliases,
        compiler_params=_cparams("parallel"),
        name="ret_sample",
    )(*args)


def _s5_disc_kernel(lr_ref, li_ref, ls_ref, br_ref, bi_ref, lbr_ref, lbi_ref, bbr_ref, bbi_ref):
    lam_r = lr_ref[...]
    lam_i = li_ref[...]
    dt = jnp.exp(ls_ref[...])
    ea = jnp.exp(lam_r * dt)
    lb_r = ea * jnp.cos(lam_i * dt)
    lb_i = ea * jnp.sin(lam_i * dt)
    lbr_ref[...] = lb_r
    lbi_ref[...] = lb_i
    nr = lb_r - 1.0
    den = lam_r * lam_r + lam_i * lam_i
    cr = (nr * lam_r + lb_i * lam_i) / den
    ci = (lb_i * lam_r - nr * lam_i) / den
    b_r = br_ref[...]
    b_i = bi_ref[...]
    bbr_ref[...] = cr * b_r - ci * b_i
    bbi_ref[...] = cr * b_i + ci * b_r


def _s5_discretize(lam_re, lam_im, log_step, b_re, b_im):
    rows = DEPTH * S5_GROUPS * S5_GROUP_CH
    rep = lambda a: jnp.repeat(a.reshape(DEPTH * S5_GROUPS, S5_STATE), S5_GROUP_CH, axis=0)
    tr = lambda a: jnp.transpose(a, (0, 1, 3, 2)).reshape(rows, S5_STATE)
    shp = jax.ShapeDtypeStruct((rows, S5_STATE), F32)
    ls = jnp.broadcast_to(log_step[:, :, None], (DEPTH, S5_GROUPS, S5_STATE))
    spec = pl.BlockSpec((rows // DEPTH, S5_STATE), lambda l: (l, 0))
    return pl.pallas_call(
        _s5_disc_kernel,
        grid=(DEPTH,),
        in_specs=[spec] * 5,
        out_specs=[spec] * 4,
        out_shape=[shp, shp, shp, shp],
        compiler_params=_cparams("parallel"),
        name="s5_disc",
    )(rep(lam_re), rep(lam_im), rep(ls), tr(b_re), tr(b_im))


def _gelu_tanh(y):
    c = math.sqrt(2.0 / math.pi)
    return 0.5 * y * (1.0 + jnp.tanh(c * (y + 0.044715 * (y * y * y))))


def _s5_prompt_kernel(u0_ref, u1_ref, u2_ref, u3_ref, w_ref, cw_ref, lamr_ref, lami_ref, d_ref,
                      z_ref, hre_ref, him_ref, bre_ref, bim_ref):
    j = pl.program_id(0)
    u_refs = (u0_ref, u1_ref, u2_ref, u3_ref)

    @pl.when(j == 0)
    def _():
        hre_ref[...] = jnp.zeros_like(hre_ref)
        him_ref[...] = jnp.zeros_like(him_ref)

    nsl = S5_BLK_ST // LANES
    zpad = jnp.zeros((S5_PAD, S5_BLK_CH), F32)

    for gh in range(2):
        seg = S5_TC + 2 * S5_PAD * gh
        for kb in range(2):
            blk = gh * 2 + kb
            c0 = blk * S5_BLK_CH
            parts = []
            for u in u_refs:
                ub = u[:, c0:c0 + S5_BLK_CH]
                parts.extend([zpad, ub, zpad] if gh else [ub])
            x = jnp.concatenate(parts, axis=0).astype(BF16)
            res = jnp.dot(x, w_ref[blk], preferred_element_type=F32)
            for b in range(BATCH):
                r0 = (b * 2 + gh) * S5_PITCH - S5_PAD * gh
                part = res[b * seg:(b + 1) * seg]
                for s in range(nsl):
                    slab = kb * nsl + s
                    bre_ref[slab, r0:r0 + seg, :] = part[:, s * LANES:(s + 1) * LANES]
                    bim_ref[slab, r0:r0 + seg, :] = part[:, S5_BLK_ST + s * LANES:S5_BLK_ST + (s + 1) * LANES]

    for sg in range(S5_NSLAB // S5_SLAB_GROUP):
        slabs = [sg * S5_SLAB_GROUP + i for i in range(S5_SLAB_GROUP)]
        lam_r = [lamr_ref[s] for s in slabs]
        lam_i = [lami_ref[s] for s in slabs]

        def step(t, carry, slabs=slabs, lam_r=lam_r, lam_i=lam_i):
            hr, hi = carry
            nr_all = []
            ni_all = []
            for i, s in enumerate(slabs):
                idx = pl.ds(t, S5_NSEQ, stride=S5_PITCH)
                br = bre_ref[s, idx, :]
                bi = bim_ref[s, idx, :]
                nr = lam_r[i] * hr[i] - lam_i[i] * hi[i] + br
                ni = lam_r[i] * hi[i] + lam_i[i] * hr[i] + bi
                bre_ref[s, idx, :] = nr
                bim_ref[s, idx, :] = ni
                nr_all.append(nr)
                ni_all.append(ni)
            return tuple(nr_all), tuple(ni_all)

        init = (tuple(hre_ref[s] for s in slabs), tuple(him_ref[s] for s in slabs))
        hr, hi = lax.fori_loop(0, S5_TC, step, init, unroll=2)
        for i, s in enumerate(slabs):
            hre_ref[s] = hr[i]
            him_ref[s] = hi[i]

    for gh in range(2):
        seg = S5_TC + 2 * S5_PAD * gh
        for kb in range(2):
            blk = gh * 2 + kb
            c0 = blk * S5_BLK_CH
            pieces = []
            for src in (bre_ref, bim_ref):
                for s in range(nsl):
                    slab = kb * nsl + s
                    starts = [(b * 2 + gh) * S5_PITCH - S5_PAD * gh for b in range(BATCH)]
                    pieces.append(jnp.concatenate(
                        [src[slab, r0:r0 + seg, :] for r0 in starts], axis=0))
            hcat = jnp.concatenate(pieces, axis=1).astype(BF16)
            y = jnp.dot(hcat, cw_ref[blk], preferred_element_type=F32)
            dvec = d_ref[:, c0:c0 + S5_BLK_CH]
            for b in range(BATCH):
                lo = b * seg + S5_PAD * gh
                yb = y[lo:lo + S5_TC] + dvec * u_refs[b][:, c0:c0 + S5_BLK_CH]
                z_ref[b, :, c0:c0 + S5_BLK_CH] = _gelu_tanh(yb)


def _s5_prompt(proj, w_blk, cw_blk, lamr_t, lami_t, dvec, l):
    nsteps = SEQ // S5_TC
    su_col = (2 * RET_QK + 2 * RET_W) // S5_W

    def u_spec(b):
        return pl.BlockSpec((S5_TC, S5_W), lambda j, b=b: (b * nsteps + j, su_col))

    layer4 = lambda shape: pl.BlockSpec((None,) + shape, lambda j: (l, 0, 0, 0))
    full3 = lambda shape: pl.BlockSpec(shape, lambda j: (0, 0, 0))
    st_shape = (S5_NSLAB, S5_NSEQ, LANES)
    return pl.pallas_call(
        _s5_prompt_kernel,
        grid=(nsteps,),
        in_specs=[
            u_spec(0), u_spec(1), u_spec(2), u_spec(3),
            layer4((S5_NBLK, S5_BLK_CH, 2 * S5_BLK_ST)),
            layer4((S5_NBLK, 2 * S5_BLK_ST, S5_BLK_CH)),
            layer4(st_shape), layer4(st_shape),
            _layer_row_spec(l, S5_W),
        ],
        out_specs=[
            pl.BlockSpec((BATCH, S5_TC, S5_W), lambda j: (0, j, 0)),
            full3(st_shape), full3(st_shape),
        ],
        out_shape=[
            jax.ShapeDtypeStruct((BATCH + 1, SEQ, S5_W), F32),
            jax.ShapeDtypeStruct(st_shape, F32),
            jax.ShapeDtypeStruct(st_shape, F32),
        ],
        scratch_shapes=[
            pltpu.VMEM((S5_NSLAB, S5_NSEQ * S5_PITCH, LANES), F32),
            pltpu.VMEM((S5_NSLAB, S5_NSEQ * S5_PITCH, LANES), F32),
        ],
        compiler_params=_cparams("arbitrary"),
        name="s5_prompt",
    )(proj, proj, proj, proj, w_blk, cw_blk, lamr_t, lami_t, dvec)


def _s5_sample_kernel(u_ref, h0r_ref, h0i_ref, w_ref, cw_ref, lbr_ref, lbi_ref, d_ref,
                      z_ref, hr_ref, hi_ref):
    for blk in range(S5_NBLK):
        c0 = blk * S5_BLK_CH
        s0 = blk * S5_BLK_ST
        u = u_ref[:, c0:c0 + S5_BLK_CH]
        res = jnp.dot(u.astype(BF16), w_ref[blk], preferred_element_type=F32)
        lr = lbr_ref[:, s0:s0 + S5_BLK_ST]
        li = lbi_ref[:, s0:s0 + S5_BLK_ST]
        h0r = h0r_ref[:, s0:s0 + S5_BLK_ST]
        h0i = h0i_ref[:, s0:s0 + S5_BLK_ST]
        hr = lr * h0r - li * h0i + res[:, :S5_BLK_ST]
        hi = lr * h0i + li * h0r + res[:, S5_BLK_ST:]
        hr_ref[:, s0:s0 + S5_BLK_ST] = hr
        hi_ref[:, s0:s0 + S5_BLK_ST] = hi
        hcat = jnp.concatenate([hr, hi], axis=1).astype(BF16)
        y = jnp.dot(hcat, cw_ref[blk], preferred_element_type=F32) + d_ref[:, c0:c0 + S5_BLK_CH] * u
        z_ref[:, c0:c0 + S5_BLK_CH] = _gelu_tanh(y)


def _s5_sample(proj, h0r, h0i, w_blk, cw_blk, lbr, lbi, dvec, l):
    nst = S5_GROUPS * S5_STATE
    su_col = (2 * RET_QK + 2 * RET_W) // S5_W
    layer3 = lambda shape: pl.BlockSpec((None,) + shape, lambda i: (l, 0, 0))
    layer4 = lambda shape: pl.BlockSpec((None,) + shape, lambda i: (l, 0, 0, 0))
    whole = lambda shape: pl.BlockSpec(shape, lambda i: (0, 0))
    return pl.pallas_call(
        _s5_sample_kernel,
        grid=(1,),
        in_specs=[
            pl.BlockSpec((M_S, S5_W), lambda i: (M_P // M_S, su_col)),
            layer3((M_S, nst)), layer3((M_S, nst)),
            layer4((S5_NBLK, S5_BLK_CH, 2 * S5_BLK_ST)),
            layer4((S5_NBLK, 2 * S5_BLK_ST, S5_BLK_CH)),
            layer3((1, nst)), layer3((1, nst)),
            _layer_row_spec(l, S5_W),
        ],
        out_specs=[whole((M_S, S5_W)), whole((M_S, nst)), whole((M_S, nst))],
        out_shape=[
            jax.ShapeDtypeStruct((M_S, S5_W), F32),
            jax.ShapeDtypeStruct((M_S, nst), F32),
            jax.ShapeDtypeStruct((M_S, nst), F32),
        ],
        compiler_params=_cparams("arbitrary"),
        name="s5_sample",
    )(proj, h0r, h0i, w_blk, cw_blk, lbr, lbi, dvec)


TM_MERGE = 520


def _merge_kernel(z_ref, o_ref, gr_ref, gs_ref, gluw_ref, glub_ref, s5p_ref, retp_ref, m_ref, z2_ref):
    @pl.when(pl.program_id(1) == 0)
    def _():
        z = z_ref[...]
        t = (jnp.dot(z.astype(BF16), gluw_ref[...].astype(BF16), preferred_element_type=F32)
             + glub_ref[...])
        z2_ref[...] = (z * jax.nn.sigmoid(t)).astype(BF16)

    b_s5 = jnp.dot(z2_ref[...], s5p_ref[...].astype(BF16), preferred_element_type=F32)
    b_ret = jnp.dot(o_ref[...], retp_ref[...].astype(BF16), preferred_element_type=F32)
    m_ref[...] = (gr_ref[...] * b_ret + gs_ref[...] * b_s5).astype(BF16)


def _merge(z, o, proj, glu_w, glu_b, s5_proj, ret_proj, l):
    gr_col = (2 * RET_QK + 2 * RET_W + S5_W) // COL
    gs_col = gr_col + D_MODEL // COL
    return pl.pallas_call(
        _merge_kernel,
        grid=(M_ALL // TM_MERGE, D_MODEL // COL),
        in_specs=[
            pl.BlockSpec((TM_MERGE, S5_W), lambda i, j: (i, 0)),
            pl.BlockSpec((TM_MERGE, RET_W), lambda i, j: (i, 0)),
            pl.BlockSpec((TM_MERGE, COL), lambda i, j: (i, gr_col + j)),
            pl.BlockSpec((TM_MERGE, COL), lambda i, j: (i, gs_col + j)),
            pl.BlockSpec((None, S5_W, S5_W), lambda i, j: (l, 0, 0)),
            _layer_row_spec(l, S5_W),
            _layer_spec(l, S5_W, COL),
            _layer_spec(l, RET_W, COL),
        ],
        out_specs=pl.BlockSpec((TM_MERGE, COL), lambda i, j: (i, j)),
        out_shape=jax.ShapeDtypeStruct((M_ALL, D_MODEL), BF16),
        scratch_shapes=[pltpu.VMEM((TM_MERGE, S5_W), BF16)],
        compiler_params=_cparams("parallel", "arbitrary"),
        name="merge",
    )(z, o, proj, proj, glu_w, glu_b, s5_proj, ret_proj)


def _final_norm_kernel(x_ref, g_ref, o_ref):
    x = x_ref[...]
    ms = jnp.mean(x * x, axis=-1, keepdims=True)
    o_ref[...] = (x * lax.rsqrt(ms + NORM_EPS)) * g_ref[...]


def _final_norm(x, g, rows, tm, block0):
    return pl.pallas_call(
        _final_norm_kernel,
        grid=(rows // tm,),
        in_specs=[pl.BlockSpec((tm, D_MODEL), lambda i: (block0 + i, 0)),
                  pl.BlockSpec((1, D_MODEL), lambda i: (0, 0))],
        out_specs=pl.BlockSpec((tm, D_MODEL), lambda i: (i, 0)),
        out_shape=jax.ShapeDtypeStruct((rows, D_MODEL), F32),
        compiler_params=_cparams("parallel"),
        name="final_norm",
    )(x, g)


def _rope_tables():
    half = RET_DK // 2
    inv = ROPE_BASE ** (-jnp.arange(half, dtype=F32) / half)
    pos = jnp.concatenate([jnp.tile(jnp.arange(SEQ, dtype=jnp.int32), BATCH),
                           jnp.full((M_S,), PAST_LEN, jnp.int32)])
    ang = pos.astype(F32)[:, None] * inv[None, :]
    cos = jnp.cos(ang)
    sin = jnp.sin(ang)
    return jnp.concatenate([cos, cos], axis=1), jnp.concatenate([-sin, sin], axis=1)


def _retention_tables():
    c = RET_CHUNK
    log_g = jnp.log1p(-jnp.exp2(-5.0 - jnp.arange(RET_HEADS, dtype=F32)))
    i = jnp.arange(c, dtype=F32)
    diff = i[:, None] - i[None, :]
    intra = jnp.where(diff[None] >= 0,
                      jnp.exp(jnp.maximum(diff, 0.0)[None] * log_g[:, None, None]), 0.0)
    q_dec = jnp.exp((i + 1.0)[None, :] * log_g[:, None])
    k_dec = jnp.exp((c - 1.0 - i)[None, :] * log_g[:, None])
    c_dec = jnp.exp(c * log_g)
    full = lambda a: jnp.broadcast_to(a, (RET_HEADS, c, RET_DV))
    g1 = jnp.broadcast_to(jnp.exp(1.0 * log_g)[:, None, None], (RET_HEADS, 1, RET_DV))
    return (intra, full(q_dec[:, :, None]), full(k_dec[:, :, None]), full(c_dec[:, None, None])), g1


def _s5_block_weights(bbr_t, bbi_t, c_re, c_im):
    eye = jnp.eye(S5_GB, dtype=F32)
    n = DEPTH * S5_NBLK

    def drive(bt):
        a = bt.reshape(n, S5_GB, S5_GROUP_CH, S5_STATE)
        return jnp.einsum('ngcp,gh->ngchp', a, eye).reshape(DEPTH, S5_NBLK, S5_BLK_CH, S5_BLK_ST)

    def readout(cm):
        a = cm.reshape(n, S5_GB, S5_GROUP_CH, S5_STATE)
        return jnp.einsum('ngcp,gh->ngphc', a, eye).reshape(DEPTH, S5_NBLK, S5_BLK_ST, S5_BLK_CH)

    w_blk = jnp.concatenate([drive(bbr_t), drive(bbi_t)], axis=3).astype(BF16)
    cw_blk = jnp.concatenate([readout(c_re), readout(-c_im)], axis=2).astype(BF16)
    return w_blk, cw_blk


def _slab_tiles(lb):
    a = lb.reshape(DEPTH, 2, S5_NSLAB, LANES).transpose(0, 2, 1, 3)
    return jnp.tile(a, (1, 1, BATCH, 1))


def kernel(x_prompt, x_sample, state_ret, state_s5_re, state_s5_im, ffn1_norm, ffn1_w1, ffn1_w3, ffn1_w2, mix_norm, w_in, ret_gn, ret_proj, s5_lam_re, s5_lam_im, s5_b_re, s5_b_im, s5_c_re, s5_c_im, s5_d, s5_log_step, glu_w, glu_b, s5_proj, w_out, ffn2_norm, ffn2_w1, ffn2_w3, ffn2_w2, final_norm):
    cos_t, sin_t = _rope_tables()
    ret_tabs, g1 = _retention_tables()
    x = jnp.concatenate([x_prompt.reshape(M_P, D_MODEL), x_sample.reshape(M_S, D_MODEL)], axis=0)

    ret_p, s5r_p, s5i_p, s5r_s, s5i_s = [], [], [], [], []
    ret_s = None
    nst = S5_GROUPS * S5_STATE
    rows3 = lambda a: a.reshape(DEPTH, 1, -1)
    ffn1_g, mix_g, ffn2_g, glu_b3 = rows3(ffn1_norm), rows3(mix_norm), rows3(ffn2_norm), rows3(glu_b)
    gn4 = ret_gn.reshape(DEPTH, RET_HEADS, 1, RET_DV)

    lbr_x, lbi_x, bbr_t, bbi_t = _s5_discretize(s5_lam_re, s5_lam_im, s5_log_step, s5_b_re, s5_b_im)
    lbr = lbr_x[::S5_GROUP_CH].reshape(DEPTH, nst)
    lbi = lbi_x[::S5_GROUP_CH].reshape(DEPTH, nst)
    w_blk, cw_blk = _s5_block_weights(bbr_t, bbi_t, s5_c_re, s5_c_im)
    lamr_t, lami_t = _slab_tiles(lbr), _slab_tiles(lbi)
    lbr3, lbi3 = lbr.reshape(DEPTH, 1, nst), lbi.reshape(DEPTH, 1, nst)
    dvec3 = s5_d.reshape(DEPTH, 1, S5_W)
    h0r_all = state_s5_re.reshape(DEPTH, M_S, nst)
    h0i_all = state_s5_im.reshape(DEPTH, M_S, nst)

    for l in range(DEPTH):
        hid = _ffn_up(x, ffn1_g, ffn1_w1, ffn1_w3, l)
        x = _mm_res(hid, ffn1_w2, x, l, scale=0.5, tm=TM_DOWN, tn=TN_DOWN, name="ffn_down")

        proj = _inproj(x, mix_g, w_in, cos_t, sin_t, l)

        o_all, s_p = _ret_prompt(proj, ret_tabs, gn4, l)
        o_s, ret_s = _ret_sample(proj, state_ret, g1, gn4, l, ret_s)
        o_all = lax.dynamic_update_slice(o_all, o_s.astype(BF16), (M_P, 0))

        z5, hre_p, him_p = _s5_prompt(proj, w_blk, cw_blk, lamr_t, lami_t, dvec3, l)
        z_s, hre_s, him_s = _s5_sample(proj, h0r_all, h0i_all, w_blk, cw_blk, lbr3, lbi3, dvec3, l)
        z = lax.dynamic_update_slice(z5.reshape((BATCH + 1) * SEQ, S5_W), z_s, (M_P, 0))

        m = _merge(z, o_all, proj, glu_w, glu_b3, s5_proj, ret_proj, l)
        x = _mm_res(m, w_out, x, l, scale=1.0, tm=TM, tn=TN_IN, name="out_proj")

        hid = _ffn_up(x, ffn2_g, ffn2_w1, ffn2_w3, l)
        x = _mm_res(hid, ffn2_w2, x, l, scale=0.5, tm=TM_DOWN, tn=TN_DOWN, name="ffn_down")

        unslab = lambda a: (a.transpose(1, 0, 2).reshape(BATCH, 2, S5_HALF_ST)
                            .reshape(BATCH, S5_GROUPS, S5_STATE))
        ret_p.append(s_p)
        s5r_p.append(unslab(hre_p))
        s5i_p.append(unslab(him_p))
        s5r_s.append(hre_s.reshape(M_S, S5_GROUPS, S5_STATE))
        s5i_s.append(him_s.reshape(M_S, S5_GROUPS, S5_STATE))

    g_fin = final_norm.reshape(1, D_MODEL)
    y_p = _final_norm(x, g_fin, M_P, 1024, 0)
    y_s = _final_norm(x, g_fin, M_S, M_S, M_P // M_S)
    return (y_p.reshape(BATCH, SEQ, D_MODEL), y_s.reshape(M_S, 1, D_MODEL),
            jnp.stack(ret_p), jnp.stack(s5r_p), jnp.stack(s5i_p),
            ret_s, jnp.stack(s5r_s), jnp.stack(s5i_s))
```

```python
import functools
import math

import jax
import jax.numpy as jnp
from jax import lax
from jax.experimental import pallas as pl
from jax.experimental.pallas import tpu as pltpu

F32 = jnp.float32
BF16 = jnp.bfloat16

D_MODEL = 2048
BATCH = 4
SEQ = 2048
DEPTH = 4
DEC_BATCH = 128
PAST_LEN = 16384
RET_HEADS = 8
RET_DK = 128
RET_DV = 128
RET_QK = RET_HEADS * RET_DK
RET_W = RET_HEADS * RET_DV
RET_CHUNK = 128
ROPE_BASE = 10000.0
S5_GROUPS = 64
S5_GROUP_CH = 16
S5_W = S5_GROUPS * S5_GROUP_CH
S5_STATE = 64
D_FF = 5632
IN_W = 2 * RET_QK + 2 * RET_W + S5_W + 2 * D_MODEL
NORM_EPS = 1e-6
HEAD_NORM_EPS = 1e-5

M_P = BATCH * SEQ
M_S = DEC_BATCH
M_ALL = M_P + M_S

LANES = 128
SUBLANES = 8
VMEM_LIMIT = 58 * 1024 * 1024

TM = 1040
TN_FF = 512
TM_DOWN = 832
TN_DOWN = 512
COL = 1024
TN_IN = 1024
IN_SUB = 256

S5_TC = 64
S5_PAD = SUBLANES // 2
S5_PITCH = S5_TC + S5_PAD
S5_GB = 16
S5_NBLK = S5_GROUPS // S5_GB
S5_BLK_CH = S5_GB * S5_GROUP_CH
S5_BLK_ST = S5_GB * S5_STATE
S5_HALF_ST = S5_GROUPS * S5_STATE // 2
S5_NSLAB = S5_HALF_ST // LANES
S5_NSEQ = BATCH * 2
S5_SLAB_GROUP = 4


def _cparams(*sem):
    return pltpu.CompilerParams(dimension_semantics=sem, vmem_limit_bytes=VMEM_LIMIT)


def _rms_to_bf16(x, g):
    ms = jnp.mean(x * x, axis=-1, keepdims=True)
    return ((x * lax.rsqrt(ms + NORM_EPS)) * g).astype(BF16)


def _ffn_up_kernel(x_ref, g_ref, w1_ref, w3_ref, o_ref, xn_ref):
    @pl.when(pl.program_id(1) == 0)
    def _():
        xn_ref[...] = _rms_to_bf16(x_ref[...], g_ref[...])

    xn = xn_ref[...]
    h1 = jnp.dot(xn, w1_ref[...].astype(BF16), preferred_element_type=F32)
    h3 = jnp.dot(xn, w3_ref[...].astype(BF16), preferred_element_type=F32)
    o_ref[...] = ((h1 * jax.nn.sigmoid(h1)) * h3).astype(BF16)


def _layer_spec(l, k, tn):
    return pl.BlockSpec((None, k, tn), lambda i, j: (l, 0, j))


def _layer_row_spec(l, n):
    return pl.BlockSpec((None, 1, n), lambda *_: (l, 0, 0))


def _ffn_up(x, g, w1, w3, l):
    return pl.pallas_call(
        _ffn_up_kernel,
        grid=(M_ALL // TM, D_FF // TN_FF),
        in_specs=[
            pl.BlockSpec((TM, D_MODEL), lambda i, j: (i, 0)),
            _layer_row_spec(l, D_MODEL),
            _layer_spec(l, D_MODEL, TN_FF),
            _layer_spec(l, D_MODEL, TN_FF),
        ],
        out_specs=pl.BlockSpec((TM, TN_FF), lambda i, j: (i, j)),
        out_shape=jax.ShapeDtypeStruct((M_ALL, D_FF), BF16),
        scratch_shapes=[pltpu.VMEM((TM, D_MODEL), BF16)],
        compiler_params=_cparams("parallel", "arbitrary"),
        name="ffn_up",
    )(x, g, w1, w3)


def _mm_res_kernel(a_ref, w_ref, x_ref, o_ref, *, scale):
    acc = jnp.dot(a_ref[...], w_ref[...].astype(BF16), preferred_element_type=F32)
    o_ref[...] = x_ref[...] + scale * acc


def _mm_res(a, w, x, l, *, scale, tm, tn, name):
    k = a.shape[1]
    n = w.shape[2]
    return pl.pallas_call(
        functools.partial(_mm_res_kernel, scale=scale),
        grid=(M_ALL // tm, n // tn),
        in_specs=[
            pl.BlockSpec((tm, k), lambda i, j: (i, 0)),
            _layer_spec(l, k, tn),
            pl.BlockSpec((tm, tn), lambda i, j: (i, j)),
        ],
        out_specs=pl.BlockSpec((tm, tn), lambda i, j: (i, j)),
        out_shape=jax.ShapeDtypeStruct((M_ALL, n), F32),
        compiler_params=_cparams("parallel", "arbitrary"),
        name=name,
    )(a, w, x)


def _inproj_kernel(x_ref, g_ref, w_ref, cos_ref, sin_ref, o_ref, xn_ref):
    j = pl.program_id(1)

    @pl.when(j == 0)
    def _():
        xn_ref[...] = _rms_to_bf16(x_ref[...], g_ref[...])

    seg = j // (COL // TN_IN)

    def project(epilogue):
        for c in range(TN_IN // IN_SUB):
            cols = slice(c * IN_SUB, (c + 1) * IN_SUB)
            acc = jnp.dot(xn_ref[...], w_ref[:, cols].astype(BF16), preferred_element_type=F32)
            o_ref[:, cols] = epilogue(acc)

    def rotary(acc):
        scale = jnp.where(seg == 0, RET_DK ** -0.5, 1.0).astype(F32)
        cos = cos_ref[...]
        sin = sin_ref[...]
        heads = []
        for h in range(IN_SUB // RET_DK):
            a = acc[:, h * RET_DK:(h + 1) * RET_DK]
            heads.append((a * cos + pltpu.roll(a, RET_DK // 2, axis=1) * sin) * scale)
        return jnp.concatenate(heads, axis=1)

    @pl.when(seg <= 1)
    def _():
        project(rotary)

    @pl.when((seg == 2) | (seg == 4))
    def _():
        project(lambda acc: acc)

    @pl.when(seg == 3)
    def _():
        project(lambda acc: acc * jax.nn.sigmoid(acc))

    @pl.when(seg >= 5)
    def _():
        project(jax.nn.sigmoid)


def _inproj(x, g, w, cos_t, sin_t, l):
    return pl.pallas_call(
        _inproj_kernel,
        grid=(M_ALL // TM, IN_W // TN_IN),
        in_specs=[
            pl.BlockSpec((TM, D_MODEL), lambda i, j: (i, 0), pipeline_mode=pl.Buffered(1)),
            _layer_row_spec(l, D_MODEL),
            _layer_spec(l, D_MODEL, TN_IN),
            pl.BlockSpec((TM, RET_DK), lambda i, j: (i, 0)),
            pl.BlockSpec((TM, RET_DK), lambda i, j: (i, 0)),
        ],
        out_specs=pl.BlockSpec((TM, TN_IN), lambda i, j: (i, j)),
        out_shape=jax.ShapeDtypeStruct((M_ALL, IN_W), F32),
        scratch_shapes=[pltpu.VMEM((TM, D_MODEL), BF16)],
        compiler_params=_cparams("parallel", "arbitrary"),
        name="inproj",
    )(x, g, w, cos_t, sin_t)


def _head_norm(o):
    mu = jnp.mean(o, axis=-1, keepdims=True)
    d = o - mu
    var = jnp.mean(d * d, axis=-1, keepdims=True)
    return d * lax.rsqrt(var + HEAD_NORM_EPS)


RET_RC = 4 * RET_CHUNK


def _ret_prompt_kernel(q_ref, k_ref, v_ref, rg_ref, intra_ref, qd_ref, kd_ref, cd_ref, gn_ref,
                       o_ref, s_ref, st_ref):
    step = pl.program_id(1)

    @pl.when(step == 0)
    def _():
        st_ref[...] = jnp.zeros_like(st_ref)

    for c in range(RET_RC // RET_CHUNK):
        rows = slice(c * RET_CHUNK, (c + 1) * RET_CHUNK)
        for h in range(RET_HEADS):
            cols = slice(h * RET_DK, (h + 1) * RET_DK)
            qc = q_ref[rows, cols]
            kc = k_ref[rows, cols]
            vb = v_ref[rows, cols].astype(BF16)
            s = st_ref[h]
            sc = lax.dot_general(qc.astype(BF16), kc.astype(BF16), (((1,), (1,)), ((), ())),
                                 preferred_element_type=F32) * intra_ref[h]
            o = (jnp.dot(sc.astype(BF16), vb, preferred_element_type=F32)
                 + jnp.dot((qc * qd_ref[h]).astype(BF16), s.astype(BF16),
                           preferred_element_type=F32))
            kt = jnp.transpose(kc * kd_ref[h]).astype(BF16)
            st_ref[h] = s * cd_ref[h] + jnp.dot(kt, vb, preferred_element_type=F32)
            on = (_head_norm(o) * gn_ref[h]) * rg_ref[rows, cols]
            o_ref[rows, cols] = on.astype(BF16)

    @pl.when(step == pl.num_programs(1) - 1)
    def _():
        s_ref[...] = st_ref[...]


def _ret_prompt(proj, tabs, gn, l):
    intra, qd, kd, cd = tabs
    nsteps = SEQ // RET_RC
    tab_spec = pl.BlockSpec((RET_HEADS, RET_CHUNK, RET_CHUNK), lambda b, s: (0, 0, 0))

    def seg_spec(seg):
        return pl.BlockSpec((RET_RC, COL), lambda b, s, seg=seg: (b * nsteps + s, seg))

    return pl.pallas_call(
        _ret_prompt_kernel,
        grid=(BATCH, nsteps),
        in_specs=[
            seg_spec(0), seg_spec(1), seg_spec(2), seg_spec(3),
            tab_spec, tab_spec, tab_spec, tab_spec,
            pl.BlockSpec((None, RET_HEADS, 1, RET_DV), lambda b, s: (l, 0, 0, 0)),
        ],
        out_specs=[
            pl.BlockSpec((RET_RC, RET_W), lambda b, s: (b * nsteps + s, 0)),
            pl.BlockSpec((None, RET_HEADS, RET_DK, RET_DV), lambda b, s: (b, 0, 0, 0)),
        ],
        out_shape=[
            jax.ShapeDtypeStruct((M_ALL, RET_W), BF16),
            jax.ShapeDtypeStruct((BATCH, RET_HEADS, RET_DK, RET_DV), F32),
        ],
        scratch_shapes=[pltpu.VMEM((RET_HEADS, RET_DK, RET_DV), F32)],
        compiler_params=_cparams("parallel", "arbitrary"),
        name="ret_prompt",
    )(proj, proj, proj, proj, intra, qd, kd, cd, gn)


RET_SB = 8


def _col_bcast(row, eye, ones3):
    hi = row.astype(BF16)
    r1 = row - hi.astype(F32)
    mid = r1.astype(BF16)
    lo = (r1 - mid.astype(F32)).astype(BF16)
    parts = [jnp.where(eye, jnp.broadcast_to(t.astype(F32), (RET_DK, RET_DK)), 0.0).astype(BF16)
             for t in (hi, mid, lo)]
    return jnp.dot(jnp.concatenate(parts, axis=1), ones3, preferred_element_type=F32)


def _ret_sample_kernel(q_ref, k_ref, v_ref, rg_ref, s_ref, g1_ref, gn_ref, *rest):
    o_ref, so_ref = rest[-2:]
    eye = (lax.broadcasted_iota(jnp.int32, (RET_DK, RET_DK), 0)
           == lax.broadcasted_iota(jnp.int32, (RET_DK, RET_DK), 1))
    ones3 = jnp.ones((3 * RET_DK, RET_DV), BF16)

    for b in range(RET_SB):
        row = slice(b, b + 1)
        for h in range(RET_HEADS):
            cols = slice(h * RET_DK, (h + 1) * RET_DK)
            kcol = _col_bcast(k_ref[row, cols], eye, ones3)
            qcol = _col_bcast(q_ref[row, cols], eye, ones3)
            s_new = s_ref[b, h] * g1_ref[h] + kcol * v_ref[row, cols]
            so_ref[b, h] = s_new
            o = jnp.sum(qcol * s_new, axis=0, keepdims=True)
            o_ref[row, cols] = (_head_norm(o) * gn_ref[h]) * rg_ref[row, cols]


def _ret_sample(proj, state, g1, gn, l, prev_states):
    row0 = M_P // RET_SB

    def seg_spec(seg):
        return pl.BlockSpec((RET_SB, COL), lambda i, seg=seg: (row0 + i, seg))

    st_spec = pl.BlockSpec((None, RET_SB, RET_HEADS, RET_DK, RET_DV), lambda i: (l, i, 0, 0, 0))
    in_specs = [
        seg_spec(0), seg_spec(1), seg_spec(2), seg_spec(3), st_spec,
        pl.BlockSpec((RET_HEADS, 1, RET_DV), lambda i: (0, 0, 0)),
        pl.BlockSpec((None, RET_HEADS, 1, RET_DV), lambda i: (l, 0, 0, 0)),
    ]
    args = [proj, proj, proj, proj, state, g1, gn]
    aliases = {}
    if prev_states is not None:
        in_specs.append(pl.BlockSpec(memory_space=pl.ANY))
        args.append(prev_states)
        aliases = {len(args) - 1: 1}
    return pl.pallas_call(
        _ret_sample_kernel,
        grid=(M_S // RET_SB,),
        in_specs=in_specs,
        out_specs=[pl.BlockSpec((RET_SB, RET_W), lambda i: (i, 0)), st_spec],
        out_shape=[
            jax.ShapeDtypeStruct((M_S, RET_W), F32),
            jax.ShapeDtypeStruct((DEPTH, M_S, RET_HEADS, RET_DK, RET_DV), F32),
        ],
        input_output_aliases=aliases,
        compiler_params=_cparams("parallel"),
        name="ret_sample",
    )(*args)


def _s5_disc_kernel(lr_ref, li_ref, ls_ref, br_ref, bi_ref, lbr_ref, lbi_ref, bbr_ref, bbi_ref):
    lam_r = lr_ref[...]
    lam_i = li_ref[...]
    dt = jnp.exp(ls_ref[...])
    ea = jnp.exp(lam_r * dt)
    lb_r = ea * jnp.cos(lam_i * dt)
    lb_i = ea * jnp.sin(lam_i * dt)
    lbr_ref[...] = lb_r
    lbi_ref[...] = lb_i
    nr = lb_r - 1.0
    den = lam_r * lam_r + lam_i * lam_i
    cr = (nr * lam_r + lb_i * lam_i) / den
    ci = (lb_i * lam_r - nr * lam_i) / den
    b_r = br_ref[...]
    b_i = bi_ref[...]
    bbr_ref[...] = cr * b_r - ci * b_i
    bbi_ref[...] = cr * b_i + ci * b_r


def _s5_discretize(lam_re, lam_im, log_step, b_re, b_im):
    rows = DEPTH * S5_GROUPS * S5_GROUP_CH
    rep = lambda a: jnp.repeat(a.reshape(DEPTH * S5_GROUPS, S5_STATE), S5_GROUP_CH, axis=0)
    tr = lambda a: jnp.transpose(a, (0, 1, 3, 2)).reshape(rows, S5_STATE)
    shp = jax.ShapeDtypeStruct((rows, S5_STATE), F32)
    ls = jnp.broadcast_to(log_step[:, :, None], (DEPTH, S5_GROUPS, S5_STATE))
    spec = pl.BlockSpec((rows // DEPTH, S5_STATE), lambda l: (l, 0))
    return pl.pallas_call(
        _s5_disc_kernel,
        grid=(DEPTH,),
        in_specs=[spec] * 5,
        out_specs=[spec] * 4,
        out_shape=[shp, shp, shp, shp],
        compiler_params=_cparams("parallel"),
        name="s5_disc",
    )(rep(lam_re), rep(lam_im), rep(ls), tr(b_re), tr(b_im))


def _gelu_tanh(y):
    c = math.sqrt(2.0 / math.pi)
    return 0.5 * y * (1.0 + jnp.tanh(c * (y + 0.044715 * (y * y * y))))


def _s5_prompt_kernel(u0_ref, u1_ref, u2_ref, u3_ref, w_ref, cw_ref, lamr_ref, lami_ref, d_ref,
                      z_ref, hre_ref, him_ref, bre_ref, bim_ref):
    j = pl.program_id(0)
    u_refs = (u0_ref, u1_ref, u2_ref, u3_ref)

    @pl.when(j == 0)
    def _():
        hre_ref[...] = jnp.zeros_like(hre_ref)
        him_ref[...] = jnp.zeros_like(him_ref)

    nsl = S5_BLK_ST // LANES
    zpad = jnp.zeros((S5_PAD, S5_BLK_CH), F32)

    for gh in range(2):
        seg = S5_TC + 2 * S5_PAD * gh
        for kb in range(2):
            blk = gh * 2 + kb
            c0 = blk * S5_BLK_CH
            parts = []
            for u in u_refs:
                ub = u[:, c0:c0 + S5_BLK_CH]
                parts.extend([zpad, ub, zpad] if gh else [ub])
            x = jnp.concatenate(parts, axis=0).astype(BF16)
            res = jnp.dot(x, w_ref[blk], preferred_element_type=F32)
            for b in range(BATCH):
                r0 = (b * 2 + gh) * S5_PITCH - S5_PAD * gh
                part = res[b * seg:(b + 1) * seg]
                for s in range(nsl):
                    slab = kb * nsl + s
                    bre_ref[slab, r0:r0 + seg, :] = part[:, s * LANES:(s + 1) * LANES]
                    bim_ref[slab, r0:r0 + seg, :] = part[:, S5_BLK_ST + s * LANES:S5_BLK_ST + (s + 1) * LANES]

    for sg in range(S5_NSLAB // S5_SLAB_GROUP):
        slabs = [sg * S5_SLAB_GROUP + i for i in range(S5_SLAB_GROUP)]
        lam_r = [lamr_ref[s] for s in slabs]
        lam_i = [lami_ref[s] for s in slabs]

        def step(t, carry, slabs=slabs, lam_r=lam_r, lam_i=lam_i):
            hr, hi = carry
            nr_all = []
            ni_all = []
            for i, s in enumerate(slabs):
                idx = pl.ds(t, S5_NSEQ, stride=S5_PITCH)
                br = bre_ref[s, idx, :]
                bi = bim_ref[s, idx, :]
                nr = lam_r[i] * hr[i] - lam_i[i] * hi[i] + br
                ni = lam_r[i] * hi[i] + lam_i[i] * hr[i] + bi
                bre_ref[s, idx, :] = nr
                bim_ref[s, idx, :] = ni
                nr_all.append(nr)
                ni_all.append(ni)
            return tuple(nr_all), tuple(ni_all)

        init = (tuple(hre_ref[s] for s in slabs), tuple(him_ref[s] for s in slabs))
        hr, hi = lax.fori_loop(0, S5_TC, step, init, unroll=2)
        for i, s in enumerate(slabs):
            hre_ref[s] = hr[i]
            him_ref[s] = hi[i]

    for gh in range(2):
        seg = S5_TC + 2 * S5_PAD * gh
        for kb in range(2):
            blk = gh * 2 + kb
            c0 = blk * S5_BLK_CH
            pieces = []
            for src in (bre_ref, bim_ref):
                for s in range(nsl):
                    slab = kb * nsl + s
                    starts = [(b * 2 + gh) * S5_PITCH - S5_PAD * gh for b in range(BATCH)]
                    pieces.append(jnp.concatenate(
                        [src[slab, r0:r0 + seg, :] for r0 in starts], axis=0))
            hcat = jnp.concatenate(pieces, axis=1).astype(BF16)
            y = jnp.dot(hcat, cw_ref[blk], preferred_element_type=F32)
            dvec = d_ref[:, c0:c0 + S5_BLK_CH]
            for b in range(BATCH):
                lo = b * seg + S5_PAD * gh
                yb = y[lo:lo + S5_TC] + dvec * u_refs[b][:, c0:c0 + S5_BLK_CH]
                z_ref[b, :, c0:c0 + S5_BLK_CH] = _gelu_tanh(yb)


def _s5_prompt(proj, w_blk, cw_blk, lamr_t, lami_t, dvec, l):
    nsteps = SEQ // S5_TC
    su_col = (2 * RET_QK + 2 * RET_W) // S5_W

    def u_spec(b):
        return pl.BlockSpec((S5_TC, S5_W), lambda j, b=b: (b * nsteps + j, su_col))

    layer4 = lambda shape: pl.BlockSpec((None,) + shape, lambda j: (l, 0, 0, 0))
    full3 = lambda shape: pl.BlockSpec(shape, lambda j: (0, 0, 0))
    st_shape = (S5_NSLAB, S5_NSEQ, LANES)
    return pl.pallas_call(
        _s5_prompt_kernel,
        grid=(nsteps,),
        in_specs=[
            u_spec(0), u_spec(1), u_spec(2), u_spec(3),
            layer4((S5_NBLK, S5_BLK_CH, 2 * S5_BLK_ST)),
            layer4((S5_NBLK, 2 * S5_BLK_ST, S5_BLK_CH)),
            layer4(st_shape), layer4(st_shape),
            _layer_row_spec(l, S5_W),
        ],
        out_specs=[
            pl.BlockSpec((BATCH, S5_TC, S5_W), lambda j: (0, j, 0)),
            full3(st_shape), full3(st_shape),
        ],
        out_shape=[
            jax.ShapeDtypeStruct((BATCH + 1, SEQ, S5_W), F32),
            jax.ShapeDtypeStruct(st_shape, F32),
            jax.ShapeDtypeStruct(st_shape, F32),
        ],
        scratch_shapes=[
            pltpu.VMEM((S5_NSLAB, S5_NSEQ * S5_PITCH, LANES), F32),
            pltpu.VMEM((S5_NSLAB, S5_NSEQ * S5_PITCH, LANES), F32),
        ],
        compiler_params=_cparams("arbitrary"),
        name="s5_prompt",
    )(proj, proj, proj, proj, w_blk, cw_blk, lamr_t, lami_t, dvec)


def _s5_sample_kernel(u_ref, h0r_ref, h0i_ref, w_ref, cw_ref, lbr_ref, lbi_ref, d_ref,
                      z_ref, hr_ref, hi_ref):
    for blk in range(S5_NBLK):
        c0 = blk * S5_BLK_CH
        s0 = blk * S5_BLK_ST
        u = u_ref[:, c0:c0 + S5_BLK_CH]
        res = jnp.dot(u.astype(BF16), w_ref[blk], preferred_element_type=F32)
        lr = lbr_ref[:, s0:s0 + S5_BLK_ST]
        li = lbi_ref[:, s0:s0 + S5_BLK_ST]
        h0r = h0r_ref[:, s0:s0 + S5_BLK_ST]
        h0i = h0i_ref[:, s0:s0 + S5_BLK_ST]
        hr = lr * h0r - li * h0i + res[:, :S5_BLK_ST]
        hi = lr * h0i + li * h0r + res[:, S5_BLK_ST:]
        hr_ref[:, s0:s0 + S5_BLK_ST] = hr
        hi_ref[:, s0:s0 + S5_BLK_ST] = hi
        hcat = jnp.concatenate([hr, hi], axis=1).astype(BF16)
        y = jnp.dot(hcat, cw_ref[blk], preferred_element_type=F32) + d_ref[:, c0:c0 + S5_BLK_CH] * u
        z_ref[:, c0:c0 + S5_BLK_CH] = _gelu_tanh(y)


def _s5_sample(proj, h0r, h0i, w_blk, cw_blk, lbr, lbi, dvec, l):
    nst = S5_GROUPS * S5_STATE
    su_col = (2 * RET_QK + 2 * RET_W) // S5_W
    layer3 = lambda shape: pl.BlockSpec((None,) + shape, lambda i: (l, 0, 0))
    layer4 = lambda shape: pl.BlockSpec((None,) + shape, lambda i: (l, 0, 0, 0))
    whole = lambda shape: pl.BlockSpec(shape, lambda i: (0, 0))
    return pl.pallas_call(
        _s5_sample_kernel,
        grid=(1,),
        in_specs=[
            pl.BlockSpec((M_S, S5_W), lambda i: (M_P // M_S, su_col)),
            layer3((M_S, nst)), layer3((M_S, nst)),
            layer4((S5_NBLK, S5_BLK_CH, 2 * S5_BLK_ST)),
            layer4((S5_NBLK, 2 * S5_BLK_ST, S5_BLK_CH)),
            layer3((1, nst)), layer3((1, nst)),
            _layer_row_spec(l, S5_W),
        ],
        out_specs=[whole((M_S, S5_W)), whole((M_S, nst)), whole((M_S, nst))],
        out_shape=[
            jax.ShapeDtypeStruct((M_S, S5_W), F32),
            jax.ShapeDtypeStruct((M_S, nst), F32),
            jax.ShapeDtypeStruct((M_S, nst), F32),
        ],
        compiler_params=_cparams("arbitrary"),
        name="s5_sample",
    )(proj, h0r, h0i, w_blk, cw_blk, lbr, lbi, dvec)


TM_MERGE = 520


def _merge_kernel(z_ref, o_ref, gr_ref, gs_ref, gluw_ref, glub_ref, s5p_ref, retp_ref, m_ref, z2_ref):
    @pl.when(pl.program_id(1) == 0)
    def _():
        z = z_ref[...]
        t = (jnp.dot(z.astype(BF16), gluw_ref[...].astype(BF16), preferred_element_type=F32)
             + glub_ref[...])
        z2_ref[...] = (z * jax.nn.sigmoid(t)).astype(BF16)

    b_s5 = jnp.dot(z2_ref[...], s5p_ref[...].astype(BF16), preferred_element_type=F32)
    b_ret = jnp.dot(o_ref[...], retp_ref[...].astype(BF16), preferred_element_type=F32)
    m_ref[...] = (gr_ref[...] * b_ret + gs_ref[...] * b_s5).astype(BF16)


def _merge(z, o, proj, glu_w, glu_b, s5_proj, ret_proj, l):
    gr_col = (2 * RET_QK + 2 * RET_W + S5_W) // COL
    gs_col = gr_col + D_MODEL // COL
    return pl.pallas_call(
        _merge_kernel,
        grid=(M_ALL // TM_MERGE, D_MODEL // COL),
        in_specs=[
            pl.BlockSpec((TM_MERGE, S5_W), lambda i, j: (i, 0)),
            pl.BlockSpec((TM_MERGE, RET_W), lambda i, j: (i, 0)),
            pl.BlockSpec((TM_MERGE, COL), lambda i, j: (i, gr_col + j)),
            pl.BlockSpec((TM_MERGE, COL), lambda i, j: (i, gs_col + j)),
            pl.BlockSpec((None, S5_W, S5_W), lambda i, j: (l, 0, 0)),
            _layer_row_spec(l, S5_W),
            _layer_spec(l, S5_W, COL),
            _layer_spec(l, RET_W, COL),
        ],
        out_specs=pl.BlockSpec((TM_MERGE, COL), lambda i, j: (i, j)),
        out_shape=jax.ShapeDtypeStruct((M_ALL, D_MODEL), BF16),
        scratch_shapes=[pltpu.VMEM((TM_MERGE, S5_W), BF16)],
        compiler_params=_cparams("parallel", "arbitrary"),
        name="merge",
    )(z, o, proj, proj, glu_w, glu_b, s5_proj, ret_proj)


def _final_norm_kernel(x_ref, g_ref, o_ref):
    x = x_ref[...]
    ms = jnp.mean(x * x, axis=-1, keepdims=True)
    o_ref[...] = (x * lax.rsqrt(ms + NORM_EPS)) * g_ref[...]


def _final_norm(x, g, rows, tm, block0):
    return pl.pallas_call(
        _final_norm_kernel,
        grid=(rows // tm,),
        in_specs=[pl.BlockSpec((tm, D_MODEL), lambda i: (block0 + i, 0)),
                  pl.BlockSpec((1, D_MODEL), lambda i: (0, 0))],
        out_specs=pl.BlockSpec((tm, D_MODEL), lambda i: (i, 0)),
        out_shape=jax.ShapeDtypeStruct((rows, D_MODEL), F32),
        compiler_params=_cparams("parallel"),
        name="final_norm",
    )(x, g)


def _rope_tables():
    half = RET_DK // 2
    inv = ROPE_BASE ** (-jnp.arange(half, dtype=F32) / half)
    pos = jnp.concatenate([jnp.tile(jnp.arange(SEQ, dtype=jnp.int32), BATCH),
                           jnp.full((M_S,), PAST_LEN, jnp.int32)])
    ang = pos.astype(F32)[:, None] * inv[None, :]
    cos = jnp.cos(ang)
    sin = jnp.sin(ang)
    return jnp.concatenate([cos, cos], axis=1), jnp.concatenate([-sin, sin], axis=1)


def _retention_tables():
    c = RET_CHUNK
    log_g = jnp.log1p(-jnp.exp2(-5.0 - jnp.arange(RET_HEADS, dtype=F32)))
    i = jnp.arange(c, dtype=F32)
    diff = i[:, None] - i[None, :]
    intra = jnp.where(diff[None] >= 0,
                      jnp.exp(jnp.maximum(diff, 0.0)[None] * log_g[:, None, None]), 0.0)
    q_dec = jnp.exp((i + 1.0)[None, :] * log_g[:, None])
    k_dec = jnp.exp((c - 1.0 - i)[None, :] * log_g[:, None])
    c_dec = jnp.exp(c * log_g)
    full = lambda a: jnp.broadcast_to(a, (RET_HEADS, c, RET_DV))
    g1 = jnp.broadcast_to(jnp.exp(1.0 * log_g)[:, None, None], (RET_HEADS, 1, RET_DV))
    return (intra, full(q_dec[:, :, None]), full(k_dec[:, :, None]), full(c_dec[:, None, None])), g1


def _s5_block_weights(bbr_t, bbi_t, c_re, c_im):
    eye = jnp.eye(S5_GB, dtype=F32)
    n = DEPTH * S5_NBLK

    def drive(bt):
        a = bt.reshape(n, S5_GB, S5_GROUP_CH, S5_STATE)
        return jnp.einsum('ngcp,gh->ngchp', a, eye).reshape(DEPTH, S5_NBLK, S5_BLK_CH, S5_BLK_ST)

    def readout(cm):
        a = cm.reshape(n, S5_GB, S5_GROUP_CH, S5_STATE)
        return jnp.einsum('ngcp,gh->ngphc', a, eye).reshape(DEPTH, S5_NBLK, S5_BLK_ST, S5_BLK_CH)

    w_blk = jnp.concatenate([drive(bbr_t), drive(bbi_t)], axis=3).astype(BF16)
    cw_blk = jnp.concatenate([readout(c_re), readout(-c_im)], axis=2).astype(BF16)
    return w_blk, cw_blk


def _slab_tiles(lb):
    a = lb.reshape(DEPTH, 2, S5_NSLAB, LANES).transpose(0, 2, 1, 3)
    return jnp.tile(a, (1, 1, BATCH, 1))


def kernel(x_prompt, x_sample, state_ret, state_s5_re, state_s5_im, ffn1_norm, ffn1_w1, ffn1_w3, ffn1_w2, mix_norm, w_in, ret_gn, ret_proj, s5_lam_re, s5_lam_im, s5_b_re, s5_b_im, s5_c_re, s5_c_im, s5_d, s5_log_step, glu_w, glu_b, s5_proj, w_out, ffn2_norm, ffn2_w1, ffn2_w3, ffn2_w2, final_norm):
    cos_t, sin_t = _rope_tables()
    ret_tabs, g1 = _retention_tables()
    x = jnp.concatenate([x_prompt.reshape(M_P, D_MODEL), x_sample.reshape(M_S, D_MODEL)], axis=0)

    ret_p, s5r_p, s5i_p, s5r_s, s5i_s = [], [], [], [], []
    ret_s = None
    nst = S5_GROUPS * S5_STATE
    rows3 = lambda a: a.reshape(DEPTH, 1, -1)
    ffn1_g, mix_g, ffn2_g, glu_b3 = rows3(ffn1_norm), rows3(mix_norm), rows3(ffn2_norm), rows3(glu_b)
    gn4 = ret_gn.reshape(DEPTH, RET_HEADS, 1, RET_DV)

    lbr_x, lbi_x, bbr_t, bbi_t = _s5_discretize(s5_lam_re, s5_lam_im, s5_log_step, s5_b_re, s5_b_im)
    lbr = lbr_x[::S5_GROUP_CH].reshape(DEPTH, nst)
    lbi = lbi_x[::S5_GROUP_CH].reshape(DEPTH, nst)
    w_blk, cw_blk = _s5_block_weights(bbr_t, bbi_t, s5_c_re, s5_c_im)
    lamr_t, lami_t = _slab_tiles(lbr), _slab_tiles(lbi)
    lbr3, lbi3 = lbr.reshape(DEPTH, 1, nst), lbi.reshape(DEPTH, 1, nst)
    dvec3 = s5_d.reshape(DEPTH, 1, S5_W)
    h0r_all = state_s5_re.reshape(DEPTH, M_S, nst)
    h0i_all = state_s5_im.reshape(DEPTH, M_S, nst)

    for l in range(DEPTH):
        hid = _ffn_up(x, ffn1_g, ffn1_w1, ffn1_w3, l)
        x = _mm_res(hid, ffn1_w2, x, l, scale=0.5, tm=TM_DOWN, tn=TN_DOWN, name="ffn_down")

        proj = _inproj(x, mix_g, w_in, cos_t, sin_t, l)

        o_all, s_p = _ret_prompt(proj, ret_tabs, gn4, l)
        o_s, ret_s = _ret_sample(proj, state_ret, g1, gn4, l, ret_s)
        o_all = lax.dynamic_update_slice(o_all, o_s.astype(BF16), (M_P, 0))

        z5, hre_p, him_p = _s5_prompt(proj, w_blk, cw_blk, lamr_t, lami_t, dvec3, l)
        z_s, hre_s, him_s = _s5_sample(proj, h0r_all, h0i_all, w_blk, cw_blk, lbr3, lbi3, dvec3, l)
        z = lax.dynamic_update_slice(z5.reshape((BATCH + 1) * SEQ, S5_W), z_s, (M_P, 0))

        m = _merge(z, o_all, proj, glu_w, glu_b3, s5_proj, ret_proj, l)
        x = _mm_res(m, w_out, x, l, scale=1.0, tm=TM, tn=TN_IN, name="out_proj")

        hid = _ffn_up(x, ffn2_g, ffn2_w1, ffn2_w3, l)
        x = _mm_res(hid, ffn2_w2, x, l, scale=0.5, tm=TM_DOWN, tn=TN_DOWN, name="ffn_down")

        unslab = lambda a: (a.transpose(1, 0, 2).reshape(BATCH, 2, S5_HALF_ST)
                            .reshape(BATCH, S5_GROUPS, S5_STATE))
        ret_p.append(s_p)
        s5r_p.append(unslab(hre_p))
        s5i_p.append(unslab(him_p))
        s5r_s.append(hre_s.reshape(M_S, S5_GROUPS, S5_STATE))
        s5i_s.append(him_s.reshape(M_S, S5_GROUPS, S5_STATE))

    g_fin = final_norm.reshape(1, D_MODEL)
    y_p = _final_norm(x, g_fin, M_P, 1024, 0)
    y_s = _final_norm(x, g_fin, M_S, M_S, M_P // M_S)
    return (y_p.reshape(BATCH, SEQ, D_MODEL), y_s.reshape(M_S, 1, D_MODEL),
            jnp.stack(ret_p), jnp.stack(s5r_p), jnp.stack(s5i_p),
            ret_s, jnp.stack(s5r_s), jnp.stack(s5i_s))
```

```python
import functools
import math

import jax
import jax.numpy as jnp
from jax import lax
from jax.experimental import pallas as pl
from jax.experimental.pallas import tpu as pltpu

F32 = jnp.float32
BF16 = jnp.bfloat16

D_MODEL = 2048
BATCH = 4
SEQ = 2048
DEPTH = 4
DEC_BATCH = 128
PAST_LEN = 16384
RET_HEADS = 8
RET_DK = 128
RET_DV = 128
RET_QK = RET_HEADS * RET_DK
RET_W = RET_HEADS * RET_DV
RET_CHUNK = 128
ROPE_BASE = 10000.0
S5_GROUPS = 64
S5_GROUP_CH = 16
S5_W = S5_GROUPS * S5_GROUP_CH
S5_STATE = 64
D_FF = 5632
IN_W = 2 * RET_QK + 2 * RET_W + S5_W + 2 * D_MODEL
NORM_EPS = 1e-6
HEAD_NORM_EPS = 1e-5

M_P = BATCH * SEQ
M_S = DEC_BATCH
M_ALL = M_P + M_S

LANES = 128
SUBLANES = 8
VMEM_LIMIT = 58 * 1024 * 1024

TM = 1040
TN_FF = 512
TM_DOWN = 832
TN_DOWN = 512
COL = 1024
TM_BIG = 4 * TM
TN_IN = 512
IN_SUB = 256

S5_TC = 64
S5_PAD = SUBLANES // 2
S5_PITCH = S5_TC + S5_PAD
S5_GB = 16
S5_NBLK = S5_GROUPS // S5_GB
S5_BLK_CH = S5_GB * S5_GROUP_CH
S5_BLK_ST = S5_GB * S5_STATE
S5_HALF_ST = S5_GROUPS * S5_STATE // 2
S5_NSLAB = S5_HALF_ST // LANES
S5_NSEQ = BATCH * 2
S5_SLAB_GROUP = 4


def _cparams(*sem):
    return pltpu.CompilerParams(dimension_semantics=sem, vmem_limit_bytes=VMEM_LIMIT)


def _rms_to_bf16(x, g):
    ms = jnp.mean(x * x, axis=-1, keepdims=True)
    return ((x * lax.rsqrt(ms + NORM_EPS)) * g).astype(BF16)


def _layer_spec(l, k, tn):
    return pl.BlockSpec((None, k, tn), lambda i, j: (l, 0, j))


def _layer_row_spec(l, n):
    return pl.BlockSpec((None, 1, n), lambda *_: (l, 0, 0))


def _norm_kernel(x_ref, g_ref, o_ref):
    o_ref[...] = _rms_to_bf16(x_ref[...], g_ref[...])


def _norm(x, g, l):
    return pl.pallas_call(
        _norm_kernel,
        grid=(M_ALL // TM,),
        in_specs=[pl.BlockSpec((TM, D_MODEL), lambda i: (i, 0)), _layer_row_spec(l, D_MODEL)],
        out_specs=pl.BlockSpec((TM, D_MODEL), lambda i: (i, 0)),
        out_shape=jax.ShapeDtypeStruct((M_ALL, D_MODEL), BF16),
        compiler_params=_cparams("parallel"),
        name="norm",
    )(x, g)


def _resident_rows_spec():
    return pl.BlockSpec((TM_BIG, D_MODEL), lambda r, j: (r, 0), pipeline_mode=pl.Buffered(1))


def _ffn_up_kernel(xn_ref, w1_ref, w3_ref, o_ref):
    w1 = w1_ref[...].astype(BF16)
    w3 = w3_ref[...].astype(BF16)
    for s in range(TM_BIG // TM):
        rows = slice(s * TM, (s + 1) * TM)
        xs = xn_ref[rows, :]
        h1 = jnp.dot(xs, w1, preferred_element_type=F32)
        h3 = jnp.dot(xs, w3, preferred_element_type=F32)
        o_ref[rows, :] = ((h1 * jax.nn.sigmoid(h1)) * h3).astype(BF16)


def _ffn_up(xn, w1, w3, l):
    return pl.pallas_call(
        _ffn_up_kernel,
        grid=(M_ALL // TM_BIG, D_FF // TN_FF),
        in_specs=[
            _resident_rows_spec(),
            _layer_spec(l, D_MODEL, TN_FF),
            _layer_spec(l, D_MODEL, TN_FF),
        ],
        out_specs=pl.BlockSpec((TM_BIG, TN_FF), lambda r, j: (r, j)),
        out_shape=jax.ShapeDtypeStruct((M_ALL, D_FF), BF16),
        compiler_params=_cparams("parallel", "arbitrary"),
        name="ffn_up",
    )(xn, w1, w3)


def _mm_res_kernel(a_ref, w_ref, x_ref, o_ref, *, scale):
    acc = jnp.dot(a_ref[...], w_ref[...].astype(BF16), preferred_element_type=F32)
    o_ref[...] = x_ref[...] + scale * acc


def _mm_res(a, w, x, l, *, scale, tm, tn, name):
    k = a.shape[1]
    n = w.shape[2]
    return pl.pallas_call(
        functools.partial(_mm_res_kernel, scale=scale),
        grid=(M_ALL // tm, n // tn),
        in_specs=[
            pl.BlockSpec((tm, k), lambda i, j: (i, 0)),
            _layer_spec(l, k, tn),
            pl.BlockSpec((tm, tn), lambda i, j: (i, j)),
        ],
        out_specs=pl.BlockSpec((tm, tn), lambda i, j: (i, j)),
        out_shape=jax.ShapeDtypeStruct((M_ALL, n), F32),
        compiler_params=_cparams("parallel", "arbitrary"),
        name=name,
    )(a, w, x)


def _inproj_kernel(xn_ref, w_ref, cos_ref, sin_ref, o_ref):
    seg = pl.program_id(1) // (COL // TN_IN)

    def project(epilogue):
        wb = [w_ref[:, c * IN_SUB:(c + 1) * IN_SUB].astype(BF16) for c in range(TN_IN // IN_SUB)]
        for s in range(TM_BIG // TM):
            rows = slice(s * TM, (s + 1) * TM)
            for c in range(TN_IN // IN_SUB):
                acc = jnp.dot(xn_ref[rows, :], wb[c], preferred_element_type=F32)
                o_ref[rows, c * IN_SUB:(c + 1) * IN_SUB] = epilogue(acc, rows)

    def rotary(acc, rows):
        scale = jnp.where(seg == 0, RET_DK ** -0.5, 1.0).astype(F32)
        cos = cos_ref[rows, :]
        sin = sin_ref[rows, :]
        heads = []
        for h in range(IN_SUB // RET_DK):
            a = acc[:, h * RET_DK:(h + 1) * RET_DK]
            heads.append((a * cos + pltpu.roll(a, RET_DK // 2, axis=1) * sin) * scale)
        return jnp.concatenate(heads, axis=1)

    @pl.when(seg <= 1)
    def _():
        project(rotary)

    @pl.when((seg == 2) | (seg == 4))
    def _():
        project(lambda acc, rows: acc)

    @pl.when(seg == 3)
    def _():
        project(lambda acc, rows: acc * jax.nn.sigmoid(acc))

    @pl.when(seg >= 5)
    def _():
        project(lambda acc, rows: jax.nn.sigmoid(acc))


def _inproj(xn, w, cos_t, sin_t, l):
    tab_spec = pl.BlockSpec((TM_BIG, RET_DK), lambda r, j: (r, 0), pipeline_mode=pl.Buffered(1))
    return pl.pallas_call(
        _inproj_kernel,
        grid=(M_ALL // TM_BIG, IN_W // TN_IN),
        in_specs=[
            _resident_rows_spec(),
            _layer_spec(l, D_MODEL, TN_IN),
            tab_spec, tab_spec,
        ],
        out_specs=pl.BlockSpec((TM_BIG, TN_IN), lambda r, j: (r, j)),
        out_shape=jax.ShapeDtypeStruct((M_ALL, IN_W), F32),
        compiler_params=_cparams("parallel", "arbitrary"),
        name="inproj",
    )(xn, w, cos_t, sin_t)


def _head_norm(o):
    mu = jnp.mean(o, axis=-1, keepdims=True)
    d = o - mu
    var = jnp.mean(d * d, axis=-1, keepdims=True)
    return d * lax.rsqrt(var + HEAD_NORM_EPS)


RET_RC = 4 * RET_CHUNK


def _ret_prompt_kernel(q_ref, k_ref, v_ref, rg_ref, intra_ref, qd_ref, kd_ref, cd_ref, gn_ref,
                       o_ref, s_ref, st_ref):
    step = pl.program_id(1)

    @pl.when(step == 0)
    def _():
        st_ref[...] = jnp.zeros_like(st_ref)

    for c in range(RET_RC // RET_CHUNK):
        rows = slice(c * RET_CHUNK, (c + 1) * RET_CHUNK)
        for h in range(RET_HEADS):
            cols = slice(h * RET_DK, (h + 1) * RET_DK)
            qc = q_ref[rows, cols]
            kc = k_ref[rows, cols]
            vb = v_ref[rows, cols].astype(BF16)
            s = st_ref[h]
            sc = lax.dot_general(qc.astype(BF16), kc.astype(BF16), (((1,), (1,)), ((), ())),
                                 preferred_element_type=F32) * intra_ref[h]
            o = (jnp.dot(sc.astype(BF16), vb, preferred_element_type=F32)
                 + jnp.dot((qc * qd_ref[h]).astype(BF16), s.astype(BF16),
                           preferred_element_type=F32))
            kt = jnp.transpose(kc * kd_ref[h]).astype(BF16)
            st_ref[h] = s * cd_ref[h] + jnp.dot(kt, vb, preferred_element_type=F32)
            on = (_head_norm(o) * gn_ref[h]) * rg_ref[rows, cols]
            o_ref[rows, cols] = on.astype(BF16)

    @pl.when(step == pl.num_programs(1) - 1)
    def _():
        s_ref[...] = st_ref[...]


def _ret_prompt(proj, tabs, gn, l):
    intra, qd, kd, cd = tabs
    nsteps = SEQ // RET_RC
    tab_spec = pl.BlockSpec((RET_HEADS, RET_CHUNK, RET_CHUNK), lambda b, s: (0, 0, 0))

    def seg_spec(seg):
        return pl.BlockSpec((RET_RC, COL), lambda b, s, seg=seg: (b * nsteps + s, seg))

    return pl.pallas_call(
        _ret_prompt_kernel,
        grid=(BATCH, nsteps),
        in_specs=[
            seg_spec(0), seg_spec(1), seg_spec(2), seg_spec(3),
            tab_spec, tab_spec, tab_spec, tab_spec,
            pl.BlockSpec((None, RET_HEADS, 1, RET_DV), lambda b, s: (l, 0, 0, 0)),
        ],
        out_specs=[
            pl.BlockSpec((RET_RC, RET_W), lambda b, s: (b * nsteps + s, 0)),
            pl.BlockSpec((None, RET_HEADS, RET_DK, RET_DV), lambda b, s: (b, 0, 0, 0)),
        ],
        out_shape=[
            jax.ShapeDtypeStruct((M_ALL, RET_W), BF16),
            jax.ShapeDtypeStruct((BATCH, RET_HEADS, RET_DK, RET_DV), F32),
        ],
        scratch_shapes=[pltpu.VMEM((RET_HEADS, RET_DK, RET_DV), F32)],
        compiler_params=_cparams("parallel", "arbitrary"),
        name="ret_prompt",
    )(proj, proj, proj, proj, intra, qd, kd, cd, gn)


RET_SB = 8


def _col_bcast(row, eye, ones3):
    hi = row.astype(BF16)
    r1 = row - hi.astype(F32)
    mid = r1.astype(BF16)
    lo = (r1 - mid.astype(F32)).astype(BF16)
    parts = [jnp.where(eye, jnp.broadcast_to(t.astype(F32), (RET_DK, RET_DK)), 0.0).astype(BF16)
             for t in (hi, mid, lo)]
    return jnp.dot(jnp.concatenate(parts, axis=1), ones3, preferred_element_type=F32)


def _ret_sample_kernel(q_ref, k_ref, v_ref, rg_ref, s_ref, g1_ref, gn_ref, *rest):
    o_ref, so_ref = rest[-2:]
    eye = (lax.broadcasted_iota(jnp.int32, (RET_DK, RET_DK), 0)
           == lax.broadcasted_iota(jnp.int32, (RET_DK, RET_DK), 1))
    ones3 = jnp.ones((3 * RET_DK, RET_DV), BF16)

    for b in range(RET_SB):
        row = slice(b, b + 1)
        for h in range(RET_HEADS):
            cols = slice(h * RET_DK, (h + 1) * RET_DK)
            kcol = _col_bcast(k_ref[row, cols], eye, ones3)
            qcol = _col_bcast(q_ref[row, cols], eye, ones3)
            s_new = s_ref[b, h] * g1_ref[h] + kcol * v_ref[row, cols]
            so_ref[b, h] = s_new
            o = jnp.sum(qcol * s_new, axis=0, keepdims=True)
            o_ref[row, cols] = (_head_norm(o) * gn_ref[h]) * rg_ref[row, cols]


def _ret_sample(proj, state, g1, gn, l, prev_states):
    row0 = M_P // RET_SB

    def seg_spec(seg):
        return pl.BlockSpec((RET_SB, COL), lambda i, seg=seg: (row0 + i, seg))

    st_spec = pl.BlockSpec((None, RET_SB, RET_HEADS, RET_DK, RET_DV), lambda i: (l, i, 0, 0, 0))
    in_specs = [
        seg_spec(0), seg_spec(1), seg_spec(2), seg_spec(3), st_spec,
        pl.BlockSpec((RET_HEADS, 1, RET_DV), lambda i: (0, 0, 0)),
        pl.BlockSpec((None, RET_HEADS, 1, RET_DV), lambda i: (l, 0, 0, 0)),
    ]
    args = [proj, proj, proj, proj, state, g1, gn]
    aliases = {}
    if prev_states is not None:
        in_specs.append(pl.BlockSpec(memory_space=pl.ANY))
        args.append(prev_states)
        aliases = {len(args) - 1: 1}
    return pl.pallas_call(
        _ret_sample_kernel,
        grid=(M_S // RET_SB,),
        in_specs=in_specs,
        out_specs=[pl.BlockSpec((RET_SB, RET_W), lambda i: (i, 0)), st_spec],
        out_shape=[
            jax.ShapeDtypeStruct((M_S, RET_W), F32),
            jax.ShapeDtypeStruct((DEPTH, M_S, RET_HEADS, RET_DK, RET_DV), F32),
        ],
        input_output_aliases=aliases,
        compiler_params=_cparams("parallel"),
        name="ret_sample",
    )(*args)


def _s5_disc_kernel(lr_ref, li_ref, ls_ref, br_ref, bi_ref, lbr_ref, lbi_ref, bbr_ref, bbi_ref):
    lam_r = lr_ref[...]
    lam_i = li_ref[...]
    dt = jnp.exp(ls_ref[...])
    ea = jnp.exp(lam_r * dt)
    lb_r = ea * jnp.cos(lam_i * dt)
    lb_i = ea * jnp.sin(lam_i * dt)
    lbr_ref[...] = lb_r
    lbi_ref[...] = lb_i
    nr = lb_r - 1.0
    den = lam_r * lam_r + lam_i * lam_i
    cr = (nr * lam_r + lb_i * lam_i) / den
    ci = (lb_i * lam_r - nr * lam_i) / den
    b_r = br_ref[...]
    b_i = bi_ref[...]
    bbr_ref[...] = cr * b_r - ci * b_i
    bbi_ref[...] = cr * b_i + ci * b_r


def _s5_discretize(lam_re, lam_im, log_step, b_re, b_im):
    rows = DEPTH * S5_GROUPS * S5_GROUP_CH
    rep = lambda a: jnp.repeat(a.reshape(DEPTH * S5_GROUPS, S5_STATE), S5_GROUP_CH, axis=0)
    tr = lambda a: jnp.transpose(a, (0, 1, 3, 2)).reshape(rows, S5_STATE)
    shp = jax.ShapeDtypeStruct((rows, S5_STATE), F32)
    ls = jnp.broadcast_to(log_step[:, :, None], (DEPTH, S5_GROUPS, S5_STATE))
    spec = pl.BlockSpec((rows // DEPTH, S5_STATE), lambda l: (l, 0))
    return pl.pallas_call(
        _s5_disc_kernel,
        grid=(DEPTH,),
        in_specs=[spec] * 5,
        out_specs=[spec] * 4,
        out_shape=[shp, shp, shp, shp],
        compiler_params=_cparams("parallel"),
        name="s5_disc",
    )(rep(lam_re), rep(lam_im), rep(ls), tr(b_re), tr(b_im))


def _gelu_tanh(y):
    c = math.sqrt(2.0 / math.pi)
    return 0.5 * y * (1.0 + jnp.tanh(c * (y + 0.044715 * (y * y * y))))


def _s5_prompt_kernel(u0_ref, u1_ref, u2_ref, u3_ref, w_ref, cw_ref, lamr_ref, lami_ref, d_ref,
                      z_ref, hre_ref, him_ref, bre_ref, bim_ref):
    j = pl.program_id(0)
    u_refs = (u0_ref, u1_ref, u2_ref, u3_ref)

    @pl.when(j == 0)
    def _():
        hre_ref[...] = jnp.zeros_like(hre_ref)
        him_ref[...] = jnp.zeros_like(him_ref)

    nsl = S5_BLK_ST // LANES
    zpad = jnp.zeros((S5_PAD, S5_BLK_CH), F32)

    for gh in range(2):
        seg = S5_TC + 2 * S5_PAD * gh
        for kb in range(2):
            blk = gh * 2 + kb
            c0 = blk * S5_BLK_CH
            parts = []
            for u in u_refs:
                ub = u[:, c0:c0 + S5_BLK_CH]
                parts.extend([zpad, ub, zpad] if gh else [ub])
            x = jnp.concatenate(parts, axis=0).astype(BF16)
            res = jnp.dot(x, w_ref[blk], preferred_element_type=F32)
            for b in range(BATCH):
                r0 = (b * 2 + gh) * S5_PITCH - S5_PAD * gh
                part = res[b * seg:(b + 1) * seg]
                for s in range(nsl):
                    slab = kb * nsl + s
                    bre_ref[slab, r0:r0 + seg, :] = part[:, s * LANES:(s + 1) * LANES]
                    bim_ref[slab, r0:r0 + seg, :] = part[:, S5_BLK_ST + s * LANES:S5_BLK_ST + (s + 1) * LANES]

    for sg in range(S5_NSLAB // S5_SLAB_GROUP):
        slabs = [sg * S5_SLAB_GROUP + i for i in range(S5_SLAB_GROUP)]
        lam_r = [lamr_ref[s] for s in slabs]
        lam_i = [lami_ref[s] for s in slabs]

        def step(t, carry, slabs=slabs, lam_r=lam_r, lam_i=lam_i):
            hr, hi = carry
            nr_all = []
            ni_all = []
            for i, s in enumerate(slabs):
                idx = pl.ds(t, S5_NSEQ, stride=S5_PITCH)
                br = bre_ref[s, idx, :]
                bi = bim_ref[s, idx, :]
                nr = lam_r[i] * hr[i] - lam_i[i] * hi[i] + br
                ni = lam_r[i] * hi[i] + lam_i[i] * hr[i] + bi
                bre_ref[s, idx, :] = nr
                bim_ref[s, idx, :] = ni
                nr_all.append(nr)
                ni_all.append(ni)
            return tuple(nr_all), tuple(ni_all)

        init = (tuple(hre_ref[s] for s in slabs), tuple(him_ref[s] for s in slabs))
        hr, hi = lax.fori_loop(0, S5_TC, step, init, unroll=2)
        for i, s in enumerate(slabs):
            hre_ref[s] = hr[i]
            him_ref[s] = hi[i]

    for gh in range(2):
        seg = S5_TC + 2 * S5_PAD * gh
        for kb in range(2):
            blk = gh * 2 + kb
            c0 = blk * S5_BLK_CH
            pieces = []
            for src in (bre_ref, bim_ref):
                for s in range(nsl):
                    slab = kb * nsl + s
                    starts = [(b * 2 + gh) * S5_PITCH - S5_PAD * gh for b in range(BATCH)]
                    pieces.append(jnp.concatenate(
                        [src[slab, r0:r0 + seg, :] for r0 in starts], axis=0))
            hcat = jnp.concatenate(pieces, axis=1).astype(BF16)
            y = jnp.dot(hcat, cw_ref[blk], preferred_element_type=F32)
            dvec = d_ref[:, c0:c0 + S5_BLK_CH]
            for b in range(BATCH):
                lo = b * seg + S5_PAD * gh
                yb = y[lo:lo + S5_TC] + dvec * u_refs[b][:, c0:c0 + S5_BLK_CH]
                z_ref[b, :, c0:c0 + S5_BLK_CH] = _gelu_tanh(yb)


def _s5_prompt(proj, w_blk, cw_blk, lamr_t, lami_t, dvec, l):
    nsteps = SEQ // S5_TC
    su_col = (2 * RET_QK + 2 * RET_W) // S5_W

    def u_spec(b):
        return pl.BlockSpec((S5_TC, S5_W), lambda j, b=b: (b * nsteps + j, su_col))

    layer4 = lambda shape: pl.BlockSpec((None,) + shape, lambda j: (l, 0, 0, 0))
    full3 = lambda shape: pl.BlockSpec(shape, lambda j: (0, 0, 0))
    st_shape = (S5_NSLAB, S5_NSEQ, LANES)
    return pl.pallas_call(
        _s5_prompt_kernel,
        grid=(nsteps,),
        in_specs=[
            u_spec(0), u_spec(1), u_spec(2), u_spec(3),
            layer4((S5_NBLK, S5_BLK_CH, 2 * S5_BLK_ST)),
            layer4((S5_NBLK, 2 * S5_BLK_ST, S5_BLK_CH)),
            layer4(st_shape), layer4(st_shape),
            _layer_row_spec(l, S5_W),
        ],
        out_specs=[
            pl.BlockSpec((BATCH, S5_TC, S5_W), lambda j: (0, j, 0)),
            full3(st_shape), full3(st_shape),
        ],
        out_shape=[
            jax.ShapeDtypeStruct((BATCH + 1, SEQ, S5_W), F32),
            jax.ShapeDtypeStruct(st_shape, F32),
            jax.ShapeDtypeStruct(st_shape, F32),
        ],
        scratch_shapes=[
            pltpu.VMEM((S5_NSLAB, S5_NSEQ * S5_PITCH, LANES), F32),
            pltpu.VMEM((S5_NSLAB, S5_NSEQ * S5_PITCH, LANES), F32),
        ],
        compiler_params=_cparams("arbitrary"),
        name="s5_prompt",
    )(proj, proj, proj, proj, w_blk, cw_blk, lamr_t, lami_t, dvec)


def _s5_sample_kernel(u_ref, h0r_ref, h0i_ref, w_ref, cw_ref, lbr_ref, lbi_ref, d_ref,
                      z_ref, hr_ref, hi_ref):
    for blk in range(S5_NBLK):
        c0 = blk * S5_BLK_CH
        s0 = blk * S5_BLK_ST
        u = u_ref[:, c0:c0 + S5_BLK_CH]
        res = jnp.dot(u.astype(BF16), w_ref[blk], preferred_element_type=F32)
        lr = lbr_ref[:, s0:s0 + S5_BLK_ST]
        li = lbi_ref[:, s0:s0 + S5_BLK_ST]
        h0r = h0r_ref[:, s0:s0 + S5_BLK_ST]
        h0i = h0i_ref[:, s0:s0 + S5_BLK_ST]
        hr = lr * h0r - li * h0i + res[:, :S5_BLK_ST]
        hi = lr * h0i + li * h0r + res[:, S5_BLK_ST:]
        hr_ref[:, s0:s0 + S5_BLK_ST] = hr
        hi_ref[:, s0:s0 + S5_BLK_ST] = hi
        hcat = jnp.concatenate([hr, hi], axis=1).astype(BF16)
        y = jnp.dot(hcat, cw_ref[blk], preferred_element_type=F32) + d_ref[:, c0:c0 + S5_BLK_CH] * u
        z_ref[:, c0:c0 + S5_BLK_CH] = _gelu_tanh(y)


def _s5_sample(proj, h0r, h0i, w_blk, cw_blk, lbr, lbi, dvec, l):
    nst = S5_GROUPS * S5_STATE
    su_col = (2 * RET_QK + 2 * RET_W) // S5_W
    layer3 = lambda shape: pl.BlockSpec((None,) + shape, lambda i: (l, 0, 0))
    layer4 = lambda shape: pl.BlockSpec((None,) + shape, lambda i: (l, 0, 0, 0))
    whole = lambda shape: pl.BlockSpec(shape, lambda i: (0, 0))
    return pl.pallas_call(
        _s5_sample_kernel,
        grid=(1,),
        in_specs=[
            pl.BlockSpec((M_S, S5_W), lambda i: (M_P // M_S, su_col)),
            layer3((M_S, nst)), layer3((M_S, nst)),
            layer4((S5_NBLK, S5_BLK_CH, 2 * S5_BLK_ST)),
            layer4((S5_NBLK, 2 * S5_BLK_ST, S5_BLK_CH)),
            layer3((1, nst)), layer3((1, nst)),
            _layer_row_spec(l, S5_W),
        ],
        out_specs=[whole((M_S, S5_W)), whole((M_S, nst)), whole((M_S, nst))],
        out_shape=[
            jax.ShapeDtypeStruct((M_S, S5_W), F32),
            jax.ShapeDtypeStruct((M_S, nst), F32),
            jax.ShapeDtypeStruct((M_S, nst), F32),
        ],
        compiler_params=_cparams("arbitrary"),
        name="s5_sample",
    )(proj, h0r, h0i, w_blk, cw_blk, lbr, lbi, dvec)


TM_MERGE = 520


def _merge_kernel(z_ref, o_ref, gr_ref, gs_ref, gluw_ref, glub_ref, s5p_ref, retp_ref, m_ref, z2_ref):
    @pl.when(pl.program_id(1) == 0)
    def _():
        z = z_ref[...]
        t = (jnp.dot(z.astype(BF16), gluw_ref[...].astype(BF16), preferred_element_type=F32)
             + glub_ref[...])
        z2_ref[...] = (z * jax.nn.sigmoid(t)).astype(BF16)

    b_s5 = jnp.dot(z2_ref[...], s5p_ref[...].astype(BF16), preferred_element_type=F32)
    b_ret = jnp.dot(o_ref[...], retp_ref[...].astype(BF16), preferred_element_type=F32)
    m_ref[...] = (gr_ref[...] * b_ret + gs_ref[...] * b_s5).astype(BF16)


def _merge(z, o, proj, glu_w, glu_b, s5_proj, ret_proj, l):
    gr_col = (2 * RET_QK + 2 * RET_W + S5_W) // COL
    gs_col = gr_col + D_MODEL // COL
    return pl.pallas_call(
        _merge_kernel,
        grid=(M_ALL // TM_MERGE, D_MODEL // COL),
        in_specs=[
            pl.BlockSpec((TM_MERGE, S5_W), lambda i, j: (i, 0)),
            pl.BlockSpec((TM_MERGE, RET_W), lambda i, j: (i, 0)),
            pl.BlockSpec((TM_MERGE, COL), lambda i, j: (i, gr_col + j)),
            pl.BlockSpec((TM_MERGE, COL), lambda i, j: (i, gs_col + j)),
            pl.BlockSpec((None, S5_W, S5_W), lambda i, j: (l, 0, 0)),
            _layer_row_spec(l, S5_W),
            _layer_spec(l, S5_W, COL),
            _layer_spec(l, RET_W, COL),
        ],
        out_specs=pl.BlockSpec((TM_MERGE, COL), lambda i, j: (i, j)),
        out_shape=jax.ShapeDtypeStruct((M_ALL, D_MODEL), BF16),
        scratch_shapes=[pltpu.VMEM((TM_MERGE, S5_W), BF16)],
        compiler_params=_cparams("parallel", "arbitrary"),
        name="merge",
    )(z, o, proj, proj, glu_w, glu_b, s5_proj, ret_proj)


def _final_norm_kernel(x_ref, g_ref, o_ref):
    x = x_ref[...]
    ms = jnp.mean(x * x, axis=-1, keepdims=True)
    o_ref[...] = (x * lax.rsqrt(ms + NORM_EPS)) * g_ref[...]


def _final_norm(x, g, rows, tm, block0):
    return pl.pallas_call(
        _final_norm_kernel,
        grid=(rows // tm,),
        in_specs=[pl.BlockSpec((tm, D_MODEL), lambda i: (block0 + i, 0)),
                  pl.BlockSpec((1, D_MODEL), lambda i: (0, 0))],
        out_specs=pl.BlockSpec((tm, D_MODEL), lambda i: (i, 0)),
        out_shape=jax.ShapeDtypeStruct((rows, D_MODEL), F32),
        compiler_params=_cparams("parallel"),
        name="final_norm",
    )(x, g)


def _rope_tables():
    half = RET_DK // 2
    inv = ROPE_BASE ** (-jnp.arange(half, dtype=F32) / half)
    pos = jnp.concatenate([jnp.tile(jnp.arange(SEQ, dtype=jnp.int32), BATCH),
                           jnp.full((M_S,), PAST_LEN, jnp.int32)])
    ang = pos.astype(F32)[:, None] * inv[None, :]
    cos = jnp.cos(ang)
    sin = jnp.sin(ang)
    return jnp.concatenate([cos, cos], axis=1), jnp.concatenate([-sin, sin], axis=1)


def _retention_tables():
    c = RET_CHUNK
    log_g = jnp.log1p(-jnp.exp2(-5.0 - jnp.arange(RET_HEADS, dtype=F32)))
    i = jnp.arange(c, dtype=F32)
    diff = i[:, None] - i[None, :]
    intra = jnp.where(diff[None] >= 0,
                      jnp.exp(jnp.maximum(diff, 0.0)[None] * log_g[:, None, None]), 0.0)
    q_dec = jnp.exp((i + 1.0)[None, :] * log_g[:, None])
    k_dec = jnp.exp((c - 1.0 - i)[None, :] * log_g[:, None])
    c_dec = jnp.exp(c * log_g)
    full = lambda a: jnp.broadcast_to(a, (RET_HEADS, c, RET_DV))
    g1 = jnp.broadcast_to(jnp.exp(1.0 * log_g)[:, None, None], (RET_HEADS, 1, RET_DV))
    return (intra, full(q_dec[:, :, None]), full(k_dec[:, :, None]), full(c_dec[:, None, None])), g1


def _s5_block_weights(bbr_t, bbi_t, c_re, c_im):
    eye = jnp.eye(S5_GB, dtype=F32)
    n = DEPTH * S5_NBLK

    def drive(bt):
        a = bt.reshape(n, S5_GB, S5_GROUP_CH, S5_STATE)
        return jnp.einsum('ngcp,gh->ngchp', a, eye).reshape(DEPTH, S5_NBLK, S5_BLK_CH, S5_BLK_ST)

    def readout(cm):
        a = cm.reshape(n, S5_GB, S5_GROUP_CH, S5_STATE)
        return jnp.einsum('ngcp,gh->ngphc', a, eye).reshape(DEPTH, S5_NBLK, S5_BLK_ST, S5_BLK_CH)

    w_blk = jnp.concatenate([drive(bbr_t), drive(bbi_t)], axis=3).astype(BF16)
    cw_blk = jnp.concatenate([readout(c_re), readout(-c_im)], axis=2).astype(BF16)
    return w_blk, cw_blk


def _slab_tiles(lb):
    a = lb.reshape(DEPTH, 2, S5_NSLAB, LANES).transpose(0, 2, 1, 3)
    return jnp.tile(a, (1, 1, BATCH, 1))


def kernel(x_prompt, x_sample, state_ret, state_s5_re, state_s5_im, ffn1_norm, ffn1_w1, ffn1_w3, ffn1_w2, mix_norm, w_in, ret_gn, ret_proj, s5_lam_re, s5_lam_im, s5_b_re, s5_b_im, s5_c_re, s5_c_im, s5_d, s5_log_step, glu_w, glu_b, s5_proj, w_out, ffn2_norm, ffn2_w1, ffn2_w3, ffn2_w2, final_norm):
    cos_t, sin_t = _rope_tables()
    ret_tabs, g1 = _retention_tables()
    x = jnp.concatenate([x_prompt.reshape(M_P, D_MODEL), x_sample.reshape(M_S, D_MODEL)], axis=0)

    ret_p, s5r_p, s5i_p, s5r_s, s5i_s = [], [], [], [], []
    ret_s = None
    nst = S5_GROUPS * S5_STATE
    rows3 = lambda a: a.reshape(DEPTH, 1, -1)
    ffn1_g, mix_g, ffn2_g, glu_b3 = rows3(ffn1_norm), rows3(mix_norm), rows3(ffn2_norm), rows3(glu_b)
    gn4 = ret_gn.reshape(DEPTH, RET_HEADS, 1, RET_DV)

    lbr_x, lbi_x, bbr_t, bbi_t = _s5_discretize(s5_lam_re, s5_lam_im, s5_log_step, s5_b_re, s5_b_im)
    lbr = lbr_x[::S5_GROUP_CH].reshape(DEPTH, nst)
    lbi = lbi_x[::S5_GROUP_CH].reshape(DEPTH, nst)
    w_blk, cw_blk = _s5_block_weights(bbr_t, bbi_t, s5_c_re, s5_c_im)
    lamr_t, lami_t = _slab_tiles(lbr), _slab_tiles(lbi)
    lbr3, lbi3 = lbr.reshape(DEPTH, 1, nst), lbi.reshape(DEPTH, 1, nst)
    dvec3 = s5_d.reshape(DEPTH, 1, S5_W)
    h0r_all = state_s5_re.reshape(DEPTH, M_S, nst)
    h0i_all = state_s5_im.reshape(DEPTH, M_S, nst)

    for l in range(DEPTH):
        hid = _ffn_up(_norm(x, ffn1_g, l), ffn1_w1, ffn1_w3, l)
        x = _mm_res(hid, ffn1_w2, x, l, scale=0.5, tm=TM_DOWN, tn=TN_DOWN, name="ffn_down")

        proj = _inproj(_norm(x, mix_g, l), w_in, cos_t, sin_t, l)

        o_all, s_p = _ret_prompt(proj, ret_tabs, gn4, l)
        o_s, ret_s = _ret_sample(proj, state_ret, g1, gn4, l, ret_s)
        o_all = lax.dynamic_update_slice(o_all, o_s.astype(BF16), (M_P, 0))

        z5, hre_p, him_p = _s5_prompt(proj, w_blk, cw_blk, lamr_t, lami_t, dvec3, l)
        z_s, hre_s, him_s = _s5_sample(proj, h0r_all, h0i_all, w_blk, cw_blk, lbr3, lbi3, dvec3, l)
        z = lax.dynamic_update_slice(z5.reshape((BATCH + 1) * SEQ, S5_W), z_s, (M_P, 0))

        m = _merge(z, o_all, proj, glu_w, glu_b3, s5_proj, ret_proj, l)
        x = _mm_res(m, w_out, x, l, scale=1.0, tm=TM, tn=COL, name="out_proj")

        hid = _ffn_up(_norm(x, ffn2_g, l), ffn2_w1, ffn2_w3, l)
        x = _mm_res(hid, ffn2_w2, x, l, scale=0.5, tm=TM_DOWN, tn=TN_DOWN, name="ffn_down")

        unslab = lambda a: (a.transpose(1, 0, 2).reshape(BATCH, 2, S5_HALF_ST)
                            .reshape(BATCH, S5_GROUPS, S5_STATE))
        ret_p.append(s_p)
        s5r_p.append(unslab(hre_p))
        s5i_p.append(unslab(him_p))
        s5r_s.append(hre_s.reshape(M_S, S5_GROUPS, S5_STATE))
        s5i_s.append(him_s.reshape(M_S, S5_GROUPS, S5_STATE))

    g_fin = final_norm.reshape(1, D_MODEL)
    y_p = _final_norm(x, g_fin, M_P, 1024, 0)
    y_s = _final_norm(x, g_fin, M_S, M_S, M_P // M_S)
    return (y_p.reshape(BATCH, SEQ, D_MODEL), y_s.reshape(M_S, 1, D_MODEL),
            jnp.stack(ret_p), jnp.stack(s5r_p), jnp.stack(s5i_p),
            ret_s, jnp.stack(s5r_s), jnp.stack(s5i_s))
```

```python
import functools
import math

import jax
import jax.numpy as jnp
from jax import lax
from jax.experimental import pallas as pl
from jax.experimental.pallas import tpu as pltpu

F32 = jnp.float32
BF16 = jnp.bfloat16

D_MODEL = 2048
BATCH = 4
SEQ = 2048
DEPTH = 4
DEC_BATCH = 128
PAST_LEN = 16384
RET_HEADS = 8
RET_DK = 128
RET_DV = 128
RET_QK = RET_HEADS * RET_DK
RET_W = RET_HEADS * RET_DV
RET_CHUNK = 128
ROPE_BASE = 10000.0
S5_GROUPS = 64
S5_GROUP_CH = 16
S5_W = S5_GROUPS * S5_GROUP_CH
S5_STATE = 64
D_FF = 5632
IN_W = 2 * RET_QK + 2 * RET_W + S5_W + 2 * D_MODEL
NORM_EPS = 1e-6
HEAD_NORM_EPS = 1e-5

M_P = BATCH * SEQ
M_S = DEC_BATCH
M_ALL = M_P + M_S

LANES = 128
SUBLANES = 8
VMEM_LIMIT = 58 * 1024 * 1024

TM = 1040
TN_FF = 512
TM_DOWN = 832
TN_DOWN = 512
COL = 1024
TM_BIG = 4 * TM
TN_IN = 512
IN_SUB = 256

S5_TC = 64
S5_PAD = SUBLANES // 2
S5_PITCH = S5_TC + S5_PAD
S5_GB = 16
S5_NBLK = S5_GROUPS // S5_GB
S5_BLK_CH = S5_GB * S5_GROUP_CH
S5_BLK_ST = S5_GB * S5_STATE
S5_HALF_ST = S5_GROUPS * S5_STATE // 2
S5_NSLAB = S5_HALF_ST // LANES
S5_NSEQ = BATCH * 2
S5_SLAB_GROUP = 4


def _cparams(*sem):
    return pltpu.CompilerParams(dimension_semantics=sem, vmem_limit_bytes=VMEM_LIMIT)


def _rms_to_bf16(x, g):
    ms = jnp.mean(x * x, axis=-1, keepdims=True)
    return ((x * lax.rsqrt(ms + NORM_EPS)) * g).astype(BF16)


def _layer_spec(l, k, tn):
    return pl.BlockSpec((None, k, tn), lambda i, j: (l, 0, j))


def _layer_row_spec(l, n):
    return pl.BlockSpec((None, 1, n), lambda *_: (l, 0, 0))


def _norm_kernel(x_ref, g_ref, o_ref):
    o_ref[...] = _rms_to_bf16(x_ref[...], g_ref[...])


def _norm(x, g, l):
    return pl.pallas_call(
        _norm_kernel,
        grid=(M_ALL // TM,),
        in_specs=[pl.BlockSpec((TM, D_MODEL), lambda i: (i, 0)), _layer_row_spec(l, D_MODEL)],
        out_specs=pl.BlockSpec((TM, D_MODEL), lambda i: (i, 0)),
        out_shape=jax.ShapeDtypeStruct((M_ALL, D_MODEL), BF16),
        compiler_params=_cparams("parallel"),
        name="norm",
    )(x, g)


def _resident_rows_spec():
    return pl.BlockSpec((TM_BIG, D_MODEL), lambda r, j: (r, 0), pipeline_mode=pl.Buffered(1))


def _ffn_up_kernel(xn_ref, w1_ref, w3_ref, o_ref):
    w1 = w1_ref[...].astype(BF16)
    w3 = w3_ref[...].astype(BF16)
    for s in range(TM_BIG // TM):
        rows = slice(s * TM, (s + 1) * TM)
        xs = xn_ref[rows, :]
        h1 = jnp.dot(xs, w1, preferred_element_type=F32)
        h3 = jnp.dot(xs, w3, preferred_element_type=F32)
        o_ref[rows, :] = ((h1 * jax.nn.sigmoid(h1)) * h3).astype(BF16)


def _ffn_up(xn, w1, w3, l):
    return pl.pallas_call(
        _ffn_up_kernel,
        grid=(M_ALL // TM_BIG, D_FF // TN_FF),
        in_specs=[
            _resident_rows_spec(),
            _layer_spec(l, D_MODEL, TN_FF),
            _layer_spec(l, D_MODEL, TN_FF),
        ],
        out_specs=pl.BlockSpec((TM_BIG, TN_FF), lambda r, j: (r, j)),
        out_shape=jax.ShapeDtypeStruct((M_ALL, D_FF), BF16),
        compiler_params=_cparams("parallel", "arbitrary"),
        name="ffn_up",
    )(xn, w1, w3)


def _mm_res_kernel(a_ref, w_ref, x_ref, o_ref, *, scale):
    acc = jnp.dot(a_ref[...], w_ref[...].astype(BF16), preferred_element_type=F32)
    o_ref[...] = x_ref[...] + scale * acc


def _mm_res(a, w, x, l, *, scale, tm, tn, name):
    k = a.shape[1]
    n = w.shape[2]
    return pl.pallas_call(
        functools.partial(_mm_res_kernel, scale=scale),
        grid=(M_ALL // tm, n // tn),
        in_specs=[
            pl.BlockSpec((tm, k), lambda i, j: (i, 0)),
            _layer_spec(l, k, tn),
            pl.BlockSpec((tm, tn), lambda i, j: (i, j)),
        ],
        out_specs=pl.BlockSpec((tm, tn), lambda i, j: (i, j)),
        out_shape=jax.ShapeDtypeStruct((M_ALL, n), F32),
        compiler_params=_cparams("parallel", "arbitrary"),
        name=name,
    )(a, w, x)


def _inproj_kernel(xn_ref, w_ref, cos_ref, sin_ref, o_ref):
    seg = pl.program_id(1) // (COL // TN_IN)

    def project(epilogue):
        wb = [w_ref[:, c * IN_SUB:(c + 1) * IN_SUB].astype(BF16) for c in range(TN_IN // IN_SUB)]
        for s in range(TM_BIG // TM):
            rows = slice(s * TM, (s + 1) * TM)
            for c in range(TN_IN // IN_SUB):
                acc = jnp.dot(xn_ref[rows, :], wb[c], preferred_element_type=F32)
                o_ref[rows, c * IN_SUB:(c + 1) * IN_SUB] = epilogue(acc, rows)

    def rotary(acc, rows):
        scale = jnp.where(seg == 0, RET_DK ** -0.5, 1.0).astype(F32)
        cos = cos_ref[rows, :]
        sin = sin_ref[rows, :]
        heads = []
        for h in range(IN_SUB // RET_DK):
            a = acc[:, h * RET_DK:(h + 1) * RET_DK]
            heads.append((a * cos + pltpu.roll(a, RET_DK // 2, axis=1) * sin) * scale)
        return jnp.concatenate(heads, axis=1)

    @pl.when(seg <= 1)
    def _():
        project(rotary)

    @pl.when((seg == 2) | (seg == 4))
    def _():
        project(lambda acc, rows: acc)

    @pl.when(seg == 3)
    def _():
        project(lambda acc, rows: acc * jax.nn.sigmoid(acc))

    @pl.when(seg >= 5)
    def _():
        project(lambda acc, rows: jax.nn.sigmoid(acc))


def _inproj(xn, w, cos_t, sin_t, l):
    tab_spec = pl.BlockSpec((TM_BIG, RET_DK), lambda r, j: (r, 0), pipeline_mode=pl.Buffered(1))
    return pl.pallas_call(
        _inproj_kernel,
        grid=(M_ALL // TM_BIG, IN_W // TN_IN),
        in_specs=[
            _resident_rows_spec(),
            _layer_spec(l, D_MODEL, TN_IN),
            tab_spec, tab_spec,
        ],
        out_specs=pl.BlockSpec((TM_BIG, TN_IN), lambda r, j: (r, j)),
        out_shape=jax.ShapeDtypeStruct((M_ALL, IN_W), F32),
        compiler_params=_cparams("parallel", "arbitrary"),
        name="inproj",
    )(xn, w, cos_t, sin_t)


def _head_norm(o):
    mu = jnp.mean(o, axis=-1, keepdims=True)
    d = o - mu
    var = jnp.mean(d * d, axis=-1, keepdims=True)
    return d * lax.rsqrt(var + HEAD_NORM_EPS)


RET_RC = 4 * RET_CHUNK


def _ret_prompt_kernel(q_ref, k_ref, v_ref, rg_ref, intra_ref, qd_ref, kd_ref, cd_ref, gn_ref,
                       o_ref, s_ref, st_ref):
    step = pl.program_id(1)

    @pl.when(step == 0)
    def _():
        st_ref[...] = jnp.zeros_like(st_ref)

    for c in range(RET_RC // RET_CHUNK):
        rows = slice(c * RET_CHUNK, (c + 1) * RET_CHUNK)
        for h in range(RET_HEADS):
            cols = slice(h * RET_DK, (h + 1) * RET_DK)
            qc = q_ref[rows, cols]
            kc = k_ref[rows, cols]
            vb = v_ref[rows, cols].astype(BF16)
            s = st_ref[h]
            sc = lax.dot_general(qc.astype(BF16), kc.astype(BF16), (((1,), (1,)), ((), ())),
                                 preferred_element_type=F32) * intra_ref[h]
            o = (jnp.dot(sc.astype(BF16), vb, preferred_element_type=F32)
                 + jnp.dot((qc * qd_ref[h]).astype(BF16), s.astype(BF16),
                           preferred_element_type=F32))
            kt = jnp.transpose(kc * kd_ref[h]).astype(BF16)
            st_ref[h] = s * cd_ref[h] + jnp.dot(kt, vb, preferred_element_type=F32)
            on = (_head_norm(o) * gn_ref[h]) * rg_ref[rows, cols]
            o_ref[rows, cols] = on.astype(BF16)

    @pl.when(step == pl.num_programs(1) - 1)
    def _():
        s_ref[...] = st_ref[...]


def _ret_prompt(proj, tabs, gn, l):
    intra, qd, kd, cd = tabs
    nsteps = SEQ // RET_RC
    tab_spec = pl.BlockSpec((RET_HEADS, RET_CHUNK, RET_CHUNK), lambda b, s: (0, 0, 0))

    def seg_spec(seg):
        return pl.BlockSpec((RET_RC, COL), lambda b, s, seg=seg: (b * nsteps + s, seg))

    return pl.pallas_call(
        _ret_prompt_kernel,
        grid=(BATCH, nsteps),
        in_specs=[
            seg_spec(0), seg_spec(1), seg_spec(2), seg_spec(3),
            tab_spec, tab_spec, tab_spec, tab_spec,
            pl.BlockSpec((None, RET_HEADS, 1, RET_DV), lambda b, s: (l, 0, 0, 0)),
        ],
        out_specs=[
            pl.BlockSpec((RET_RC, RET_W), lambda b, s: (b * nsteps + s, 0)),
            pl.BlockSpec((None, RET_HEADS, RET_DK, RET_DV), lambda b, s: (b, 0, 0, 0)),
        ],
        out_shape=[
            jax.ShapeDtypeStruct((M_ALL, RET_W), BF16),
            jax.ShapeDtypeStruct((BATCH, RET_HEADS, RET_DK, RET_DV), F32),
        ],
        scratch_shapes=[pltpu.VMEM((RET_HEADS, RET_DK, RET_DV), F32)],
        compiler_params=_cparams("parallel", "arbitrary"),
        name="ret_prompt",
    )(proj, proj, proj, proj, intra, qd, kd, cd, gn)


RET_SB = 8


def _col_bcast(row, eye, ones3):
    hi = row.astype(BF16)
    r1 = row - hi.astype(F32)
    mid = r1.astype(BF16)
    lo = (r1 - mid.astype(F32)).astype(BF16)
    parts = [jnp.where(eye, jnp.broadcast_to(t.astype(F32), (RET_DK, RET_DK)), 0.0).astype(BF16)
             for t in (hi, mid, lo)]
    return jnp.dot(jnp.concatenate(parts, axis=1), ones3, preferred_element_type=F32)


def _ret_sample_kernel(q_ref, k_ref, v_ref, rg_ref, s_ref, g1_ref, gn_ref, *rest):
    o_ref, so_ref = rest[-2:]
    eye = (lax.broadcasted_iota(jnp.int32, (RET_DK, RET_DK), 0)
           == lax.broadcasted_iota(jnp.int32, (RET_DK, RET_DK), 1))
    ones3 = jnp.ones((3 * RET_DK, RET_DV), BF16)

    for b in range(RET_SB):
        row = slice(b, b + 1)
        for h in range(RET_HEADS):
            cols = slice(h * RET_DK, (h + 1) * RET_DK)
            kcol = _col_bcast(k_ref[row, cols], eye, ones3)
            qcol = _col_bcast(q_ref[row, cols], eye, ones3)
            s_new = s_ref[b, h] * g1_ref[h] + kcol * v_ref[row, cols]
            so_ref[b, h] = s_new
            o = jnp.sum(qcol * s_new, axis=0, keepdims=True)
            o_ref[row, cols] = (_head_norm(o) * gn_ref[h]) * rg_ref[row, cols]


def _ret_sample(proj, state, g1, gn, l, prev_states):
    row0 = M_P // RET_SB

    def seg_spec(seg):
        return pl.BlockSpec((RET_SB, COL), lambda i, seg=seg: (row0 + i, seg))

    st_spec = pl.BlockSpec((None, RET_SB, RET_HEADS, RET_DK, RET_DV), lambda i: (l, i, 0, 0, 0))
    in_specs = [
        seg_spec(0), seg_spec(1), seg_spec(2), seg_spec(3), st_spec,
        pl.BlockSpec((RET_HEADS, 1, RET_DV), lambda i: (0, 0, 0)),
        pl.BlockSpec((None, RET_HEADS, 1, RET_DV), lambda i: (l, 0, 0, 0)),
    ]
    args = [proj, proj, proj, proj, state, g1, gn]
    aliases = {}
    if prev_states is not None:
        in_specs.append(pl.BlockSpec(memory_space=pl.ANY))
        args.append(prev_states)
        aliases = {len(args) - 1: 1}
    return pl.pallas_call(
        _ret_sample_kernel,
        grid=(M_S // RET_SB,),
        in_specs=in_specs,
        out_specs=[pl.BlockSpec((RET_SB, RET_W), lambda i: (i, 0)), st_spec],
        out_shape=[
            jax.ShapeDtypeStruct((M_S, RET_W), F32),
            jax.ShapeDtypeStruct((DEPTH, M_S, RET_HEADS, RET_DK, RET_DV), F32),
        ],
        input_output_aliases=aliases,
        compiler_params=_cparams("parallel"),
        name="ret_sample",
    )(*args)


def _s5_disc_kernel(lr_ref, li_ref, ls_ref, br_ref, bi_ref, lbr_ref, lbi_ref, bbr_ref, bbi_ref):
    lam_r = lr_ref[...]
    lam_i = li_ref[...]
    dt = jnp.exp(ls_ref[...])
    ea = jnp.exp(lam_r * dt)
    lb_r = ea * jnp.cos(lam_i * dt)
    lb_i = ea * jnp.sin(lam_i * dt)
    lbr_ref[...] = lb_r
    lbi_ref[...] = lb_i
    nr = lb_r - 1.0
    den = lam_r * lam_r + lam_i * lam_i
    cr = (nr * lam_r + lb_i * lam_i) / den
    ci = (lb_i * lam_r - nr * lam_i) / den
    b_r = br_ref[...]
    b_i = bi_ref[...]
    bbr_ref[...] = cr * b_r - ci * b_i
    bbi_ref[...] = cr * b_i + ci * b_r


def _s5_discretize(lam_re, lam_im, log_step, b_re, b_im):
    rows = DEPTH * S5_GROUPS * S5_GROUP_CH
    rep = lambda a: jnp.repeat(a.reshape(DEPTH * S5_GROUPS, S5_STATE), S5_GROUP_CH, axis=0)
    tr = lambda a: jnp.transpose(a, (0, 1, 3, 2)).reshape(rows, S5_STATE)
    shp = jax.ShapeDtypeStruct((rows, S5_STATE), F32)
    ls = jnp.broadcast_to(log_step[:, :, None], (DEPTH, S5_GROUPS, S5_STATE))
    spec = pl.BlockSpec((rows // DEPTH, S5_STATE), lambda l: (l, 0))
    return pl.pallas_call(
        _s5_disc_kernel,
        grid=(DEPTH,),
        in_specs=[spec] * 5,
        out_specs=[spec] * 4,
        out_shape=[shp, shp, shp, shp],
        compiler_params=_cparams("parallel"),
        name="s5_disc",
    )(rep(lam_re), rep(lam_im), rep(ls), tr(b_re), tr(b_im))


def _gelu_tanh(y):
    c = math.sqrt(2.0 / math.pi)
    return 0.5 * y * (1.0 + jnp.tanh(c * (y + 0.044715 * (y * y * y))))


def _s5_prompt_kernel(u0_ref, u1_ref, u2_ref, u3_ref, w_ref, cw_ref, lamr_ref, lami_ref, d_ref,
                      z_ref, hre_ref, him_ref, bre_ref, bim_ref):
    j = pl.program_id(0)
    u_refs = (u0_ref, u1_ref, u2_ref, u3_ref)

    @pl.when(j == 0)
    def _():
        hre_ref[...] = jnp.zeros_like(hre_ref)
        him_ref[...] = jnp.zeros_like(him_ref)

    nsl = S5_BLK_ST // LANES
    zpad = jnp.zeros((S5_PAD, S5_BLK_CH), F32)

    for gh in range(2):
        seg = S5_TC + 2 * S5_PAD * gh
        for kb in range(2):
            blk = gh * 2 + kb
            c0 = blk * S5_BLK_CH
            parts = []
            for u in u_refs:
                ub = u[:, c0:c0 + S5_BLK_CH]
                parts.extend([zpad, ub, zpad] if gh else [ub])
            x = jnp.concatenate(parts, axis=0).astype(BF16)
            res = jnp.dot(x, w_ref[blk], preferred_element_type=F32)
            for b in range(BATCH):
                r0 = (b * 2 + gh) * S5_PITCH - S5_PAD * gh
                part = res[b * seg:(b + 1) * seg]
                for s in range(nsl):
                    slab = kb * nsl + s
                    bre_ref[slab, r0:r0 + seg, :] = part[:, s * LANES:(s + 1) * LANES]
                    bim_ref[slab, r0:r0 + seg, :] = part[:, S5_BLK_ST + s * LANES:S5_BLK_ST + (s + 1) * LANES]

    for sg in range(S5_NSLAB // S5_SLAB_GROUP):
        slabs = [sg * S5_SLAB_GROUP + i for i in range(S5_SLAB_GROUP)]
        lam_r = [lamr_ref[s] for s in slabs]
        lam_i = [lami_ref[s] for s in slabs]

        def step(t, carry, slabs=slabs, lam_r=lam_r, lam_i=lam_i):
            hr, hi = carry
            nr_all = []
            ni_all = []
            for i, s in enumerate(slabs):
                idx = pl.ds(t, S5_NSEQ, stride=S5_PITCH)
                br = bre_ref[s, idx, :]
                bi = bim_ref[s, idx, :]
                nr = lam_r[i] * hr[i] - lam_i[i] * hi[i] + br
                ni = lam_r[i] * hi[i] + lam_i[i] * hr[i] + bi
                bre_ref[s, idx, :] = nr
                bim_ref[s, idx, :] = ni
                nr_all.append(nr)
                ni_all.append(ni)
            return tuple(nr_all), tuple(ni_all)

        init = (tuple(hre_ref[s] for s in slabs), tuple(him_ref[s] for s in slabs))
        hr, hi = lax.fori_loop(0, S5_TC, step, init, unroll=2)
        for i, s in enumerate(slabs):
            hre_ref[s] = hr[i]
            him_ref[s] = hi[i]

    for gh in range(2):
        seg = S5_TC + 2 * S5_PAD * gh
        for kb in range(2):
            blk = gh * 2 + kb
            c0 = blk * S5_BLK_CH
            pieces = []
            for src in (bre_ref, bim_ref):
                for s in range(nsl):
                    slab = kb * nsl + s
                    starts = [(b * 2 + gh) * S5_PITCH - S5_PAD * gh for b in range(BATCH)]
                    pieces.append(jnp.concatenate(
                        [src[slab, r0:r0 + seg, :] for r0 in starts], axis=0))
            hcat = jnp.concatenate(pieces, axis=1).astype(BF16)
            y = jnp.dot(hcat, cw_ref[blk], preferred_element_type=F32)
            dvec = d_ref[:, c0:c0 + S5_BLK_CH]
            for b in range(BATCH):
                lo = b * seg + S5_PAD * gh
                yb = y[lo:lo + S5_TC] + dvec * u_refs[b][:, c0:c0 + S5_BLK_CH]
                z_ref[b, :, c0:c0 + S5_BLK_CH] = _gelu_tanh(yb)


def _s5_prompt(proj, w_blk, cw_blk, lamr_t, lami_t, dvec, l):
    nsteps = SEQ // S5_TC
    su_col = (2 * RET_QK + 2 * RET_W) // S5_W

    def u_spec(b):
        return pl.BlockSpec((S5_TC, S5_W), lambda j, b=b: (b * nsteps + j, su_col))

    layer4 = lambda shape: pl.BlockSpec((None,) + shape, lambda j: (l, 0, 0, 0))
    full3 = lambda shape: pl.BlockSpec(shape, lambda j: (0, 0, 0))
    st_shape = (S5_NSLAB, S5_NSEQ, LANES)
    return pl.pallas_call(
        _s5_prompt_kernel,
        grid=(nsteps,),
        in_specs=[
            u_spec(0), u_spec(1), u_spec(2), u_spec(3),
            layer4((S5_NBLK, S5_BLK_CH, 2 * S5_BLK_ST)),
            layer4((S5_NBLK, 2 * S5_BLK_ST, S5_BLK_CH)),
            layer4(st_shape), layer4(st_shape),
            _layer_row_spec(l, S5_W),
        ],
        out_specs=[
            pl.BlockSpec((BATCH, S5_TC, S5_W), lambda j: (0, j, 0)),
            full3(st_shape), full3(st_shape),
        ],
        out_shape=[
            jax.ShapeDtypeStruct((BATCH + 1, SEQ, S5_W), F32),
            jax.ShapeDtypeStruct(st_shape, F32),
            jax.ShapeDtypeStruct(st_shape, F32),
        ],
        scratch_shapes=[
            pltpu.VMEM((S5_NSLAB, S5_NSEQ * S5_PITCH, LANES), F32),
            pltpu.VMEM((S5_NSLAB, S5_NSEQ * S5_PITCH, LANES), F32),
        ],
        compiler_params=_cparams("arbitrary"),
        name="s5_prompt",
    )(proj, proj, proj, proj, w_blk, cw_blk, lamr_t, lami_t, dvec)


def _s5_sample_kernel(u_ref, h0r_ref, h0i_ref, w_ref, cw_ref, lbr_ref, lbi_ref, d_ref,
                      z_ref, hr_ref, hi_ref):
    for blk in range(S5_NBLK):
        c0 = blk * S5_BLK_CH
        s0 = blk * S5_BLK_ST
        u = u_ref[:, c0:c0 + S5_BLK_CH]
        res = jnp.dot(u.astype(BF16), w_ref[blk], preferred_element_type=F32)
        lr = lbr_ref[:, s0:s0 + S5_BLK_ST]
        li = lbi_ref[:, s0:s0 + S5_BLK_ST]
        h0r = h0r_ref[:, s0:s0 + S5_BLK_ST]
        h0i = h0i_ref[:, s0:s0 + S5_BLK_ST]
        hr = lr * h0r - li * h0i + res[:, :S5_BLK_ST]
        hi = lr * h0i + li * h0r + res[:, S5_BLK_ST:]
        hr_ref[:, s0:s0 + S5_BLK_ST] = hr
        hi_ref[:, s0:s0 + S5_BLK_ST] = hi
        hcat = jnp.concatenate([hr, hi], axis=1).astype(BF16)
        y = jnp.dot(hcat, cw_ref[blk], preferred_element_type=F32) + d_ref[:, c0:c0 + S5_BLK_CH] * u
        z_ref[:, c0:c0 + S5_BLK_CH] = _gelu_tanh(y)


def _s5_sample(proj, h0r, h0i, w_blk, cw_blk, lbr, lbi, dvec, l):
    nst = S5_GROUPS * S5_STATE
    su_col = (2 * RET_QK + 2 * RET_W) // S5_W
    layer3 = lambda shape: pl.BlockSpec((None,) + shape, lambda i: (l, 0, 0))
    layer4 = lambda shape: pl.BlockSpec((None,) + shape, lambda i: (l, 0, 0, 0))
    whole = lambda shape: pl.BlockSpec(shape, lambda i: (0, 0))
    return pl.pallas_call(
        _s5_sample_kernel,
        grid=(1,),
        in_specs=[
            pl.BlockSpec((M_S, S5_W), lambda i: (M_P // M_S, su_col)),
            layer3((M_S, nst)), layer3((M_S, nst)),
            layer4((S5_NBLK, S5_BLK_CH, 2 * S5_BLK_ST)),
            layer4((S5_NBLK, 2 * S5_BLK_ST, S5_BLK_CH)),
            layer3((1, nst)), layer3((1, nst)),
            _layer_row_spec(l, S5_W),
        ],
        out_specs=[whole((M_S, S5_W)), whole((M_S, nst)), whole((M_S, nst))],
        out_shape=[
            jax.ShapeDtypeStruct((M_S, S5_W), F32),
            jax.ShapeDtypeStruct((M_S, nst), F32),
            jax.ShapeDtypeStruct((M_S, nst), F32),
        ],
        compiler_params=_cparams("arbitrary"),
        name="s5_sample",
    )(proj, h0r, h0i, w_blk, cw_blk, lbr, lbi, dvec)


TM_MERGE = 520


def _resident_layer_spec(l, shape):
    return pl.BlockSpec((None,) + shape, lambda i: (l, 0, 0), pipeline_mode=pl.Buffered(1))


def _merge_kernel(z_ref, o_ref, gr0_ref, gr1_ref, gs0_ref, gs1_ref, gluw_ref, glub_ref, s5p_ref,
                  retp_ref, m_ref):
    z = z_ref[...]
    t = jnp.dot(z.astype(BF16), gluw_ref[...], preferred_element_type=F32) + glub_ref[...]
    z2 = (z * jax.nn.sigmoid(t)).astype(BF16)
    o = o_ref[...]
    for c, (gr_ref, gs_ref) in enumerate(((gr0_ref, gs0_ref), (gr1_ref, gs1_ref))):
        cols = slice(c * COL, (c + 1) * COL)
        b_s5 = jnp.dot(z2, s5p_ref[:, cols], preferred_element_type=F32)
        b_ret = jnp.dot(o, retp_ref[:, cols], preferred_element_type=F32)
        m_ref[:, cols] = (gr_ref[...] * b_ret + gs_ref[...] * b_s5).astype(BF16)


def _merge(z, o, proj, glu_w, glu_b, s5_proj, ret_proj, l):
    gr_col = (2 * RET_QK + 2 * RET_W + S5_W) // COL
    gs_col = gr_col + D_MODEL // COL
    gate = lambda col: pl.BlockSpec((TM_MERGE, COL), lambda i, col=col: (i, col))
    return pl.pallas_call(
        _merge_kernel,
        grid=(M_ALL // TM_MERGE,),
        in_specs=[
            pl.BlockSpec((TM_MERGE, S5_W), lambda i: (i, 0)),
            pl.BlockSpec((TM_MERGE, RET_W), lambda i: (i, 0)),
            gate(gr_col), gate(gr_col + 1), gate(gs_col), gate(gs_col + 1),
            _resident_layer_spec(l, (S5_W, S5_W)),
            _layer_row_spec(l, S5_W),
            _resident_layer_spec(l, (S5_W, D_MODEL)),
            _resident_layer_spec(l, (RET_W, D_MODEL)),
        ],
        out_specs=pl.BlockSpec((TM_MERGE, D_MODEL), lambda i: (i, 0)),
        out_shape=jax.ShapeDtypeStruct((M_ALL, D_MODEL), BF16),
        compiler_params=_cparams("parallel"),
        name="merge",
    )(z, o, proj, proj, proj, proj, glu_w, glu_b, s5_proj, ret_proj)


def _out_proj_kernel(m_ref, w_ref, x_ref, g_ref, o_ref, xn_ref):
    for c in range(D_MODEL // COL):
        cols = slice(c * COL, (c + 1) * COL)
        o_ref[:, cols] = x_ref[:, cols] + jnp.dot(m_ref[...], w_ref[:, cols],
                                                  preferred_element_type=F32)
    xn_ref[...] = _rms_to_bf16(o_ref[...], g_ref[...])


def _out_proj(m, w_out, x, g, l):
    row_spec = lambda: pl.BlockSpec((TM_MERGE, D_MODEL), lambda i: (i, 0))
    return pl.pallas_call(
        _out_proj_kernel,
        grid=(M_ALL // TM_MERGE,),
        in_specs=[row_spec(), _resident_layer_spec(l, (D_MODEL, D_MODEL)), row_spec(),
                  _layer_row_spec(l, D_MODEL)],
        out_specs=[row_spec(), row_spec()],
        out_shape=[jax.ShapeDtypeStruct((M_ALL, D_MODEL), F32),
                   jax.ShapeDtypeStruct((M_ALL, D_MODEL), BF16)],
        compiler_params=_cparams("parallel"),
        name="out_proj",
    )(m, w_out, x, g)


def _final_norm_kernel(x_ref, g_ref, o_ref):
    x = x_ref[...]
    ms = jnp.mean(x * x, axis=-1, keepdims=True)
    o_ref[...] = (x * lax.rsqrt(ms + NORM_EPS)) * g_ref[...]


def _final_norm(x, g, rows, tm, block0):
    return pl.pallas_call(
        _final_norm_kernel,
        grid=(rows // tm,),
        in_specs=[pl.BlockSpec((tm, D_MODEL), lambda i: (block0 + i, 0)),
                  pl.BlockSpec((1, D_MODEL), lambda i: (0, 0))],
        out_specs=pl.BlockSpec((tm, D_MODEL), lambda i: (i, 0)),
        out_shape=jax.ShapeDtypeStruct((rows, D_MODEL), F32),
        compiler_params=_cparams("parallel"),
        name="final_norm",
    )(x, g)


def _rope_tables():
    half = RET_DK // 2
    inv = ROPE_BASE ** (-jnp.arange(half, dtype=F32) / half)
    pos = jnp.concatenate([jnp.tile(jnp.arange(SEQ, dtype=jnp.int32), BATCH),
                           jnp.full((M_S,), PAST_LEN, jnp.int32)])
    ang = pos.astype(F32)[:, None] * inv[None, :]
    cos = jnp.cos(ang)
    sin = jnp.sin(ang)
    return jnp.concatenate([cos, cos], axis=1), jnp.concatenate([-sin, sin], axis=1)


def _retention_tables():
    c = RET_CHUNK
    log_g = jnp.log1p(-jnp.exp2(-5.0 - jnp.arange(RET_HEADS, dtype=F32)))
    i = jnp.arange(c, dtype=F32)
    diff = i[:, None] - i[None, :]
    intra = jnp.where(diff[None] >= 0,
                      jnp.exp(jnp.maximum(diff, 0.0)[None] * log_g[:, None, None]), 0.0)
    q_dec = jnp.exp((i + 1.0)[None, :] * log_g[:, None])
    k_dec = jnp.exp((c - 1.0 - i)[None, :] * log_g[:, None])
    c_dec = jnp.exp(c * log_g)
    full = lambda a: jnp.broadcast_to(a, (RET_HEADS, c, RET_DV))
    g1 = jnp.broadcast_to(jnp.exp(1.0 * log_g)[:, None, None], (RET_HEADS, 1, RET_DV))
    return (intra, full(q_dec[:, :, None]), full(k_dec[:, :, None]), full(c_dec[:, None, None])), g1


def _s5_block_weights(bbr_t, bbi_t, c_re, c_im):
    ch_group = jnp.arange(S5_BLK_CH, dtype=jnp.int32) // S5_GROUP_CH
    st_group = jnp.arange(S5_BLK_ST, dtype=jnp.int32) // S5_STATE
    drive_mask = ch_group[:, None] == st_group[None, :]

    def drive(bt):
        a = bt.reshape(DEPTH, S5_NBLK, S5_BLK_CH, 1, S5_STATE)
        a = jnp.broadcast_to(a, (DEPTH, S5_NBLK, S5_BLK_CH, S5_GB, S5_STATE))
        return jnp.where(drive_mask, a.reshape(DEPTH, S5_NBLK, S5_BLK_CH, S5_BLK_ST), 0.0)

    def readout(cm):
        a = cm.reshape(DEPTH, S5_NBLK, S5_GB, S5_GROUP_CH, S5_STATE)
        a = a.transpose(0, 1, 4, 2, 3).reshape(DEPTH, S5_NBLK, 1, S5_STATE, S5_BLK_CH)
        a = jnp.broadcast_to(a, (DEPTH, S5_NBLK, S5_GB, S5_STATE, S5_BLK_CH))
        return jnp.where(drive_mask.T, a.reshape(DEPTH, S5_NBLK, S5_BLK_ST, S5_BLK_CH), 0.0)

    w_blk = jnp.concatenate([drive(bbr_t), drive(bbi_t)], axis=3).astype(BF16)
    cw_blk = jnp.concatenate([readout(c_re), readout(-c_im)], axis=2).astype(BF16)
    return w_blk, cw_blk


def _slab_tiles(lb):
    a = lb.reshape(DEPTH, 2, S5_NSLAB, LANES).transpose(0, 2, 1, 3)
    return jnp.tile(a, (1, 1, BATCH, 1))


def kernel(x_prompt, x_sample, state_ret, state_s5_re, state_s5_im, ffn1_norm, ffn1_w1, ffn1_w3, ffn1_w2, mix_norm, w_in, ret_gn, ret_proj, s5_lam_re, s5_lam_im, s5_b_re, s5_b_im, s5_c_re, s5_c_im, s5_d, s5_log_step, glu_w, glu_b, s5_proj, w_out, ffn2_norm, ffn2_w1, ffn2_w3, ffn2_w2, final_norm):
    cos_t, sin_t = _rope_tables()
    ret_tabs, g1 = _retention_tables()
    x = jnp.concatenate([x_prompt.reshape(M_P, D_MODEL), x_sample.reshape(M_S, D_MODEL)], axis=0)

    ret_p, s5r_p, s5i_p, s5r_s, s5i_s = [], [], [], [], []
    ret_s = None
    nst = S5_GROUPS * S5_STATE
    rows3 = lambda a: a.reshape(DEPTH, 1, -1)
    ffn1_g, mix_g, ffn2_g, glu_b3 = rows3(ffn1_norm), rows3(mix_norm), rows3(ffn2_norm), rows3(glu_b)
    gn4 = ret_gn.reshape(DEPTH, RET_HEADS, 1, RET_DV)
    glu_w16, s5_proj16, ret_proj16, w_out16 = (a.astype(BF16) for a in (glu_w, s5_proj, ret_proj, w_out))

    lbr_x, lbi_x, bbr_t, bbi_t = _s5_discretize(s5_lam_re, s5_lam_im, s5_log_step, s5_b_re, s5_b_im)
    lbr = lbr_x[::S5_GROUP_CH].reshape(DEPTH, nst)
    lbi = lbi_x[::S5_GROUP_CH].reshape(DEPTH, nst)
    w_blk, cw_blk = _s5_block_weights(bbr_t, bbi_t, s5_c_re, s5_c_im)
    lamr_t, lami_t = _slab_tiles(lbr), _slab_tiles(lbi)
    lbr3, lbi3 = lbr.reshape(DEPTH, 1, nst), lbi.reshape(DEPTH, 1, nst)
    dvec3 = s5_d.reshape(DEPTH, 1, S5_W)
    h0r_all = state_s5_re.reshape(DEPTH, M_S, nst)
    h0i_all = state_s5_im.reshape(DEPTH, M_S, nst)

    for l in range(DEPTH):
        hid = _ffn_up(_norm(x, ffn1_g, l), ffn1_w1, ffn1_w3, l)
        x = _mm_res(hid, ffn1_w2, x, l, scale=0.5, tm=TM_DOWN, tn=TN_DOWN, name="ffn_down")

        proj = _inproj(_norm(x, mix_g, l), w_in, cos_t, sin_t, l)

        o_all, s_p = _ret_prompt(proj, ret_tabs, gn4, l)
        o_s, ret_s = _ret_sample(proj, state_ret, g1, gn4, l, ret_s)
        o_all = lax.dynamic_update_slice(o_all, o_s.astype(BF16), (M_P, 0))

        z5, hre_p, him_p = _s5_prompt(proj, w_blk, cw_blk, lamr_t, lami_t, dvec3, l)
        z_s, hre_s, him_s = _s5_sample(proj, h0r_all, h0i_all, w_blk, cw_blk, lbr3, lbi3, dvec3, l)
        z = lax.dynamic_update_slice(z5.reshape((BATCH + 1) * SEQ, S5_W), z_s, (M_P, 0))

        m = _merge(z, o_all, proj, glu_w16, glu_b3, s5_proj16, ret_proj16, l)
        x, xn2 = _out_proj(m, w_out16, x, ffn2_g, l)

        hid = _ffn_up(xn2, ffn2_w1, ffn2_w3, l)
        x = _mm_res(hid, ffn2_w2, x, l, scale=0.5, tm=TM_DOWN, tn=TN_DOWN, name="ffn_down")

        unslab = lambda a: (a.transpose(1, 0, 2).reshape(BATCH, 2, S5_HALF_ST)
                            .reshape(BATCH, S5_GROUPS, S5_STATE))
        ret_p.append(s_p)
        s5r_p.append(unslab(hre_p))
        s5i_p.append(unslab(him_p))
        s5r_s.append(hre_s.reshape(M_S, S5_GROUPS, S5_STATE))
        s5i_s.append(him_s.reshape(M_S, S5_GROUPS, S5_STATE))

    g_fin = final_norm.reshape(1, D_MODEL)
    y_p = _final_norm(x, g_fin, M_P, 1024, 0)
    y_s = _final_norm(x, g_fin, M_S, M_S, M_P // M_S)
    return (y_p.reshape(BATCH, SEQ, D_MODEL), y_s.reshape(M_S, 1, D_MODEL),
            jnp.stack(ret_p), jnp.stack(s5r_p), jnp.stack(s5i_p),
            ret_s, jnp.stack(s5r_s), jnp.stack(s5i_s))
```

```python
import functools
import math

import jax
import jax.numpy as jnp
from jax import lax
from jax.experimental import pallas as pl
from jax.experimental.pallas import tpu as pltpu

F32 = jnp.float32
BF16 = jnp.bfloat16

D_MODEL = 2048
BATCH = 4
SEQ = 2048
DEPTH = 4
DEC_BATCH = 128
PAST_LEN = 16384
RET_HEADS = 8
RET_DK = 128
RET_DV = 128
RET_QK = RET_HEADS * RET_DK
RET_W = RET_HEADS * RET_DV
RET_CHUNK = 128
ROPE_BASE = 10000.0
S5_GROUPS = 64
S5_GROUP_CH = 16
S5_W = S5_GROUPS * S5_GROUP_CH
S5_STATE = 64
D_FF = 5632
IN_W = 2 * RET_QK + 2 * RET_W + S5_W + 2 * D_MODEL
NORM_EPS = 1e-6
HEAD_NORM_EPS = 1e-5

M_P = BATCH * SEQ
M_S = DEC_BATCH
M_ALL = M_P + M_S

LANES = 128
SUBLANES = 8
VMEM_LIMIT = 58 * 1024 * 1024

TM = 1040
TN_FF = 512
TM_DOWN = 416
COL = 1024
TM_BIG = 4 * TM
TN_IN = 512
IN_SUB = 256

S5_TC = 64
S5_PAD = SUBLANES // 2
S5_PITCH = S5_TC + S5_PAD
S5_GB = 16
S5_NBLK = S5_GROUPS // S5_GB
S5_BLK_CH = S5_GB * S5_GROUP_CH
S5_BLK_ST = S5_GB * S5_STATE
S5_HALF_ST = S5_GROUPS * S5_STATE // 2
S5_NSLAB = S5_HALF_ST // LANES
S5_NSEQ = BATCH * 2
S5_SLAB_GROUP = 4


def _cparams(*sem):
    return pltpu.CompilerParams(dimension_semantics=sem, vmem_limit_bytes=VMEM_LIMIT)


def _rms_to_bf16(x, g):
    ms = jnp.mean(x * x, axis=-1, keepdims=True)
    return ((x * lax.rsqrt(ms + NORM_EPS)) * g).astype(BF16)


def _layer_spec(l, k, tn):
    return pl.BlockSpec((None, k, tn), lambda i, j: (l, 0, j))


def _layer_row_spec(l, n):
    return pl.BlockSpec((None, 1, n), lambda *_: (l, 0, 0))


def _norm_kernel(x_ref, g_ref, o_ref):
    o_ref[...] = _rms_to_bf16(x_ref[...], g_ref[...])


def _norm(x, g, l):
    return pl.pallas_call(
        _norm_kernel,
        grid=(M_ALL // TM,),
        in_specs=[pl.BlockSpec((TM, D_MODEL), lambda i: (i, 0)), _layer_row_spec(l, D_MODEL)],
        out_specs=pl.BlockSpec((TM, D_MODEL), lambda i: (i, 0)),
        out_shape=jax.ShapeDtypeStruct((M_ALL, D_MODEL), BF16),
        compiler_params=_cparams("parallel"),
        name="norm",
    )(x, g)


def _resident_rows_spec():
    return pl.BlockSpec((TM_BIG, D_MODEL), lambda r, j: (r, 0), pipeline_mode=pl.Buffered(1))


def _ffn_up_kernel(xn_ref, w1_ref, w3_ref, o_ref):
    w1 = w1_ref[...].astype(BF16)
    w3 = w3_ref[...].astype(BF16)
    for s in range(TM_BIG // TM):
        rows = slice(s * TM, (s + 1) * TM)
        xs = xn_ref[rows, :]
        h1 = jnp.dot(xs, w1, preferred_element_type=F32)
        h3 = jnp.dot(xs, w3, preferred_element_type=F32)
        o_ref[rows, :] = ((h1 * jax.nn.sigmoid(h1)) * h3).astype(BF16)


def _ffn_up(xn, w1, w3, l):
    return pl.pallas_call(
        _ffn_up_kernel,
        grid=(M_ALL // TM_BIG, D_FF // TN_FF),
        in_specs=[
            _resident_rows_spec(),
            _layer_spec(l, D_MODEL, TN_FF),
            _layer_spec(l, D_MODEL, TN_FF),
        ],
        out_specs=pl.BlockSpec((TM_BIG, TN_FF), lambda r, j: (r, j)),
        out_shape=jax.ShapeDtypeStruct((M_ALL, D_FF), BF16),
        compiler_params=_cparams("parallel", "arbitrary"),
        name="ffn_up",
    )(xn, w1, w3)


def _resident_layer_spec(l, shape):
    return pl.BlockSpec((None,) + shape, lambda i: (l, 0, 0), pipeline_mode=pl.Buffered(1))


def _proj_res_kernel(a_ref, w_ref, x_ref, *rest, scale):
    o_ref = rest[-2] if len(rest) == 3 else rest[0]
    for c in range(D_MODEL // COL):
        cols = slice(c * COL, (c + 1) * COL)
        acc = jnp.dot(a_ref[...], w_ref[:, cols], preferred_element_type=F32)
        o_ref[:, cols] = x_ref[:, cols] + scale * acc
    if len(rest) == 3:
        rest[2][...] = _rms_to_bf16(o_ref[...], rest[0][...])


def _proj_res(a, w16, x, l, *, scale, tm, name, gain=None, gain_layer=None):
    k = a.shape[1]
    row_spec = lambda: pl.BlockSpec((tm, D_MODEL), lambda i: (i, 0))
    in_specs = [pl.BlockSpec((tm, k), lambda i: (i, 0)), _resident_layer_spec(l, (k, D_MODEL)),
                row_spec()]
    args = [a, w16, x]
    out_specs = [row_spec()]
    out_shape = [jax.ShapeDtypeStruct((M_ALL, D_MODEL), F32)]
    if gain is not None:
        in_specs.append(_layer_row_spec(gain_layer, D_MODEL))
        args.append(gain)
        out_specs.append(row_spec())
        out_shape.append(jax.ShapeDtypeStruct((M_ALL, D_MODEL), BF16))
    return pl.pallas_call(
        functools.partial(_proj_res_kernel, scale=scale),
        grid=(M_ALL // tm,),
        in_specs=in_specs,
        out_specs=out_specs,
        out_shape=out_shape,
        compiler_params=_cparams("parallel"),
        name=name,
    )(*args)


def _inproj_kernel(xn_ref, w_ref, cos_ref, sin_ref, o_ref):
    seg = pl.program_id(1) // (COL // TN_IN)

    def project(epilogue):
        wb = [w_ref[:, c * IN_SUB:(c + 1) * IN_SUB].astype(BF16) for c in range(TN_IN // IN_SUB)]
        for s in range(TM_BIG // TM):
            rows = slice(s * TM, (s + 1) * TM)
            for c in range(TN_IN // IN_SUB):
                acc = jnp.dot(xn_ref[rows, :], wb[c], preferred_element_type=F32)
                o_ref[rows, c * IN_SUB:(c + 1) * IN_SUB] = epilogue(acc, rows)

    def rotary(acc, rows):
        scale = jnp.where(seg == 0, RET_DK ** -0.5, 1.0).astype(F32)
        cos = cos_ref[rows, :]
        sin = sin_ref[rows, :]
        heads = []
        for h in range(IN_SUB // RET_DK):
            a = acc[:, h * RET_DK:(h + 1) * RET_DK]
            heads.append((a * cos + pltpu.roll(a, RET_DK // 2, axis=1) * sin) * scale)
        return jnp.concatenate(heads, axis=1)

    @pl.when(seg <= 1)
    def _():
        project(rotary)

    @pl.when((seg == 2) | (seg == 4))
    def _():
        project(lambda acc, rows: acc)

    @pl.when(seg == 3)
    def _():
        project(lambda acc, rows: acc * jax.nn.sigmoid(acc))

    @pl.when(seg >= 5)
    def _():
        project(lambda acc, rows: jax.nn.sigmoid(acc))


def _inproj(xn, w, cos_t, sin_t, l):
    tab_spec = pl.BlockSpec((TM_BIG, RET_DK), lambda r, j: (r, 0), pipeline_mode=pl.Buffered(1))
    return pl.pallas_call(
        _inproj_kernel,
        grid=(M_ALL // TM_BIG, IN_W // TN_IN),
        in_specs=[
            _resident_rows_spec(),
            _layer_spec(l, D_MODEL, TN_IN),
            tab_spec, tab_spec,
        ],
        out_specs=pl.BlockSpec((TM_BIG, TN_IN), lambda r, j: (r, j)),
        out_shape=jax.ShapeDtypeStruct((M_ALL, IN_W), F32),
        compiler_params=_cparams("parallel", "arbitrary"),
        name="inproj",
    )(xn, w, cos_t, sin_t)


def _head_norm(o):
    mu = jnp.mean(o, axis=-1, keepdims=True)
    d = o - mu
    var = jnp.mean(d * d, axis=-1, keepdims=True)
    return d * lax.rsqrt(var + HEAD_NORM_EPS)


RET_RC = 4 * RET_CHUNK


def _ret_prompt_kernel(q_ref, k_ref, v_ref, rg_ref, intra_ref, qd_ref, kd_ref, cd_ref, gn_ref,
                       o_ref, s_ref, st_ref):
    step = pl.program_id(1)

    @pl.when(step == 0)
    def _():
        st_ref[...] = jnp.zeros_like(st_ref)

    for c in range(RET_RC // RET_CHUNK):
        rows = slice(c * RET_CHUNK, (c + 1) * RET_CHUNK)
        for h in range(RET_HEADS):
            cols = slice(h * RET_DK, (h + 1) * RET_DK)
            qc = q_ref[rows, cols]
            kc = k_ref[rows, cols]
            vb = v_ref[rows, cols].astype(BF16)
            s = st_ref[h]
            sc = lax.dot_general(qc.astype(BF16), kc.astype(BF16), (((1,), (1,)), ((), ())),
                                 preferred_element_type=F32) * intra_ref[h]
            o = (jnp.dot(sc.astype(BF16), vb, preferred_element_type=F32)
                 + jnp.dot((qc * qd_ref[h]).astype(BF16), s.astype(BF16),
                           preferred_element_type=F32))
            kt = jnp.transpose(kc * kd_ref[h]).astype(BF16)
            st_ref[h] = s * cd_ref[h] + jnp.dot(kt, vb, preferred_element_type=F32)
            on = (_head_norm(o) * gn_ref[h]) * rg_ref[rows, cols]
            o_ref[rows, cols] = on.astype(BF16)

    @pl.when(step == pl.num_programs(1) - 1)
    def _():
        s_ref[...] = st_ref[...]


def _ret_prompt(proj, tabs, gn, l):
    intra, qd, kd, cd = tabs
    nsteps = SEQ // RET_RC
    tab_spec = pl.BlockSpec((RET_HEADS, RET_CHUNK, RET_CHUNK), lambda b, s: (0, 0, 0))

    def seg_spec(seg):
        return pl.BlockSpec((RET_RC, COL), lambda b, s, seg=seg: (b * nsteps + s, seg))

    return pl.pallas_call(
        _ret_prompt_kernel,
        grid=(BATCH, nsteps),
        in_specs=[
            seg_spec(0), seg_spec(1), seg_spec(2), seg_spec(3),
            tab_spec, tab_spec, tab_spec, tab_spec,
            pl.BlockSpec((None, RET_HEADS, 1, RET_DV), lambda b, s: (l, 0, 0, 0)),
        ],
        out_specs=[
            pl.BlockSpec((RET_RC, RET_W), lambda b, s: (b * nsteps + s, 0)),
            pl.BlockSpec((None, RET_HEADS, RET_DK, RET_DV), lambda b, s: (b, 0, 0, 0)),
        ],
        out_shape=[
            jax.ShapeDtypeStruct((M_ALL, RET_W), BF16),
            jax.ShapeDtypeStruct((BATCH, RET_HEADS, RET_DK, RET_DV), F32),
        ],
        scratch_shapes=[pltpu.VMEM((RET_HEADS, RET_DK, RET_DV), F32)],
        compiler_params=_cparams("parallel", "arbitrary"),
        name="ret_prompt",
    )(proj, proj, proj, proj, intra, qd, kd, cd, gn)


RET_SB = 8


def _col_bcast(row, eye, ones3):
    hi = row.astype(BF16)
    r1 = row - hi.astype(F32)
    mid = r1.astype(BF16)
    lo = (r1 - mid.astype(F32)).astype(BF16)
    parts = [jnp.where(eye, jnp.broadcast_to(t.astype(F32), (RET_DK, RET_DK)), 0.0).astype(BF16)
             for t in (hi, mid, lo)]
    return jnp.dot(jnp.concatenate(parts, axis=1), ones3, preferred_element_type=F32)


def _ret_sample_kernel(q_ref, k_ref, v_ref, rg_ref, s_ref, g1_ref, gn_ref, *rest):
    o_ref, so_ref = rest[-2:]
    eye = (lax.broadcasted_iota(jnp.int32, (RET_DK, RET_DK), 0)
           == lax.broadcasted_iota(jnp.int32, (RET_DK, RET_DK), 1))
    ones3 = jnp.ones((3 * RET_DK, RET_DV), BF16)

    for b in range(RET_SB):
        row = slice(b, b + 1)
        for h in range(RET_HEADS):
            cols = slice(h * RET_DK, (h + 1) * RET_DK)
            kcol = _col_bcast(k_ref[row, cols], eye, ones3)
            qcol = _col_bcast(q_ref[row, cols], eye, ones3)
            s_new = s_ref[b, h] * g1_ref[h] + kcol * v_ref[row, cols]
            so_ref[b, h] = s_new
            o = jnp.sum(qcol * s_new, axis=0, keepdims=True)
            o_ref[row, cols] = (_head_norm(o) * gn_ref[h]) * rg_ref[row, cols]


def _ret_sample(proj, state, g1, gn, l, prev_states):
    row0 = M_P // RET_SB

    def seg_spec(seg):
        return pl.BlockSpec((RET_SB, COL), lambda i, seg=seg: (row0 + i, seg))

    st_spec = pl.BlockSpec((None, RET_SB, RET_HEADS, RET_DK, RET_DV), lambda i: (l, i, 0, 0, 0))
    in_specs = [
        seg_spec(0), seg_spec(1), seg_spec(2), seg_spec(3), st_spec,
        pl.BlockSpec((RET_HEADS, 1, RET_DV), lambda i: (0, 0, 0)),
        pl.BlockSpec((None, RET_HEADS, 1, RET_DV), lambda i: (l, 0, 0, 0)),
    ]
    args = [proj, proj, proj, proj, state, g1, gn]
    aliases = {}
    if prev_states is not None:
        in_specs.append(pl.BlockSpec(memory_space=pl.ANY))
        args.append(prev_states)
        aliases = {len(args) - 1: 1}
    return pl.pallas_call(
        _ret_sample_kernel,
        grid=(M_S // RET_SB,),
        in_specs=in_specs,
        out_specs=[pl.BlockSpec((RET_SB, RET_W), lambda i: (i, 0)), st_spec],
        out_shape=[
            jax.ShapeDtypeStruct((M_S, RET_W), F32),
            jax.ShapeDtypeStruct((DEPTH, M_S, RET_HEADS, RET_DK, RET_DV), F32),
        ],
        input_output_aliases=aliases,
        compiler_params=_cparams("parallel"),
        name="ret_sample",
    )(*args)


def _s5_disc_kernel(lr_ref, li_ref, ls_ref, br_ref, bi_ref, lbr_ref, lbi_ref, bbr_ref, bbi_ref):
    lam_r = lr_ref[...]
    lam_i = li_ref[...]
    dt = jnp.exp(ls_ref[...])
    ea = jnp.exp(lam_r * dt)
    lb_r = ea * jnp.cos(lam_i * dt)
    lb_i = ea * jnp.sin(lam_i * dt)
    lbr_ref[...] = lb_r
    lbi_ref[...] = lb_i
    nr = lb_r - 1.0
    den = lam_r * lam_r + lam_i * lam_i
    cr = (nr * lam_r + lb_i * lam_i) / den
    ci = (lb_i * lam_r - nr * lam_i) / den
    b_r = br_ref[...]
    b_i = bi_ref[...]
    bbr_ref[...] = cr * b_r - ci * b_i
    bbi_ref[...] = cr * b_i + ci * b_r


def _s5_discretize(lam_re, lam_im, log_step, b_re, b_im):
    rows = DEPTH * S5_GROUPS * S5_GROUP_CH
    rep = lambda a: jnp.repeat(a.reshape(DEPTH * S5_GROUPS, S5_STATE), S5_GROUP_CH, axis=0)
    tr = lambda a: jnp.transpose(a, (0, 1, 3, 2)).reshape(rows, S5_STATE)
    shp = jax.ShapeDtypeStruct((rows, S5_STATE), F32)
    ls = jnp.broadcast_to(log_step[:, :, None], (DEPTH, S5_GROUPS, S5_STATE))
    spec = pl.BlockSpec((rows // DEPTH, S5_STATE), lambda l: (l, 0))
    return pl.pallas_call(
        _s5_disc_kernel,
        grid=(DEPTH,),
        in_specs=[spec] * 5,
        out_specs=[spec] * 4,
        out_shape=[shp, shp, shp, shp],
        compiler_params=_cparams("parallel"),
        name="s5_disc",
    )(rep(lam_re), rep(lam_im), rep(ls), tr(b_re), tr(b_im))


def _gelu_tanh(y):
    c = math.sqrt(2.0 / math.pi)
    return 0.5 * y * (1.0 + jnp.tanh(c * (y + 0.044715 * (y * y * y))))


def _s5_prompt_kernel(u0_ref, u1_ref, u2_ref, u3_ref, w_ref, cw_ref, lamr_ref, lami_ref, d_ref,
                      z_ref, hre_ref, him_ref, bre_ref, bim_ref):
    j = pl.program_id(0)
    u_refs = (u0_ref, u1_ref, u2_ref, u3_ref)

    @pl.when(j == 0)
    def _():
        hre_ref[...] = jnp.zeros_like(hre_ref)
        him_ref[...] = jnp.zeros_like(him_ref)

    nsl = S5_BLK_ST // LANES
    zpad = jnp.zeros((S5_PAD, S5_BLK_CH), F32)

    for gh in range(2):
        seg = S5_TC + 2 * S5_PAD * gh
        for kb in range(2):
            blk = gh * 2 + kb
            c0 = blk * S5_BLK_CH
            parts = []
            for u in u_refs:
                ub = u[:, c0:c0 + S5_BLK_CH]
                parts.extend([zpad, ub, zpad] if gh else [ub])
            x = jnp.concatenate(parts, axis=0).astype(BF16)
            res = jnp.dot(x, w_ref[blk], preferred_element_type=F32)
            for b in range(BATCH):
                r0 = (b * 2 + gh) * S5_PITCH - S5_PAD * gh
                part = res[b * seg:(b + 1) * seg]
                for s in range(nsl):
                    slab = kb * nsl + s
                    bre_ref[slab, r0:r0 + seg, :] = part[:, s * LANES:(s + 1) * LANES]
                    bim_ref[slab, r0:r0 + seg, :] = part[:, S5_BLK_ST + s * LANES:S5_BLK_ST + (s + 1) * LANES]

    for sg in range(S5_NSLAB // S5_SLAB_GROUP):
        slabs = [sg * S5_SLAB_GROUP + i for i in range(S5_SLAB_GROUP)]
        lam_r = [lamr_ref[s] for s in slabs]
        lam_i = [lami_ref[s] for s in slabs]

        def step(t, carry, slabs=slabs, lam_r=lam_r, lam_i=lam_i):
            hr, hi = carry
            nr_all = []
            ni_all = []
            for i, s in enumerate(slabs):
                idx = pl.ds(t, S5_NSEQ, stride=S5_PITCH)
                br = bre_ref[s, idx, :]
                bi = bim_ref[s, idx, :]
                nr = lam_r[i] * hr[i] - lam_i[i] * hi[i] + br
                ni = lam_r[i] * hi[i] + lam_i[i] * hr[i] + bi
                bre_ref[s, idx, :] = nr
                bim_ref[s, idx, :] = ni
                nr_all.append(nr)
                ni_all.append(ni)
            return tuple(nr_all), tuple(ni_all)

        init = (tuple(hre_ref[s] for s in slabs), tuple(him_ref[s] for s in slabs))
        hr, hi = lax.fori_loop(0, S5_TC, step, init, unroll=2)
        for i, s in enumerate(slabs):
            hre_ref[s] = hr[i]
            him_ref[s] = hi[i]

    for gh in range(2):
        seg = S5_TC + 2 * S5_PAD * gh
        for kb in range(2):
            blk = gh * 2 + kb
            c0 = blk * S5_BLK_CH
            pieces = []
            for src in (bre_ref, bim_ref):
                for s in range(nsl):
                    slab = kb * nsl + s
                    starts = [(b * 2 + gh) * S5_PITCH - S5_PAD * gh for b in range(BATCH)]
                    pieces.append(jnp.concatenate(
                        [src[slab, r0:r0 + seg, :] for r0 in starts], axis=0))
            hcat = jnp.concatenate(pieces, axis=1).astype(BF16)
            y = jnp.dot(hcat, cw_ref[blk], preferred_element_type=F32)
            dvec = d_ref[:, c0:c0 + S5_BLK_CH]
            for b in range(BATCH):
                lo = b * seg + S5_PAD * gh
                yb = y[lo:lo + S5_TC] + dvec * u_refs[b][:, c0:c0 + S5_BLK_CH]
                z_ref[b, :, c0:c0 + S5_BLK_CH] = _gelu_tanh(yb)


def _s5_prompt(proj, w_blk, cw_blk, lamr_t, lami_t, dvec, l):
    nsteps = SEQ // S5_TC
    su_col = (2 * RET_QK + 2 * RET_W) // S5_W

    def u_spec(b):
        return pl.BlockSpec((S5_TC, S5_W), lambda j, b=b: (b * nsteps + j, su_col))

    layer4 = lambda shape: pl.BlockSpec((None,) + shape, lambda j: (l, 0, 0, 0))
    full3 = lambda shape: pl.BlockSpec(shape, lambda j: (0, 0, 0))
    st_shape = (S5_NSLAB, S5_NSEQ, LANES)
    return pl.pallas_call(
        _s5_prompt_kernel,
        grid=(nsteps,),
        in_specs=[
            u_spec(0), u_spec(1), u_spec(2), u_spec(3),
            layer4((S5_NBLK, S5_BLK_CH, 2 * S5_BLK_ST)),
            layer4((S5_NBLK, 2 * S5_BLK_ST, S5_BLK_CH)),
            layer4(st_shape), layer4(st_shape),
            _layer_row_spec(l, S5_W),
        ],
        out_specs=[
            pl.BlockSpec((BATCH, S5_TC, S5_W), lambda j: (0, j, 0)),
            full3(st_shape), full3(st_shape),
        ],
        out_shape=[
            jax.ShapeDtypeStruct((BATCH + 1, SEQ, S5_W), F32),
            jax.ShapeDtypeStruct(st_shape, F32),
            jax.ShapeDtypeStruct(st_shape, F32),
        ],
        scratch_shapes=[
            pltpu.VMEM((S5_NSLAB, S5_NSEQ * S5_PITCH, LANES), F32),
            pltpu.VMEM((S5_NSLAB, S5_NSEQ * S5_PITCH, LANES), F32),
        ],
        compiler_params=_cparams("arbitrary"),
        name="s5_prompt",
    )(proj, proj, proj, proj, w_blk, cw_blk, lamr_t, lami_t, dvec)


def _s5_sample_kernel(u_ref, h0r_ref, h0i_ref, w_ref, cw_ref, lbr_ref, lbi_ref, d_ref,
                      z_ref, hr_ref, hi_ref):
    for blk in range(S5_NBLK):
        c0 = blk * S5_BLK_CH
        s0 = blk * S5_BLK_ST
        u = u_ref[:, c0:c0 + S5_BLK_CH]
        res = jnp.dot(u.astype(BF16), w_ref[blk], preferred_element_type=F32)
        lr = lbr_ref[:, s0:s0 + S5_BLK_ST]
        li = lbi_ref[:, s0:s0 + S5_BLK_ST]
        h0r = h0r_ref[:, s0:s0 + S5_BLK_ST]
        h0i = h0i_ref[:, s0:s0 + S5_BLK_ST]
        hr = lr * h0r - li * h0i + res[:, :S5_BLK_ST]
        hi = lr * h0i + li * h0r + res[:, S5_BLK_ST:]
        hr_ref[:, s0:s0 + S5_BLK_ST] = hr
        hi_ref[:, s0:s0 + S5_BLK_ST] = hi
        hcat = jnp.concatenate([hr, hi], axis=1).astype(BF16)
        y = jnp.dot(hcat, cw_ref[blk], preferred_element_type=F32) + d_ref[:, c0:c0 + S5_BLK_CH] * u
        z_ref[:, c0:c0 + S5_BLK_CH] = _gelu_tanh(y)


def _s5_sample(proj, h0r, h0i, w_blk, cw_blk, lbr, lbi, dvec, l):
    nst = S5_GROUPS * S5_STATE
    su_col = (2 * RET_QK + 2 * RET_W) // S5_W
    layer3 = lambda shape: pl.BlockSpec((None,) + shape, lambda i: (l, 0, 0))
    layer4 = lambda shape: pl.BlockSpec((None,) + shape, lambda i: (l, 0, 0, 0))
    whole = lambda shape: pl.BlockSpec(shape, lambda i: (0, 0))
    return pl.pallas_call(
        _s5_sample_kernel,
        grid=(1,),
        in_specs=[
            pl.BlockSpec((M_S, S5_W), lambda i: (M_P // M_S, su_col)),
            layer3((M_S, nst)), layer3((M_S, nst)),
            layer4((S5_NBLK, S5_BLK_CH, 2 * S5_BLK_ST)),
            layer4((S5_NBLK, 2 * S5_BLK_ST, S5_BLK_CH)),
            layer3((1, nst)), layer3((1, nst)),
            _layer_row_spec(l, S5_W),
        ],
        out_specs=[whole((M_S, S5_W)), whole((M_S, nst)), whole((M_S, nst))],
        out_shape=[
            jax.ShapeDtypeStruct((M_S, S5_W), F32),
            jax.ShapeDtypeStruct((M_S, nst), F32),
            jax.ShapeDtypeStruct((M_S, nst), F32),
        ],
        compiler_params=_cparams("arbitrary"),
        name="s5_sample",
    )(proj, h0r, h0i, w_blk, cw_blk, lbr, lbi, dvec)


TM_MERGE = 520


def _merge_kernel(z_ref, o_ref, gr0_ref, gr1_ref, gs0_ref, gs1_ref, gluw_ref, glub_ref, s5p_ref,
                  retp_ref, m_ref):
    z = z_ref[...]
    t = jnp.dot(z.astype(BF16), gluw_ref[...], preferred_element_type=F32) + glub_ref[...]
    z2 = (z * jax.nn.sigmoid(t)).astype(BF16)
    o = o_ref[...]
    for c, (gr_ref, gs_ref) in enumerate(((gr0_ref, gs0_ref), (gr1_ref, gs1_ref))):
        cols = slice(c * COL, (c + 1) * COL)
        b_s5 = jnp.dot(z2, s5p_ref[:, cols], preferred_element_type=F32)
        b_ret = jnp.dot(o, retp_ref[:, cols], preferred_element_type=F32)
        m_ref[:, cols] = (gr_ref[...] * b_ret + gs_ref[...] * b_s5).astype(BF16)


def _merge(z, o, proj, glu_w, glu_b, s5_proj, ret_proj, l):
    gr_col = (2 * RET_QK + 2 * RET_W + S5_W) // COL
    gs_col = gr_col + D_MODEL // COL
    gate = lambda col: pl.BlockSpec((TM_MERGE, COL), lambda i, col=col: (i, col))
    return pl.pallas_call(
        _merge_kernel,
        grid=(M_ALL // TM_MERGE,),
        in_specs=[
            pl.BlockSpec((TM_MERGE, S5_W), lambda i: (i, 0)),
            pl.BlockSpec((TM_MERGE, RET_W), lambda i: (i, 0)),
            gate(gr_col), gate(gr_col + 1), gate(gs_col), gate(gs_col + 1),
            _resident_layer_spec(l, (S5_W, S5_W)),
            _layer_row_spec(l, S5_W),
            _resident_layer_spec(l, (S5_W, D_MODEL)),
            _resident_layer_spec(l, (RET_W, D_MODEL)),
        ],
        out_specs=pl.BlockSpec((TM_MERGE, D_MODEL), lambda i: (i, 0)),
        out_shape=jax.ShapeDtypeStruct((M_ALL, D_MODEL), BF16),
        compiler_params=_cparams("parallel"),
        name="merge",
    )(z, o, proj, proj, proj, proj, glu_w, glu_b, s5_proj, ret_proj)


def _final_norm_kernel(x_ref, g_ref, o_ref):
    x = x_ref[...]
    ms = jnp.mean(x * x, axis=-1, keepdims=True)
    o_ref[...] = (x * lax.rsqrt(ms + NORM_EPS)) * g_ref[...]


def _final_norm(x, g, rows, tm, block0):
    return pl.pallas_call(
        _final_norm_kernel,
        grid=(rows // tm,),
        in_specs=[pl.BlockSpec((tm, D_MODEL), lambda i: (block0 + i, 0)),
                  pl.BlockSpec((1, D_MODEL), lambda i: (0, 0))],
        out_specs=pl.BlockSpec((tm, D_MODEL), lambda i: (i, 0)),
        out_shape=jax.ShapeDtypeStruct((rows, D_MODEL), F32),
        compiler_params=_cparams("parallel"),
        name="final_norm",
    )(x, g)


def _rope_tables():
    half = RET_DK // 2
    inv = ROPE_BASE ** (-jnp.arange(half, dtype=F32) / half)
    pos = jnp.concatenate([jnp.tile(jnp.arange(SEQ, dtype=jnp.int32), BATCH),
                           jnp.full((M_S,), PAST_LEN, jnp.int32)])
    ang = pos.astype(F32)[:, None] * inv[None, :]
    cos = jnp.cos(ang)
    sin = jnp.sin(ang)
    return jnp.concatenate([cos, cos], axis=1), jnp.concatenate([-sin, sin], axis=1)


def _retention_tables():
    c = RET_CHUNK
    log_g = jnp.log1p(-jnp.exp2(-5.0 - jnp.arange(RET_HEADS, dtype=F32)))
    i = jnp.arange(c, dtype=F32)
    diff = i[:, None] - i[None, :]
    intra = jnp.where(diff[None] >= 0,
                      jnp.exp(jnp.maximum(diff, 0.0)[None] * log_g[:, None, None]), 0.0)
    q_dec = jnp.exp((i + 1.0)[None, :] * log_g[:, None])
    k_dec = jnp.exp((c - 1.0 - i)[None, :] * log_g[:, None])
    c_dec = jnp.exp(c * log_g)
    full = lambda a: jnp.broadcast_to(a, (RET_HEADS, c, RET_DV))
    g1 = jnp.broadcast_to(jnp.exp(1.0 * log_g)[:, None, None], (RET_HEADS, 1, RET_DV))
    return (intra, full(q_dec[:, :, None]), full(k_dec[:, :, None]), full(c_dec[:, None, None])), g1


def _s5_block_weights(bbr_t, bbi_t, c_re, c_im):
    ch_group = jnp.arange(S5_BLK_CH, dtype=jnp.int32) // S5_GROUP_CH
    st_group = jnp.arange(S5_BLK_ST, dtype=jnp.int32) // S5_STATE
    drive_mask = ch_group[:, None] == st_group[None, :]

    def drive(bt):
        a = bt.reshape(DEPTH, S5_NBLK, S5_BLK_CH, 1, S5_STATE)
        a = jnp.broadcast_to(a, (DEPTH, S5_NBLK, S5_BLK_CH, S5_GB, S5_STATE))
        return jnp.where(drive_mask, a.reshape(DEPTH, S5_NBLK, S5_BLK_CH, S5_BLK_ST), 0.0)

    def readout(cm):
        a = cm.reshape(DEPTH, S5_NBLK, S5_GB, S5_GROUP_CH, S5_STATE)
        a = a.transpose(0, 1, 4, 2, 3).reshape(DEPTH, S5_NBLK, 1, S5_STATE, S5_BLK_CH)
        a = jnp.broadcast_to(a, (DEPTH, S5_NBLK, S5_GB, S5_STATE, S5_BLK_CH))
        return jnp.where(drive_mask.T, a.reshape(DEPTH, S5_NBLK, S5_BLK_ST, S5_BLK_CH), 0.0)

    w_blk = jnp.concatenate([drive(bbr_t), drive(bbi_t)], axis=3).astype(BF16)
    cw_blk = jnp.concatenate([readout(c_re), readout(-c_im)], axis=2).astype(BF16)
    return w_blk, cw_blk


def _slab_tiles(lb):
    a = lb.reshape(DEPTH, 2, S5_NSLAB, LANES).transpose(0, 2, 1, 3)
    return jnp.tile(a, (1, 1, BATCH, 1))


def kernel(x_prompt, x_sample, state_ret, state_s5_re, state_s5_im, ffn1_norm, ffn1_w1, ffn1_w3, ffn1_w2, mix_norm, w_in, ret_gn, ret_proj, s5_lam_re, s5_lam_im, s5_b_re, s5_b_im, s5_c_re, s5_c_im, s5_d, s5_log_step, glu_w, glu_b, s5_proj, w_out, ffn2_norm, ffn2_w1, ffn2_w3, ffn2_w2, final_norm):
    cos_t, sin_t = _rope_tables()
    ret_tabs, g1 = _retention_tables()
    x = jnp.concatenate([x_prompt.reshape(M_P, D_MODEL), x_sample.reshape(M_S, D_MODEL)], axis=0)

    ret_p, s5r_p, s5i_p, s5r_s, s5i_s = [], [], [], [], []
    ret_s = None
    nst = S5_GROUPS * S5_STATE
    rows3 = lambda a: a.reshape(DEPTH, 1, -1)
    ffn1_g, mix_g, ffn2_g, glu_b3 = rows3(ffn1_norm), rows3(mix_norm), rows3(ffn2_norm), rows3(glu_b)
    gn4 = ret_gn.reshape(DEPTH, RET_HEADS, 1, RET_DV)
    glu_w16, s5_proj16, ret_proj16, w_out16, ffn1_w2_16, ffn2_w2_16 = (
        a.astype(BF16) for a in (glu_w, s5_proj, ret_proj, w_out, ffn1_w2, ffn2_w2))

    lbr_x, lbi_x, bbr_t, bbi_t = _s5_discretize(s5_lam_re, s5_lam_im, s5_log_step, s5_b_re, s5_b_im)
    lbr = lbr_x[::S5_GROUP_CH].reshape(DEPTH, nst)
    lbi = lbi_x[::S5_GROUP_CH].reshape(DEPTH, nst)
    w_blk, cw_blk = _s5_block_weights(bbr_t, bbi_t, s5_c_re, s5_c_im)
    lamr_t, lami_t = _slab_tiles(lbr), _slab_tiles(lbi)
    lbr3, lbi3 = lbr.reshape(DEPTH, 1, nst), lbi.reshape(DEPTH, 1, nst)
    dvec3 = s5_d.reshape(DEPTH, 1, S5_W)
    h0r_all = state_s5_re.reshape(DEPTH, M_S, nst)
    h0i_all = state_s5_im.reshape(DEPTH, M_S, nst)

    xn = _norm(x, ffn1_g, 0)
    for l in range(DEPTH):
        hid = _ffn_up(xn, ffn1_w1, ffn1_w3, l)
        x, xn = _proj_res(hid, ffn1_w2_16, x, l, scale=0.5, tm=TM_DOWN, name="ffn_down",
                          gain=mix_g, gain_layer=l)

        proj = _inproj(xn, w_in, cos_t, sin_t, l)

        o_all, s_p = _ret_prompt(proj, ret_tabs, gn4, l)
        o_s, ret_s = _ret_sample(proj, state_ret, g1, gn4, l, ret_s)
        o_all = lax.dynamic_update_slice(o_all, o_s.astype(BF16), (M_P, 0))

        z5, hre_p, him_p = _s5_prompt(proj, w_blk, cw_blk, lamr_t, lami_t, dvec3, l)
        z_s, hre_s, him_s = _s5_sample(proj, h0r_all, h0i_all, w_blk, cw_blk, lbr3, lbi3, dvec3, l)
        z = lax.dynamic_update_slice(z5.reshape((BATCH + 1) * SEQ, S5_W), z_s, (M_P, 0))

        m = _merge(z, o_all, proj, glu_w16, glu_b3, s5_proj16, ret_proj16, l)
        x, xn = _proj_res(m, w_out16, x, l, scale=1.0, tm=TM_MERGE, name="out_proj",
                          gain=ffn2_g, gain_layer=l)

        hid = _ffn_up(xn, ffn2_w1, ffn2_w3, l)
        if l + 1 < DEPTH:
            x, xn = _proj_res(hid, ffn2_w2_16, x, l, scale=0.5, tm=TM_DOWN, name="ffn_down",
                              gain=ffn1_g, gain_layer=l + 1)
        else:
            x, = _proj_res(hid, ffn2_w2_16, x, l, scale=0.5, tm=TM_DOWN, name="ffn_down")

        unslab = lambda a: (a.transpose(1, 0, 2).reshape(BATCH, 2, S5_HALF_ST)
                            .reshape(BATCH, S5_GROUPS, S5_STATE))
        ret_p.append(s_p)
        s5r_p.append(unslab(hre_p))
        s5i_p.append(unslab(him_p))
        s5r_s.append(hre_s.reshape(M_S, S5_GROUPS, S5_STATE))
        s5i_s.append(him_s.reshape(M_S, S5_GROUPS, S5_STATE))

    g_fin = final_norm.reshape(1, D_MODEL)
    y_p = _final_norm(x, g_fin, M_P, 1024, 0)
    y_s = _final_norm(x, g_fin, M_S, M_S, M_P // M_S)
    return (y_p.reshape(BATCH, SEQ, D_MODEL), y_s.reshape(M_S, 1, D_MODEL),
            jnp.stack(ret_p), jnp.stack(s5r_p), jnp.stack(s5i_p),
            ret_s, jnp.stack(s5r_s), jnp.stack(s5i_s))
```

```python
import functools
import math

import jax
import jax.numpy as jnp
from jax import lax
from jax.experimental import pallas as pl
from jax.experimental.pallas import tpu as pltpu

F32 = jnp.float32
BF16 = jnp.bfloat16

D_MODEL = 2048
BATCH = 4
SEQ = 2048
DEPTH = 4
DEC_BATCH = 128
PAST_LEN = 16384
RET_HEADS = 8
RET_DK = 128
RET_DV = 128
RET_QK = RET_HEADS * RET_DK
RET_W = RET_HEADS * RET_DV
RET_CHUNK = 128
ROPE_BASE = 10000.0
S5_GROUPS = 64
S5_GROUP_CH = 16
S5_W = S5_GROUPS * S5_GROUP_CH
S5_STATE = 64
D_FF = 5632
IN_W = 2 * RET_QK + 2 * RET_W + S5_W + 2 * D_MODEL
NORM_EPS = 1e-6
HEAD_NORM_EPS = 1e-5

M_P = BATCH * SEQ
M_S = DEC_BATCH
M_ALL = M_P + M_S

LANES = 128
SUBLANES = 8
VMEM_LIMIT = 58 * 1024 * 1024

TM = 1040
TN_FF = 512
TM_DOWN = 416
COL = 1024
TM_BIG = 4 * TM
TN_IN = 512
IN_SUB = 256

S5_TC = 64
S5_PAD = SUBLANES // 2
S5_PITCH = S5_TC + S5_PAD
S5_GB = 16
S5_NBLK = S5_GROUPS // S5_GB
S5_BLK_CH = S5_GB * S5_GROUP_CH
S5_BLK_ST = S5_GB * S5_STATE
S5_HALF_ST = S5_GROUPS * S5_STATE // 2
S5_NSLAB = S5_HALF_ST // LANES
S5_NSEQ = BATCH * 2
S5_SLAB_GROUP = 4


def _cparams(*sem):
    return pltpu.CompilerParams(dimension_semantics=sem, vmem_limit_bytes=VMEM_LIMIT)


def _rms_to_bf16(x, g):
    ms = jnp.mean(x * x, axis=-1, keepdims=True)
    return ((x * lax.rsqrt(ms + NORM_EPS)) * g).astype(BF16)


def _layer_spec(l, k, tn):
    return pl.BlockSpec((None, k, tn), lambda i, j: (l, 0, j))


def _layer_row_spec(l, n):
    return pl.BlockSpec((None, 1, n), lambda *_: (l, 0, 0))


def _norm_kernel(x_ref, g_ref, o_ref):
    o_ref[...] = _rms_to_bf16(x_ref[...], g_ref[...])


def _norm(x, g, l):
    return pl.pallas_call(
        _norm_kernel,
        grid=(M_ALL // TM,),
        in_specs=[pl.BlockSpec((TM, D_MODEL), lambda i: (i, 0)), _layer_row_spec(l, D_MODEL)],
        out_specs=pl.BlockSpec((TM, D_MODEL), lambda i: (i, 0)),
        out_shape=jax.ShapeDtypeStruct((M_ALL, D_MODEL), BF16),
        compiler_params=_cparams("parallel"),
        name="norm",
    )(x, g)


def _resident_rows_spec():
    return pl.BlockSpec((TM_BIG, D_MODEL), lambda r, j: (r, 0), pipeline_mode=pl.Buffered(1))


def _ffn_up_kernel(xn_ref, w1_ref, w3_ref, o_ref):
    w1 = w1_ref[...].astype(BF16)
    w3 = w3_ref[...].astype(BF16)
    for s in range(TM_BIG // TM):
        rows = slice(s * TM, (s + 1) * TM)
        xs = xn_ref[rows, :]
        h1 = jnp.dot(xs, w1, preferred_element_type=F32)
        h3 = jnp.dot(xs, w3, preferred_element_type=F32)
        o_ref[rows, :] = ((h1 * jax.nn.sigmoid(h1)) * h3).astype(BF16)


def _ffn_up(xn, w1, w3, l):
    return pl.pallas_call(
        _ffn_up_kernel,
        grid=(M_ALL // TM_BIG, D_FF // TN_FF),
        in_specs=[
            _resident_rows_spec(),
            _layer_spec(l, D_MODEL, TN_FF),
            _layer_spec(l, D_MODEL, TN_FF),
        ],
        out_specs=pl.BlockSpec((TM_BIG, TN_FF), lambda r, j: (r, j)),
        out_shape=jax.ShapeDtypeStruct((M_ALL, D_FF), BF16),
        compiler_params=_cparams("parallel", "arbitrary"),
        name="ffn_up",
    )(xn, w1, w3)


def _resident_layer_spec(l, shape):
    return pl.BlockSpec((None,) + shape, lambda i: (l, 0, 0), pipeline_mode=pl.Buffered(1))


def _proj_res_kernel(a_ref, w_ref, x_ref, *rest, scale):
    o_ref = rest[-2] if len(rest) == 3 else rest[0]
    for c in range(D_MODEL // COL):
        cols = slice(c * COL, (c + 1) * COL)
        acc = jnp.dot(a_ref[...], w_ref[:, cols], preferred_element_type=F32)
        o_ref[:, cols] = x_ref[:, cols] + scale * acc
    if len(rest) == 3:
        rest[2][...] = _rms_to_bf16(o_ref[...], rest[0][...])


def _proj_res(a, w16, x, l, *, scale, tm, name, gain=None, gain_layer=None):
    k = a.shape[1]
    row_spec = lambda: pl.BlockSpec((tm, D_MODEL), lambda i: (i, 0))
    in_specs = [pl.BlockSpec((tm, k), lambda i: (i, 0)), _resident_layer_spec(l, (k, D_MODEL)),
                row_spec()]
    args = [a, w16, x]
    out_specs = [row_spec()]
    out_shape = [jax.ShapeDtypeStruct((M_ALL, D_MODEL), F32)]
    if gain is not None:
        in_specs.append(_layer_row_spec(gain_layer, D_MODEL))
        args.append(gain)
        out_specs.append(row_spec())
        out_shape.append(jax.ShapeDtypeStruct((M_ALL, D_MODEL), BF16))
    return pl.pallas_call(
        functools.partial(_proj_res_kernel, scale=scale),
        grid=(M_ALL // tm,),
        in_specs=in_specs,
        out_specs=out_specs,
        out_shape=out_shape,
        compiler_params=_cparams("parallel"),
        name=name,
    )(*args)


def _inproj_kernel(xn_ref, w_ref, cos_ref, sin_ref, o_ref):
    seg = pl.program_id(1) // (COL // TN_IN)

    def project(epilogue):
        wb = [w_ref[:, c * IN_SUB:(c + 1) * IN_SUB].astype(BF16) for c in range(TN_IN // IN_SUB)]
        for s in range(TM_BIG // TM):
            rows = slice(s * TM, (s + 1) * TM)
            for c in range(TN_IN // IN_SUB):
                acc = jnp.dot(xn_ref[rows, :], wb[c], preferred_element_type=F32)
                o_ref[rows, c * IN_SUB:(c + 1) * IN_SUB] = epilogue(acc, rows)

    def rotary(acc, rows):
        scale = jnp.where(seg == 0, RET_DK ** -0.5, 1.0).astype(F32)
        cos = cos_ref[rows, :]
        sin = sin_ref[rows, :]
        heads = []
        for h in range(IN_SUB // RET_DK):
            a = acc[:, h * RET_DK:(h + 1) * RET_DK]
            heads.append((a * cos + pltpu.roll(a, RET_DK // 2, axis=1) * sin) * scale)
        return jnp.concatenate(heads, axis=1)

    @pl.when(seg <= 1)
    def _():
        project(rotary)

    @pl.when((seg == 2) | (seg == 4))
    def _():
        project(lambda acc, rows: acc)

    @pl.when(seg == 3)
    def _():
        project(lambda acc, rows: acc * jax.nn.sigmoid(acc))

    @pl.when(seg >= 5)
    def _():
        project(lambda acc, rows: jax.nn.sigmoid(acc))


def _inproj(xn, w, cos_t, sin_t, l):
    tab_spec = pl.BlockSpec((TM_BIG, RET_DK), lambda r, j: (r, 0), pipeline_mode=pl.Buffered(1))
    return pl.pallas_call(
        _inproj_kernel,
        grid=(M_ALL // TM_BIG, IN_W // TN_IN),
        in_specs=[
            _resident_rows_spec(),
            _layer_spec(l, D_MODEL, TN_IN),
            tab_spec, tab_spec,
        ],
        out_specs=pl.BlockSpec((TM_BIG, TN_IN), lambda r, j: (r, j)),
        out_shape=jax.ShapeDtypeStruct((M_ALL, IN_W), F32),
        compiler_params=_cparams("parallel", "arbitrary"),
        name="inproj",
    )(xn, w, cos_t, sin_t)


def _head_norm(o):
    mu = jnp.mean(o, axis=-1, keepdims=True)
    d = o - mu
    var = jnp.mean(d * d, axis=-1, keepdims=True)
    return d * lax.rsqrt(var + HEAD_NORM_EPS)


RET_RC = 4 * RET_CHUNK


def _ret_prompt_kernel(q_ref, k_ref, v_ref, rg_ref, intra_ref, qd_ref, kd_ref, cd_ref, gn_ref,
                       o_ref, s_ref, st_ref):
    step = pl.program_id(1)

    @pl.when(step == 0)
    def _():
        st_ref[...] = jnp.zeros_like(st_ref)

    nc = RET_RC // RET_CHUNK
    blk = lambda c, h: (slice(c * RET_CHUNK, (c + 1) * RET_CHUNK), slice(h * RET_DK, (h + 1) * RET_DK))

    o_intra, kv = {}, {}
    for c in range(nc):
        for h in range(RET_HEADS):
            qc = q_ref[blk(c, h)]
            kc = k_ref[blk(c, h)]
            vb = v_ref[blk(c, h)].astype(BF16)
            sc = lax.dot_general(qc.astype(BF16), kc.astype(BF16), (((1,), (1,)), ((), ())),
                                 preferred_element_type=F32) * intra_ref[h]
            o_intra[c, h] = jnp.dot(sc.astype(BF16), vb, preferred_element_type=F32)
            kt = jnp.transpose(kc * kd_ref[h]).astype(BF16)
            kv[c, h] = jnp.dot(kt, vb, preferred_element_type=F32)

    for h in range(RET_HEADS):
        s = st_ref[h]
        for c in range(nc):
            qs = jnp.dot((q_ref[blk(c, h)] * qd_ref[h]).astype(BF16), s.astype(BF16),
                         preferred_element_type=F32)
            o = o_intra[c, h] + qs
            s = s * cd_ref[h] + kv[c, h]
            on = (_head_norm(o) * gn_ref[h]) * rg_ref[blk(c, h)]
            o_ref[blk(c, h)] = on.astype(BF16)
        st_ref[h] = s

    @pl.when(step == pl.num_programs(1) - 1)
    def _():
        s_ref[...] = st_ref[...]


def _ret_prompt(proj, tabs, gn, l):
    intra, qd, kd, cd = tabs
    nsteps = SEQ // RET_RC
    tab_spec = pl.BlockSpec((RET_HEADS, RET_CHUNK, RET_CHUNK), lambda b, s: (0, 0, 0))

    def seg_spec(seg):
        return pl.BlockSpec((RET_RC, COL), lambda b, s, seg=seg: (b * nsteps + s, seg))

    return pl.pallas_call(
        _ret_prompt_kernel,
        grid=(BATCH, nsteps),
        in_specs=[
            seg_spec(0), seg_spec(1), seg_spec(2), seg_spec(3),
            tab_spec, tab_spec, tab_spec, tab_spec,
            pl.BlockSpec((None, RET_HEADS, 1, RET_DV), lambda b, s: (l, 0, 0, 0)),
        ],
        out_specs=[
            pl.BlockSpec((RET_RC, RET_W), lambda b, s: (b * nsteps + s, 0)),
            pl.BlockSpec((None, RET_HEADS, RET_DK, RET_DV), lambda b, s: (b, 0, 0, 0)),
        ],
        out_shape=[
            jax.ShapeDtypeStruct((M_ALL, RET_W), BF16),
            jax.ShapeDtypeStruct((BATCH, RET_HEADS, RET_DK, RET_DV), F32),
        ],
        scratch_shapes=[pltpu.VMEM((RET_HEADS, RET_DK, RET_DV), F32)],
        compiler_params=_cparams("parallel", "arbitrary"),
        name="ret_prompt",
    )(proj, proj, proj, proj, intra, qd, kd, cd, gn)


RET_SB = 8


def _col_bcast(row, eye, ones2):
    hi = row.astype(BF16)
    lo = (row - hi.astype(F32)).astype(BF16)
    parts = [jnp.where(eye, jnp.broadcast_to(t.astype(F32), (RET_DK, RET_DK)), 0.0).astype(BF16)
             for t in (hi, lo)]
    return jnp.dot(jnp.concatenate(parts, axis=1), ones2, preferred_element_type=F32)


def _ret_sample_kernel(q_ref, k_ref, v_ref, rg_ref, s_ref, g1_ref, gn_ref, *rest):
    o_ref, so_ref = rest[-2:]
    eye = (lax.broadcasted_iota(jnp.int32, (RET_DK, RET_DK), 0)
           == lax.broadcasted_iota(jnp.int32, (RET_DK, RET_DK), 1))
    ones2 = jnp.ones((2 * RET_DK, RET_DV), BF16)

    for b in range(RET_SB):
        row = slice(b, b + 1)
        for h in range(RET_HEADS):
            cols = slice(h * RET_DK, (h + 1) * RET_DK)
            kcol = _col_bcast(k_ref[row, cols], eye, ones2)
            qcol = _col_bcast(q_ref[row, cols], eye, ones2)
            s_new = s_ref[b, h] * g1_ref[h] + kcol * v_ref[row, cols]
            so_ref[b, h] = s_new
            o = jnp.sum(qcol * s_new, axis=0, keepdims=True)
            o_ref[row, cols] = (_head_norm(o) * gn_ref[h]) * rg_ref[row, cols]


def _ret_sample(proj, state, g1, gn, l, prev_states):
    row0 = M_P // RET_SB

    def seg_spec(seg):
        return pl.BlockSpec((RET_SB, COL), lambda i, seg=seg: (row0 + i, seg))

    st_spec = pl.BlockSpec((None, RET_SB, RET_HEADS, RET_DK, RET_DV), lambda i: (l, i, 0, 0, 0))
    in_specs = [
        seg_spec(0), seg_spec(1), seg_spec(2), seg_spec(3), st_spec,
        pl.BlockSpec((RET_HEADS, 1, RET_DV), lambda i: (0, 0, 0)),
        pl.BlockSpec((None, RET_HEADS, 1, RET_DV), lambda i: (l, 0, 0, 0)),
    ]
    args = [proj, proj, proj, proj, state, g1, gn]
    aliases = {}
    if prev_states is not None:
        in_specs.append(pl.BlockSpec(memory_space=pl.ANY))
        args.append(prev_states)
        aliases = {len(args) - 1: 1}
    return pl.pallas_call(
        _ret_sample_kernel,
        grid=(M_S // RET_SB,),
        in_specs=in_specs,
        out_specs=[pl.BlockSpec((RET_SB, RET_W), lambda i: (i, 0)), st_spec],
        out_shape=[
            jax.ShapeDtypeStruct((M_S, RET_W), F32),
            jax.ShapeDtypeStruct((DEPTH, M_S, RET_HEADS, RET_DK, RET_DV), F32),
        ],
        input_output_aliases=aliases,
        compiler_params=_cparams("parallel"),
        name="ret_sample",
    )(*args)


def _s5_disc_kernel(lr_ref, li_ref, ls_ref, br_ref, bi_ref, lbr_ref, lbi_ref, bbr_ref, bbi_ref):
    lam_r = lr_ref[...]
    lam_i = li_ref[...]
    dt = jnp.exp(ls_ref[...])
    ea = jnp.exp(lam_r * dt)
    lb_r = ea * jnp.cos(lam_i * dt)
    lb_i = ea * jnp.sin(lam_i * dt)
    lbr_ref[...] = lb_r
    lbi_ref[...] = lb_i
    nr = lb_r - 1.0
    den = lam_r * lam_r + lam_i * lam_i
    cr = (nr * lam_r + lb_i * lam_i) / den
    ci = (lb_i * lam_r - nr * lam_i) / den
    b_r = br_ref[...]
    b_i = bi_ref[...]
    bbr_ref[...] = cr * b_r - ci * b_i
    bbi_ref[...] = cr * b_i + ci * b_r


def _s5_discretize(lam_re, lam_im, log_step, b_re, b_im):
    rows = DEPTH * S5_GROUPS * S5_GROUP_CH
    rep = lambda a: jnp.repeat(a.reshape(DEPTH * S5_GROUPS, S5_STATE), S5_GROUP_CH, axis=0)
    tr = lambda a: jnp.transpose(a, (0, 1, 3, 2)).reshape(rows, S5_STATE)
    shp = jax.ShapeDtypeStruct((rows, S5_STATE), F32)
    ls = jnp.broadcast_to(log_step[:, :, None], (DEPTH, S5_GROUPS, S5_STATE))
    spec = pl.BlockSpec((rows // DEPTH, S5_STATE), lambda l: (l, 0))
    return pl.pallas_call(
        _s5_disc_kernel,
        grid=(DEPTH,),
        in_specs=[spec] * 5,
        out_specs=[spec] * 4,
        out_shape=[shp, shp, shp, shp],
        compiler_params=_cparams("parallel"),
        name="s5_disc",
    )(rep(lam_re), rep(lam_im), rep(ls), tr(b_re), tr(b_im))


def _gelu_tanh(y):
    c = math.sqrt(2.0 / math.pi)
    return 0.5 * y * (1.0 + jnp.tanh(c * (y + 0.044715 * (y * y * y))))


def _s5_prompt_kernel(u0_ref, u1_ref, u2_ref, u3_ref, w_ref, cw_ref, lamr_ref, lami_ref, d_ref,
                      z_ref, hre_ref, him_ref, bre_ref, bim_ref):
    j = pl.program_id(0)
    u_refs = (u0_ref, u1_ref, u2_ref, u3_ref)

    @pl.when(j == 0)
    def _():
        hre_ref[...] = jnp.zeros_like(hre_ref)
        him_ref[...] = jnp.zeros_like(him_ref)

    nsl = S5_BLK_ST // LANES
    zpad = jnp.zeros((S5_PAD, S5_BLK_CH), F32)

    for gh in range(2):
        seg = S5_TC + 2 * S5_PAD * gh
        for kb in range(2):
            blk = gh * 2 + kb
            c0 = blk * S5_BLK_CH
            parts = []
            for u in u_refs:
                ub = u[:, c0:c0 + S5_BLK_CH]
                parts.extend([zpad, ub, zpad] if gh else [ub])
            x = jnp.concatenate(parts, axis=0).astype(BF16)
            res = jnp.dot(x, w_ref[blk], preferred_element_type=F32)
            for b in range(BATCH):
                r0 = (b * 2 + gh) * S5_PITCH - S5_PAD * gh
                part = res[b * seg:(b + 1) * seg]
                for s in range(nsl):
                    slab = kb * nsl + s
                    bre_ref[slab, r0:r0 + seg, :] = part[:, s * LANES:(s + 1) * LANES]
                    bim_ref[slab, r0:r0 + seg, :] = part[:, S5_BLK_ST + s * LANES:S5_BLK_ST + (s + 1) * LANES]

    for sg in range(S5_NSLAB // S5_SLAB_GROUP):
        slabs = [sg * S5_SLAB_GROUP + i for i in range(S5_SLAB_GROUP)]
        lam_r = [lamr_ref[s] for s in slabs]
        lam_i = [lami_ref[s] for s in slabs]

        def step(t, carry, slabs=slabs, lam_r=lam_r, lam_i=lam_i):
            hr, hi = carry
            nr_all = []
            ni_all = []
            for i, s in enumerate(slabs):
                idx = pl.ds(t, S5_NSEQ, stride=S5_PITCH)
                br = bre_ref[s, idx, :]
                bi = bim_ref[s, idx, :]
                nr = lam_r[i] * hr[i] - lam_i[i] * hi[i] + br
                ni = lam_r[i] * hi[i] + lam_i[i] * hr[i] + bi
                bre_ref[s, idx, :] = nr
                bim_ref[s, idx, :] = ni
                nr_all.append(nr)
                ni_all.append(ni)
            return tuple(nr_all), tuple(ni_all)

        init = (tuple(hre_ref[s] for s in slabs), tuple(him_ref[s] for s in slabs))
        hr, hi = lax.fori_loop(0, S5_TC, step, init, unroll=2)
        for i, s in enumerate(slabs):
            hre_ref[s] = hr[i]
            him_ref[s] = hi[i]

    for gh in range(2):
        seg = S5_TC + 2 * S5_PAD * gh
        for kb in range(2):
            blk = gh * 2 + kb
            c0 = blk * S5_BLK_CH
            pieces = []
            for src in (bre_ref, bim_ref):
                for s in range(nsl):
                    slab = kb * nsl + s
                    starts = [(b * 2 + gh) * S5_PITCH - S5_PAD * gh for b in range(BATCH)]
                    pieces.append(jnp.concatenate(
                        [src[slab, r0:r0 + seg, :] for r0 in starts], axis=0))
            hcat = jnp.concatenate(pieces, axis=1).astype(BF16)
            y = jnp.dot(hcat, cw_ref[blk], preferred_element_type=F32)
            dvec = d_ref[:, c0:c0 + S5_BLK_CH]
            for b in range(BATCH):
                lo = b * seg + S5_PAD * gh
                yb = y[lo:lo + S5_TC] + dvec * u_refs[b][:, c0:c0 + S5_BLK_CH]
                z_ref[b, :, c0:c0 + S5_BLK_CH] = _gelu_tanh(yb)


def _s5_prompt(proj, w_blk, cw_blk, lamr_t, lami_t, dvec, l):
    nsteps = SEQ // S5_TC
    su_col = (2 * RET_QK + 2 * RET_W) // S5_W

    def u_spec(b):
        return pl.BlockSpec((S5_TC, S5_W), lambda j, b=b: (b * nsteps + j, su_col))

    layer4 = lambda shape: pl.BlockSpec((None,) + shape, lambda j: (l, 0, 0, 0))
    full3 = lambda shape: pl.BlockSpec(shape, lambda j: (0, 0, 0))
    st_shape = (S5_NSLAB, S5_NSEQ, LANES)
    return pl.pallas_call(
        _s5_prompt_kernel,
        grid=(nsteps,),
        in_specs=[
            u_spec(0), u_spec(1), u_spec(2), u_spec(3),
            layer4((S5_NBLK, S5_BLK_CH, 2 * S5_BLK_ST)),
            layer4((S5_NBLK, 2 * S5_BLK_ST, S5_BLK_CH)),
            layer4(st_shape), layer4(st_shape),
            _layer_row_spec(l, S5_W),
        ],
        out_specs=[
            pl.BlockSpec((BATCH, S5_TC, S5_W), lambda j: (0, j, 0)),
            full3(st_shape), full3(st_shape),
        ],
        out_shape=[
            jax.ShapeDtypeStruct((BATCH + 1, SEQ, S5_W), F32),
            jax.ShapeDtypeStruct(st_shape, F32),
            jax.ShapeDtypeStruct(st_shape, F32),
        ],
        scratch_shapes=[
            pltpu.VMEM((S5_NSLAB, S5_NSEQ * S5_PITCH, LANES), F32),
            pltpu.VMEM((S5_NSLAB, S5_NSEQ * S5_PITCH, LANES), F32),
        ],
        compiler_params=_cparams("arbitrary"),
        name="s5_prompt",
    )(proj, proj, proj, proj, w_blk, cw_blk, lamr_t, lami_t, dvec)


def _s5_sample_kernel(u_ref, h0r_ref, h0i_ref, w_ref, cw_ref, lbr_ref, lbi_ref, d_ref,
                      z_ref, hr_ref, hi_ref):
    for blk in range(S5_NBLK):
        c0 = blk * S5_BLK_CH
        s0 = blk * S5_BLK_ST
        u = u_ref[:, c0:c0 + S5_BLK_CH]
        res = jnp.dot(u.astype(BF16), w_ref[blk], preferred_element_type=F32)
        lr = lbr_ref[:, s0:s0 + S5_BLK_ST]
        li = lbi_ref[:, s0:s0 + S5_BLK_ST]
        h0r = h0r_ref[:, s0:s0 + S5_BLK_ST]
        h0i = h0i_ref[:, s0:s0 + S5_BLK_ST]
        hr = lr * h0r - li * h0i + res[:, :S5_BLK_ST]
        hi = lr * h0i + li * h0r + res[:, S5_BLK_ST:]
        hr_ref[:, s0:s0 + S5_BLK_ST] = hr
        hi_ref[:, s0:s0 + S5_BLK_ST] = hi
        hcat = jnp.concatenate([hr, hi], axis=1).astype(BF16)
        y = jnp.dot(hcat, cw_ref[blk], preferred_element_type=F32) + d_ref[:, c0:c0 + S5_BLK_CH] * u
        z_ref[:, c0:c0 + S5_BLK_CH] = _gelu_tanh(y)


def _s5_sample(proj, h0r, h0i, w_blk, cw_blk, lbr, lbi, dvec, l):
    nst = S5_GROUPS * S5_STATE
    su_col = (2 * RET_QK + 2 * RET_W) // S5_W
    layer3 = lambda shape: pl.BlockSpec((None,) + shape, lambda i: (l, 0, 0))
    layer4 = lambda shape: pl.BlockSpec((None,) + shape, lambda i: (l, 0, 0, 0))
    whole = lambda shape: pl.BlockSpec(shape, lambda i: (0, 0))
    return pl.pallas_call(
        _s5_sample_kernel,
        grid=(1,),
        in_specs=[
            pl.BlockSpec((M_S, S5_W), lambda i: (M_P // M_S, su_col)),
            layer3((M_S, nst)), layer3((M_S, nst)),
            layer4((S5_NBLK, S5_BLK_CH, 2 * S5_BLK_ST)),
            layer4((S5_NBLK, 2 * S5_BLK_ST, S5_BLK_CH)),
            layer3((1, nst)), layer3((1, nst)),
            _layer_row_spec(l, S5_W),
        ],
        out_specs=[whole((M_S, S5_W)), whole((M_S, nst)), whole((M_S, nst))],
        out_shape=[
            jax.ShapeDtypeStruct((M_S, S5_W), F32),
            jax.ShapeDtypeStruct((M_S, nst), F32),
            jax.ShapeDtypeStruct((M_S, nst), F32),
        ],
        compiler_params=_cparams("arbitrary"),
        name="s5_sample",
    )(proj, h0r, h0i, w_blk, cw_blk, lbr, lbi, dvec)


TM_MERGE = 520


def _merge_kernel(z_ref, o_ref, gr0_ref, gr1_ref, gs0_ref, gs1_ref, gluw_ref, glub_ref, s5p_ref,
                  retp_ref, m_ref):
    z = z_ref[...]
    t = jnp.dot(z.astype(BF16), gluw_ref[...], preferred_element_type=F32) + glub_ref[...]
    z2 = (z * jax.nn.sigmoid(t)).astype(BF16)
    o = o_ref[...]
    for c, (gr_ref, gs_ref) in enumerate(((gr0_ref, gs0_ref), (gr1_ref, gs1_ref))):
        cols = slice(c * COL, (c + 1) * COL)
        b_s5 = jnp.dot(z2, s5p_ref[:, cols], preferred_element_type=F32)
        b_ret = jnp.dot(o, retp_ref[:, cols], preferred_element_type=F32)
        m_ref[:, cols] = (gr_ref[...] * b_ret + gs_ref[...] * b_s5).astype(BF16)


def _merge(z, o, proj, glu_w, glu_b, s5_proj, ret_proj, l):
    gr_col = (2 * RET_QK + 2 * RET_W + S5_W) // COL
    gs_col = gr_col + D_MODEL // COL
    gate = lambda col: pl.BlockSpec((TM_MERGE, COL), lambda i, col=col: (i, col))
    return pl.pallas_call(
        _merge_kernel,
        grid=(M_ALL // TM_MERGE,),
        in_specs=[
            pl.BlockSpec((TM_MERGE, S5_W), lambda i: (i, 0)),
            pl.BlockSpec((TM_MERGE, RET_W), lambda i: (i, 0)),
            gate(gr_col), gate(gr_col + 1), gate(gs_col), gate(gs_col + 1),
            _resident_layer_spec(l, (S5_W, S5_W)),
            _layer_row_spec(l, S5_W),
            _resident_layer_spec(l, (S5_W, D_MODEL)),
            _resident_layer_spec(l, (RET_W, D_MODEL)),
        ],
        out_specs=pl.BlockSpec((TM_MERGE, D_MODEL), lambda i: (i, 0)),
        out_shape=jax.ShapeDtypeStruct((M_ALL, D_MODEL), BF16),
        compiler_params=_cparams("parallel"),
        name="merge",
    )(z, o, proj, proj, proj, proj, glu_w, glu_b, s5_proj, ret_proj)


def _final_norm_kernel(x_ref, g_ref, o_ref):
    x = x_ref[...]
    ms = jnp.mean(x * x, axis=-1, keepdims=True)
    o_ref[...] = (x * lax.rsqrt(ms + NORM_EPS)) * g_ref[...]


def _final_norm(x, g, rows, tm, block0):
    return pl.pallas_call(
        _final_norm_kernel,
        grid=(rows // tm,),
        in_specs=[pl.BlockSpec((tm, D_MODEL), lambda i: (block0 + i, 0)),
                  pl.BlockSpec((1, D_MODEL), lambda i: (0, 0))],
        out_specs=pl.BlockSpec((tm, D_MODEL), lambda i: (i, 0)),
        out_shape=jax.ShapeDtypeStruct((rows, D_MODEL), F32),
        compiler_params=_cparams("parallel"),
        name="final_norm",
    )(x, g)


def _rope_tables():
    half = RET_DK // 2
    inv = ROPE_BASE ** (-jnp.arange(half, dtype=F32) / half)
    pos = jnp.concatenate([jnp.tile(jnp.arange(SEQ, dtype=jnp.int32), BATCH),
                           jnp.full((M_S,), PAST_LEN, jnp.int32)])
    ang = pos.astype(F32)[:, None] * inv[None, :]
    cos = jnp.cos(ang)
    sin = jnp.sin(ang)
    return jnp.concatenate([cos, cos], axis=1), jnp.concatenate([-sin, sin], axis=1)


def _retention_tables():
    c = RET_CHUNK
    log_g = jnp.log1p(-jnp.exp2(-5.0 - jnp.arange(RET_HEADS, dtype=F32)))
    i = jnp.arange(c, dtype=F32)
    diff = i[:, None] - i[None, :]
    intra = jnp.where(diff[None] >= 0,
                      jnp.exp(jnp.maximum(diff, 0.0)[None] * log_g[:, None, None]), 0.0)
    q_dec = jnp.exp((i + 1.0)[None, :] * log_g[:, None])
    k_dec = jnp.exp((c - 1.0 - i)[None, :] * log_g[:, None])
    c_dec = jnp.exp(c * log_g)
    full = lambda a: jnp.broadcast_to(a, (RET_HEADS, c, RET_DV))
    g1 = jnp.broadcast_to(jnp.exp(1.0 * log_g)[:, None, None], (RET_HEADS, 1, RET_DV))
    return (intra, full(q_dec[:, :, None]), full(k_dec[:, :, None]), full(c_dec[:, None, None])), g1


def _s5_block_weights(bbr_t, bbi_t, c_re, c_im):
    ch_group = jnp.arange(S5_BLK_CH, dtype=jnp.int32) // S5_GROUP_CH
    st_group = jnp.arange(S5_BLK_ST, dtype=jnp.int32) // S5_STATE
    drive_mask = ch_group[:, None] == st_group[None, :]

    def drive(bt):
        a = bt.reshape(DEPTH, S5_NBLK, S5_BLK_CH, 1, S5_STATE)
        a = jnp.broadcast_to(a, (DEPTH, S5_NBLK, S5_BLK_CH, S5_GB, S5_STATE))
        return jnp.where(drive_mask, a.reshape(DEPTH, S5_NBLK, S5_BLK_CH, S5_BLK_ST), 0.0)

    def readout(cm):
        a = cm.reshape(DEPTH, S5_NBLK, S5_GB, S5_GROUP_CH, S5_STATE)
        a = a.transpose(0, 1, 4, 2, 3).reshape(DEPTH, S5_NBLK, 1, S5_STATE, S5_BLK_CH)
        a = jnp.broadcast_to(a, (DEPTH, S5_NBLK, S5_GB, S5_STATE, S5_BLK_CH))
        return jnp.where(drive_mask.T, a.reshape(DEPTH, S5_NBLK, S5_BLK_ST, S5_BLK_CH), 0.0)

    w_blk = jnp.concatenate([drive(bbr_t), drive(bbi_t)], axis=3).astype(BF16)
    cw_blk = jnp.concatenate([readout(c_re), readout(-c_im)], axis=2).astype(BF16)
    return w_blk, cw_blk


def _slab_tiles(lb):
    a = lb.reshape(DEPTH, 2, S5_NSLAB, LANES).transpose(0, 2, 1, 3)
    return jnp.tile(a, (1, 1, BATCH, 1))


def kernel(x_prompt, x_sample, state_ret, state_s5_re, state_s5_im, ffn1_norm, ffn1_w1, ffn1_w3, ffn1_w2, mix_norm, w_in, ret_gn, ret_proj, s5_lam_re, s5_lam_im, s5_b_re, s5_b_im, s5_c_re, s5_c_im, s5_d, s5_log_step, glu_w, glu_b, s5_proj, w_out, ffn2_norm, ffn2_w1, ffn2_w3, ffn2_w2, final_norm):
    cos_t, sin_t = _rope_tables()
    ret_tabs, g1 = _retention_tables()
    x = jnp.concatenate([x_prompt.reshape(M_P, D_MODEL), x_sample.reshape(M_S, D_MODEL)], axis=0)

    ret_p, s5r_p, s5i_p, s5r_s, s5i_s = [], [], [], [], []
    ret_s = None
    nst = S5_GROUPS * S5_STATE
    rows3 = lambda a: a.reshape(DEPTH, 1, -1)
    ffn1_g, mix_g, ffn2_g, glu_b3 = rows3(ffn1_norm), rows3(mix_norm), rows3(ffn2_norm), rows3(glu_b)
    gn4 = ret_gn.reshape(DEPTH, RET_HEADS, 1, RET_DV)
    glu_w16, s5_proj16, ret_proj16, w_out16, ffn1_w2_16, ffn2_w2_16 = (
        a.astype(BF16) for a in (glu_w, s5_proj, ret_proj, w_out, ffn1_w2, ffn2_w2))

    lbr_x, lbi_x, bbr_t, bbi_t = _s5_discretize(s5_lam_re, s5_lam_im, s5_log_step, s5_b_re, s5_b_im)
    lbr = lbr_x[::S5_GROUP_CH].reshape(DEPTH, nst)
    lbi = lbi_x[::S5_GROUP_CH].reshape(DEPTH, nst)
    w_blk, cw_blk = _s5_block_weights(bbr_t, bbi_t, s5_c_re, s5_c_im)
    lamr_t, lami_t = _slab_tiles(lbr), _slab_tiles(lbi)
    lbr3, lbi3 = lbr.reshape(DEPTH, 1, nst), lbi.reshape(DEPTH, 1, nst)
    dvec3 = s5_d.reshape(DEPTH, 1, S5_W)
    h0r_all = state_s5_re.reshape(DEPTH, M_S, nst)
    h0i_all = state_s5_im.reshape(DEPTH, M_S, nst)

    xn = _norm(x, ffn1_g, 0)
    for l in range(DEPTH):
        hid = _ffn_up(xn, ffn1_w1, ffn1_w3, l)
        x, xn = _proj_res(hid, ffn1_w2_16, x, l, scale=0.5, tm=TM_DOWN, name="ffn_down",
                          gain=mix_g, gain_layer=l)

        proj = _inproj(xn, w_in, cos_t, sin_t, l)

        o_all, s_p = _ret_prompt(proj, ret_tabs, gn4, l)
        o_s, ret_s = _ret_sample(proj, state_ret, g1, gn4, l, ret_s)
        o_all = lax.dynamic_update_slice(o_all, o_s.astype(BF16), (M_P, 0))

        z5, hre_p, him_p = _s5_prompt(proj, w_blk, cw_blk, lamr_t, lami_t, dvec3, l)
        z_s, hre_s, him_s = _s5_sample(proj, h0r_all, h0i_all, w_blk, cw_blk, lbr3, lbi3, dvec3, l)
        z = lax.dynamic_update_slice(z5.reshape((BATCH + 1) * SEQ, S5_W), z_s, (M_P, 0))

        m = _merge(z, o_all, proj, glu_w16, glu_b3, s5_proj16, ret_proj16, l)
        x, xn = _proj_res(m, w_out16, x, l, scale=1.0, tm=TM_MERGE, name="out_proj",
                          gain=ffn2_g, gain_layer=l)

        hid = _ffn_up(xn, ffn2_w1, ffn2_w3, l)
        if l + 1 < DEPTH:
            x, xn = _proj_res(hid, ffn2_w2_16, x, l, scale=0.5, tm=TM_DOWN, name="ffn_down",
                              gain=ffn1_g, gain_layer=l + 1)
        else:
            x, = _proj_res(hid, ffn2_w2_16, x, l, scale=0.5, tm=TM_DOWN, name="ffn_down")

        unslab = lambda a: (a.transpose(1, 0, 2).reshape(BATCH, 2, S5_HALF_ST)
                            .reshape(BATCH, S5_GROUPS, S5_STATE))
        ret_p.append(s_p)
        s5r_p.append(unslab(hre_p))
        s5i_p.append(unslab(him_p))
        s5r_s.append(hre_s.reshape(M_S, S5_GROUPS, S5_STATE))
        s5i_s.append(him_s.reshape(M_S, S5_GROUPS, S5_STATE))

    g_fin = final_norm.reshape(1, D_MODEL)
    y_p = _final_norm(x, g_fin, M_P, 1024, 0)
    y_s = _final_norm(x, g_fin, M_S, M_S, M_P // M_S)
    return (y_p.reshape(BATCH, SEQ, D_MODEL), y_s.reshape(M_S, 1, D_MODEL),
            jnp.stack(ret_p), jnp.stack(s5r_p), jnp.stack(s5i_p),
            ret_s, jnp.stack(s5r_s), jnp.stack(s5i_s))
```

```python
import functools
import math

import jax
import jax.numpy as jnp
from jax import lax
from jax.experimental import pallas as pl
from jax.experimental.pallas import tpu as pltpu

F32 = jnp.float32
BF16 = jnp.bfloat16

D_MODEL = 2048
BATCH = 4
SEQ = 2048
DEPTH = 4
DEC_BATCH = 128
PAST_LEN = 16384
RET_HEADS = 8
RET_DK = 128
RET_DV = 128
RET_QK = RET_HEADS * RET_DK
RET_W = RET_HEADS * RET_DV
RET_CHUNK = 128
ROPE_BASE = 10000.0
S5_GROUPS = 64
S5_GROUP_CH = 16
S5_W = S5_GROUPS * S5_GROUP_CH
S5_STATE = 64
D_FF = 5632
IN_W = 2 * RET_QK + 2 * RET_W + S5_W + 2 * D_MODEL
NORM_EPS = 1e-6
HEAD_NORM_EPS = 1e-5

M_P = BATCH * SEQ
M_S = DEC_BATCH
M_ALL = M_P + M_S

LANES = 128
SUBLANES = 8
VMEM_LIMIT = 58 * 1024 * 1024

TM = 1040
TN_FF = 512
TM_DOWN = 416
COL = 1024
TM_BIG = 4 * TM
TN_IN = 512
IN_SUB = 256

S5_TC = 64
S5_PAD = SUBLANES // 2
S5_PITCH = S5_TC + S5_PAD
S5_GB = 16
S5_NBLK = S5_GROUPS // S5_GB
S5_BLK_CH = S5_GB * S5_GROUP_CH
S5_BLK_ST = S5_GB * S5_STATE
S5_HALF_ST = S5_GROUPS * S5_STATE // 2
S5_NSLAB = S5_HALF_ST // LANES
S5_NSEQ = BATCH * 2
S5_SLAB_GROUP = 4


def _cparams(*sem):
    return pltpu.CompilerParams(dimension_semantics=sem, vmem_limit_bytes=VMEM_LIMIT)


def _rms_to_bf16(x, g):
    ms = jnp.mean(x * x, axis=-1, keepdims=True)
    return ((x * lax.rsqrt(ms + NORM_EPS)) * g).astype(BF16)


def _layer_spec(l, k, tn):
    return pl.BlockSpec((None, k, tn), lambda i, j: (l, 0, j))


def _layer_row_spec(l, n):
    return pl.BlockSpec((None, 1, n), lambda *_: (l, 0, 0))


def _norm_kernel(x_ref, g_ref, o_ref):
    o_ref[...] = _rms_to_bf16(x_ref[...], g_ref[...])


def _norm(x, g, l):
    return pl.pallas_call(
        _norm_kernel,
        grid=(M_ALL // TM,),
        in_specs=[pl.BlockSpec((TM, D_MODEL), lambda i: (i, 0)), _layer_row_spec(l, D_MODEL)],
        out_specs=pl.BlockSpec((TM, D_MODEL), lambda i: (i, 0)),
        out_shape=jax.ShapeDtypeStruct((M_ALL, D_MODEL), BF16),
        compiler_params=_cparams("parallel"),
        name="norm",
    )(x, g)


def _resident_rows_spec():
    return pl.BlockSpec((TM_BIG, D_MODEL), lambda r, j: (r, 0), pipeline_mode=pl.Buffered(1))


def _ffn_up_kernel(xn_ref, w1_ref, w3_ref, o_ref):
    w1 = w1_ref[...].astype(BF16)
    w3 = w3_ref[...].astype(BF16)
    for s in range(TM_BIG // TM):
        rows = slice(s * TM, (s + 1) * TM)
        xs = xn_ref[rows, :]
        h1 = jnp.dot(xs, w1, preferred_element_type=F32)
        h3 = jnp.dot(xs, w3, preferred_element_type=F32)
        o_ref[rows, :] = ((h1 * jax.nn.sigmoid(h1)) * h3).astype(BF16)


def _ffn_up(xn, w1, w3, l):
    return pl.pallas_call(
        _ffn_up_kernel,
        grid=(M_ALL // TM_BIG, D_FF // TN_FF),
        in_specs=[
            _resident_rows_spec(),
            _layer_spec(l, D_MODEL, TN_FF),
            _layer_spec(l, D_MODEL, TN_FF),
        ],
        out_specs=pl.BlockSpec((TM_BIG, TN_FF), lambda r, j: (r, j)),
        out_shape=jax.ShapeDtypeStruct((M_ALL, D_FF), BF16),
        compiler_params=_cparams("parallel", "arbitrary"),
        name="ffn_up",
    )(xn, w1, w3)


def _resident_layer_spec(l, shape):
    return pl.BlockSpec((None,) + shape, lambda i: (l, 0, 0), pipeline_mode=pl.Buffered(1))


def _proj_res_kernel(a_ref, w_ref, x_ref, *rest, scale):
    o_ref = rest[-2] if len(rest) == 3 else rest[0]
    for c in range(D_MODEL // COL):
        cols = slice(c * COL, (c + 1) * COL)
        acc = jnp.dot(a_ref[...], w_ref[:, cols], preferred_element_type=F32)
        o_ref[:, cols] = x_ref[:, cols] + scale * acc
    if len(rest) == 3:
        rest[2][...] = _rms_to_bf16(o_ref[...], rest[0][...])


def _proj_res(a, w16, x, l, *, scale, tm, name, gain=None, gain_layer=None):
    k = a.shape[1]
    row_spec = lambda: pl.BlockSpec((tm, D_MODEL), lambda i: (i, 0))
    in_specs = [pl.BlockSpec((tm, k), lambda i: (i, 0)), _resident_layer_spec(l, (k, D_MODEL)),
                row_spec()]
    args = [a, w16, x]
    out_specs = [row_spec()]
    out_shape = [jax.ShapeDtypeStruct((M_ALL, D_MODEL), F32)]
    if gain is not None:
        in_specs.append(_layer_row_spec(gain_layer, D_MODEL))
        args.append(gain)
        out_specs.append(row_spec())
        out_shape.append(jax.ShapeDtypeStruct((M_ALL, D_MODEL), BF16))
    return pl.pallas_call(
        functools.partial(_proj_res_kernel, scale=scale),
        grid=(M_ALL // tm,),
        in_specs=in_specs,
        out_specs=out_specs,
        out_shape=out_shape,
        compiler_params=_cparams("parallel"),
        name=name,
    )(*args)


def _inproj_kernel(xn_ref, w_ref, cos_ref, sin_ref, o_ref):
    seg = pl.program_id(1) // (COL // TN_IN)

    def project(epilogue):
        wb = [w_ref[:, c * IN_SUB:(c + 1) * IN_SUB].astype(BF16) for c in range(TN_IN // IN_SUB)]
        for s in range(TM_BIG // TM):
            rows = slice(s * TM, (s + 1) * TM)
            for c in range(TN_IN // IN_SUB):
                acc = jnp.dot(xn_ref[rows, :], wb[c], preferred_element_type=F32)
                o_ref[rows, c * IN_SUB:(c + 1) * IN_SUB] = epilogue(acc, rows)

    def rotary(acc, rows):
        scale = jnp.where(seg == 0, RET_DK ** -0.5, 1.0).astype(F32)
        cos = cos_ref[rows, :]
        sin = sin_ref[rows, :]
        heads = []
        for h in range(IN_SUB // RET_DK):
            a = acc[:, h * RET_DK:(h + 1) * RET_DK]
            heads.append((a * cos + pltpu.roll(a, RET_DK // 2, axis=1) * sin) * scale)
        return jnp.concatenate(heads, axis=1)

    @pl.when(seg <= 1)
    def _():
        project(rotary)

    @pl.when((seg == 2) | (seg == 4))
    def _():
        project(lambda acc, rows: acc)

    @pl.when(seg == 3)
    def _():
        project(lambda acc, rows: acc * jax.nn.sigmoid(acc))

    @pl.when(seg >= 5)
    def _():
        project(lambda acc, rows: jax.nn.sigmoid(acc))


def _inproj(xn, w, cos_t, sin_t, l):
    tab_spec = pl.BlockSpec((TM_BIG, RET_DK), lambda r, j: (r, 0), pipeline_mode=pl.Buffered(1))
    return pl.pallas_call(
        _inproj_kernel,
        grid=(M_ALL // TM_BIG, IN_W // TN_IN),
        in_specs=[
            _resident_rows_spec(),
            _layer_spec(l, D_MODEL, TN_IN),
            tab_spec, tab_spec,
        ],
        out_specs=pl.BlockSpec((TM_BIG, TN_IN), lambda r, j: (r, j)),
        out_shape=jax.ShapeDtypeStruct((M_ALL, IN_W), F32),
        compiler_params=_cparams("parallel", "arbitrary"),
        name="inproj",
    )(xn, w, cos_t, sin_t)


def _head_norm(o):
    mu = jnp.mean(o, axis=-1, keepdims=True)
    d = o - mu
    var = jnp.mean(d * d, axis=-1, keepdims=True)
    return d * lax.rsqrt(var + HEAD_NORM_EPS)


RET_RC = 4 * RET_CHUNK


def _ret_prompt_kernel(q_ref, k_ref, v_ref, rg_ref, intra_ref, qd_ref, kd_ref, cd_ref, gn_ref,
                       o_ref, s_ref, st_ref):
    step = pl.program_id(1)

    @pl.when(step == 0)
    def _():
        st_ref[...] = jnp.zeros_like(st_ref)

    nc = RET_RC // RET_CHUNK
    blk = lambda c, h: (slice(c * RET_CHUNK, (c + 1) * RET_CHUNK), slice(h * RET_DK, (h + 1) * RET_DK))

    o_intra, kv = {}, {}
    for c in range(nc):
        for h in range(RET_HEADS):
            qc = q_ref[blk(c, h)]
            kc = k_ref[blk(c, h)]
            vb = v_ref[blk(c, h)].astype(BF16)
            sc = lax.dot_general(qc.astype(BF16), kc.astype(BF16), (((1,), (1,)), ((), ())),
                                 preferred_element_type=F32) * intra_ref[h]
            o_intra[c, h] = jnp.dot(sc.astype(BF16), vb, preferred_element_type=F32)
            kt = jnp.transpose(kc * kd_ref[h]).astype(BF16)
            kv[c, h] = jnp.dot(kt, vb, preferred_element_type=F32)

    for h in range(RET_HEADS):
        s = st_ref[h]
        for c in range(nc):
            qs = jnp.dot((q_ref[blk(c, h)] * qd_ref[h]).astype(BF16), s.astype(BF16),
                         preferred_element_type=F32)
            o = o_intra[c, h] + qs
            s = s * cd_ref[h] + kv[c, h]
            on = (_head_norm(o) * gn_ref[h]) * rg_ref[blk(c, h)]
            o_ref[blk(c, h)] = on.astype(BF16)
        st_ref[h] = s

    @pl.when(step == pl.num_programs(1) - 1)
    def _():
        s_ref[...] = st_ref[...]


def _ret_prompt(proj, tabs, gn, l):
    intra, qd, kd, cd = tabs
    nsteps = SEQ // RET_RC
    tab_spec = pl.BlockSpec((RET_HEADS, RET_CHUNK, RET_CHUNK), lambda b, s: (0, 0, 0))

    def seg_spec(seg):
        return pl.BlockSpec((RET_RC, COL), lambda b, s, seg=seg: (b * nsteps + s, seg))

    return pl.pallas_call(
        _ret_prompt_kernel,
        grid=(BATCH, nsteps),
        in_specs=[
            seg_spec(0), seg_spec(1), seg_spec(2), seg_spec(3),
            tab_spec, tab_spec, tab_spec, tab_spec,
            pl.BlockSpec((None, RET_HEADS, 1, RET_DV), lambda b, s: (l, 0, 0, 0)),
        ],
        out_specs=[
            pl.BlockSpec((RET_RC, RET_W), lambda b, s: (b * nsteps + s, 0)),
            pl.BlockSpec((None, RET_HEADS, RET_DK, RET_DV), lambda b, s: (b, 0, 0, 0)),
        ],
        out_shape=[
            jax.ShapeDtypeStruct((M_ALL, RET_W), BF16),
            jax.ShapeDtypeStruct((BATCH, RET_HEADS, RET_DK, RET_DV), F32),
        ],
        scratch_shapes=[pltpu.VMEM((RET_HEADS, RET_DK, RET_DV), F32)],
        compiler_params=_cparams("parallel", "arbitrary"),
        name="ret_prompt",
    )(proj, proj, proj, proj, intra, qd, kd, cd, gn)


RET_SB = 8


def _col_bcast(row, eye, ones2):
    hi = row.astype(BF16)
    lo = (row - hi.astype(F32)).astype(BF16)
    parts = [jnp.where(eye, jnp.broadcast_to(t.astype(F32), (RET_DK, RET_DK)), 0.0).astype(BF16)
             for t in (hi, lo)]
    return jnp.dot(jnp.concatenate(parts, axis=1), ones2, preferred_element_type=F32)


def _ret_sample_kernel(q_ref, k_ref, v_ref, rg_ref, s_ref, g1_ref, gn_ref, *rest):
    o_ref, so_ref = rest[-2:]
    eye = (lax.broadcasted_iota(jnp.int32, (RET_DK, RET_DK), 0)
           == lax.broadcasted_iota(jnp.int32, (RET_DK, RET_DK), 1))
    ones2 = jnp.ones((2 * RET_DK, RET_DV), BF16)

    for b in range(RET_SB):
        row = slice(b, b + 1)
        for h in range(RET_HEADS):
            cols = slice(h * RET_DK, (h + 1) * RET_DK)
            kcol = _col_bcast(k_ref[row, cols], eye, ones2)
            qcol = _col_bcast(q_ref[row, cols], eye, ones2)
            s_new = s_ref[b, h] * g1_ref[h] + kcol * v_ref[row, cols]
            so_ref[b, h] = s_new
            o = jnp.sum(qcol * s_new, axis=0, keepdims=True)
            o_ref[row, cols] = (_head_norm(o) * gn_ref[h]) * rg_ref[row, cols]


def _ret_sample(proj, state, g1, gn, l, prev_states):
    row0 = M_P // RET_SB

    def seg_spec(seg):
        return pl.BlockSpec((RET_SB, COL), lambda i, seg=seg: (row0 + i, seg))

    st_spec = pl.BlockSpec((None, RET_SB, RET_HEADS, RET_DK, RET_DV), lambda i: (l, i, 0, 0, 0))
    in_specs = [
        seg_spec(0), seg_spec(1), seg_spec(2), seg_spec(3), st_spec,
        pl.BlockSpec((RET_HEADS, 1, RET_DV), lambda i: (0, 0, 0)),
        pl.BlockSpec((None, RET_HEADS, 1, RET_DV), lambda i: (l, 0, 0, 0)),
    ]
    args = [proj, proj, proj, proj, state, g1, gn]
    aliases = {}
    if prev_states is not None:
        in_specs.append(pl.BlockSpec(memory_space=pl.ANY))
        args.append(prev_states)
        aliases = {len(args) - 1: 1}
    return pl.pallas_call(
        _ret_sample_kernel,
        grid=(M_S // RET_SB,),
        in_specs=in_specs,
        out_specs=[pl.BlockSpec((RET_SB, RET_W), lambda i: (i, 0)), st_spec],
        out_shape=[
            jax.ShapeDtypeStruct((M_S, RET_W), F32),
            jax.ShapeDtypeStruct((DEPTH, M_S, RET_HEADS, RET_DK, RET_DV), F32),
        ],
        input_output_aliases=aliases,
        compiler_params=_cparams("parallel"),
        name="ret_sample",
    )(*args)


def _s5_disc_kernel(lr_ref, li_ref, ls_ref, br_ref, bi_ref, lbr_ref, lbi_ref, bbr_ref, bbi_ref):
    lam_r = lr_ref[...]
    lam_i = li_ref[...]
    dt = jnp.exp(ls_ref[...])
    ea = jnp.exp(lam_r * dt)
    lb_r = ea * jnp.cos(lam_i * dt)
    lb_i = ea * jnp.sin(lam_i * dt)
    lbr_ref[...] = lb_r
    lbi_ref[...] = lb_i
    nr = lb_r - 1.0
    den = lam_r * lam_r + lam_i * lam_i
    cr = (nr * lam_r + lb_i * lam_i) / den
    ci = (lb_i * lam_r - nr * lam_i) / den
    b_r = br_ref[...]
    b_i = bi_ref[...]
    bbr_ref[...] = cr * b_r - ci * b_i
    bbi_ref[...] = cr * b_i + ci * b_r


def _s5_discretize(lam_re, lam_im, log_step, b_re, b_im):
    rows = DEPTH * S5_GROUPS * S5_GROUP_CH
    rep = lambda a: jnp.repeat(a.reshape(DEPTH * S5_GROUPS, S5_STATE), S5_GROUP_CH, axis=0)
    tr = lambda a: jnp.transpose(a, (0, 1, 3, 2)).reshape(rows, S5_STATE)
    shp = jax.ShapeDtypeStruct((rows, S5_STATE), F32)
    ls = jnp.broadcast_to(log_step[:, :, None], (DEPTH, S5_GROUPS, S5_STATE))
    spec = pl.BlockSpec((rows // DEPTH, S5_STATE), lambda l: (l, 0))
    return pl.pallas_call(
        _s5_disc_kernel,
        grid=(DEPTH,),
        in_specs=[spec] * 5,
        out_specs=[spec] * 4,
        out_shape=[shp, shp, shp, shp],
        compiler_params=_cparams("parallel"),
        name="s5_disc",
    )(rep(lam_re), rep(lam_im), rep(ls), tr(b_re), tr(b_im))


def _gelu_tanh(y):
    c = math.sqrt(2.0 / math.pi)
    return 0.5 * y * (1.0 + jnp.tanh(c * (y + 0.044715 * (y * y * y))))


def _s5_prompt_kernel(u0_ref, u1_ref, u2_ref, u3_ref, w_ref, cw_ref, lamr_ref, lami_ref, d_ref,
                      z_ref, hre_ref, him_ref, bre_ref, bim_ref):
    j = pl.program_id(0)
    u_refs = (u0_ref, u1_ref, u2_ref, u3_ref)

    @pl.when(j == 0)
    def _():
        hre_ref[...] = jnp.zeros_like(hre_ref)
        him_ref[...] = jnp.zeros_like(him_ref)

    nsl = S5_BLK_ST // LANES
    zpad = jnp.zeros((S5_PAD, S5_BLK_CH), F32)

    for gh in range(2):
        seg = S5_TC + 2 * S5_PAD * gh
        for kb in range(2):
            blk = gh * 2 + kb
            c0 = blk * S5_BLK_CH
            parts = []
            for u in u_refs:
                ub = u[:, c0:c0 + S5_BLK_CH]
                parts.extend([zpad, ub, zpad] if gh else [ub])
            x = jnp.concatenate(parts, axis=0).astype(BF16)
            res = jnp.dot(x, w_ref[blk], preferred_element_type=F32)
            for b in range(BATCH):
                r0 = (b * 2 + gh) * S5_PITCH - S5_PAD * gh
                part = res[b * seg:(b + 1) * seg]
                for s in range(nsl):
                    slab = kb * nsl + s
                    bre_ref[slab, r0:r0 + seg, :] = part[:, s * LANES:(s + 1) * LANES]
                    bim_ref[slab, r0:r0 + seg, :] = part[:, S5_BLK_ST + s * LANES:S5_BLK_ST + (s + 1) * LANES]

    for sg in range(S5_NSLAB // S5_SLAB_GROUP):
        slabs = [sg * S5_SLAB_GROUP + i for i in range(S5_SLAB_GROUP)]
        lam_r = [lamr_ref[s] for s in slabs]
        lam_i = [lami_ref[s] for s in slabs]

        def step(t, carry, slabs=slabs, lam_r=lam_r, lam_i=lam_i):
            hr, hi = carry
            nr_all = []
            ni_all = []
            for i, s in enumerate(slabs):
                idx = pl.ds(t, S5_NSEQ, stride=S5_PITCH)
                br = bre_ref[s, idx, :]
                bi = bim_ref[s, idx, :]
                nr = lam_r[i] * hr[i] - lam_i[i] * hi[i] + br
                ni = lam_r[i] * hi[i] + lam_i[i] * hr[i] + bi
                bre_ref[s, idx, :] = nr
                bim_ref[s, idx, :] = ni
                nr_all.append(nr)
                ni_all.append(ni)
            return tuple(nr_all), tuple(ni_all)

        hr, hi = (tuple(hre_ref[s] for s in slabs), tuple(him_ref[s] for s in slabs))
        for t in range(S5_TC):
            hr, hi = step(t, (hr, hi))
        for i, s in enumerate(slabs):
            hre_ref[s] = hr[i]
            him_ref[s] = hi[i]

    for gh in range(2):
        seg = S5_TC + 2 * S5_PAD * gh
        for kb in range(2):
            blk = gh * 2 + kb
            c0 = blk * S5_BLK_CH
            pieces = []
            for src in (bre_ref, bim_ref):
                for s in range(nsl):
                    slab = kb * nsl + s
                    starts = [(b * 2 + gh) * S5_PITCH - S5_PAD * gh for b in range(BATCH)]
                    pieces.append(jnp.concatenate(
                        [src[slab, r0:r0 + seg, :] for r0 in starts], axis=0))
            hcat = jnp.concatenate(pieces, axis=1).astype(BF16)
            y = jnp.dot(hcat, cw_ref[blk], preferred_element_type=F32)
            dvec = d_ref[:, c0:c0 + S5_BLK_CH]
            for b in range(BATCH):
                lo = b * seg + S5_PAD * gh
                yb = y[lo:lo + S5_TC] + dvec * u_refs[b][:, c0:c0 + S5_BLK_CH]
                z_ref[b, :, c0:c0 + S5_BLK_CH] = _gelu_tanh(yb)


def _s5_prompt(proj, w_blk, cw_blk, lamr_t, lami_t, dvec, l):
    nsteps = SEQ // S5_TC
    su_col = (2 * RET_QK + 2 * RET_W) // S5_W

    def u_spec(b):
        return pl.BlockSpec((S5_TC, S5_W), lambda j, b=b: (b * nsteps + j, su_col))

    layer4 = lambda shape: pl.BlockSpec((None,) + shape, lambda j: (l, 0, 0, 0))
    full3 = lambda shape: pl.BlockSpec(shape, lambda j: (0, 0, 0))
    st_shape = (S5_NSLAB, S5_NSEQ, LANES)
    return pl.pallas_call(
        _s5_prompt_kernel,
        grid=(nsteps,),
        in_specs=[
            u_spec(0), u_spec(1), u_spec(2), u_spec(3),
            layer4((S5_NBLK, S5_BLK_CH, 2 * S5_BLK_ST)),
            layer4((S5_NBLK, 2 * S5_BLK_ST, S5_BLK_CH)),
            layer4(st_shape), layer4(st_shape),
            _layer_row_spec(l, S5_W),
        ],
        out_specs=[
            pl.BlockSpec((BATCH, S5_TC, S5_W), lambda j: (0, j, 0)),
            full3(st_shape), full3(st_shape),
        ],
        out_shape=[
            jax.ShapeDtypeStruct((BATCH + 1, SEQ, S5_W), F32),
            jax.ShapeDtypeStruct(st_shape, F32),
            jax.ShapeDtypeStruct(st_shape, F32),
        ],
        scratch_shapes=[
            pltpu.VMEM((S5_NSLAB, S5_NSEQ * S5_PITCH, LANES), F32),
            pltpu.VMEM((S5_NSLAB, S5_NSEQ * S5_PITCH, LANES), F32),
        ],
        compiler_params=_cparams("arbitrary"),
        name="s5_prompt",
    )(proj, proj, proj, proj, w_blk, cw_blk, lamr_t, lami_t, dvec)


def _s5_sample_kernel(u_ref, h0r_ref, h0i_ref, w_ref, cw_ref, lbr_ref, lbi_ref, d_ref,
                      z_ref, hr_ref, hi_ref):
    for blk in range(S5_NBLK):
        c0 = blk * S5_BLK_CH
        s0 = blk * S5_BLK_ST
        u = u_ref[:, c0:c0 + S5_BLK_CH]
        res = jnp.dot(u.astype(BF16), w_ref[blk], preferred_element_type=F32)
        lr = lbr_ref[:, s0:s0 + S5_BLK_ST]
        li = lbi_ref[:, s0:s0 + S5_BLK_ST]
        h0r = h0r_ref[:, s0:s0 + S5_BLK_ST]
        h0i = h0i_ref[:, s0:s0 + S5_BLK_ST]
        hr = lr * h0r - li * h0i + res[:, :S5_BLK_ST]
        hi = lr * h0i + li * h0r + res[:, S5_BLK_ST:]
        hr_ref[:, s0:s0 + S5_BLK_ST] = hr
        hi_ref[:, s0:s0 + S5_BLK_ST] = hi
        hcat = jnp.concatenate([hr, hi], axis=1).astype(BF16)
        y = jnp.dot(hcat, cw_ref[blk], preferred_element_type=F32) + d_ref[:, c0:c0 + S5_BLK_CH] * u
        z_ref[:, c0:c0 + S5_BLK_CH] = _gelu_tanh(y)


def _s5_sample(proj, h0r, h0i, w_blk, cw_blk, lbr, lbi, dvec, l):
    nst = S5_GROUPS * S5_STATE
    su_col = (2 * RET_QK + 2 * RET_W) // S5_W
    layer3 = lambda shape: pl.BlockSpec((None,) + shape, lambda i: (l, 0, 0))
    layer4 = lambda shape: pl.BlockSpec((None,) + shape, lambda i: (l, 0, 0, 0))
    whole = lambda shape: pl.BlockSpec(shape, lambda i: (0, 0))
    return pl.pallas_call(
        _s5_sample_kernel,
        grid=(1,),
        in_specs=[
            pl.BlockSpec((M_S, S5_W), lambda i: (M_P // M_S, su_col)),
            layer3((M_S, nst)), layer3((M_S, nst)),
            layer4((S5_NBLK, S5_BLK_CH, 2 * S5_BLK_ST)),
            layer4((S5_NBLK, 2 * S5_BLK_ST, S5_BLK_CH)),
            layer3((1, nst)), layer3((1, nst)),
            _layer_row_spec(l, S5_W),
        ],
        out_specs=[whole((M_S, S5_W)), whole((M_S, nst)), whole((M_S, nst))],
        out_shape=[
            jax.ShapeDtypeStruct((M_S, S5_W), F32),
            jax.ShapeDtypeStruct((M_S, nst), F32),
            jax.ShapeDtypeStruct((M_S, nst), F32),
        ],
        compiler_params=_cparams("arbitrary"),
        name="s5_sample",
    )(proj, h0r, h0i, w_blk, cw_blk, lbr, lbi, dvec)


TM_MERGE = 520


def _merge_kernel(z_ref, o_ref, gr0_ref, gr1_ref, gs0_ref, gs1_ref, gluw_ref, glub_ref, s5p_ref,
                  retp_ref, m_ref):
    z = z_ref[...]
    t = jnp.dot(z.astype(BF16), gluw_ref[...], preferred_element_type=F32) + glub_ref[...]
    z2 = (z * jax.nn.sigmoid(t)).astype(BF16)
    o = o_ref[...]
    for c, (gr_ref, gs_ref) in enumerate(((gr0_ref, gs0_ref), (gr1_ref, gs1_ref))):
        cols = slice(c * COL, (c + 1) * COL)
        b_s5 = jnp.dot(z2, s5p_ref[:, cols], preferred_element_type=F32)
        b_ret = jnp.dot(o, retp_ref[:, cols], preferred_element_type=F32)
        m_ref[:, cols] = (gr_ref[...] * b_ret + gs_ref[...] * b_s5).astype(BF16)


def _merge(z, o, proj, glu_w, glu_b, s5_proj, ret_proj, l):
    gr_col = (2 * RET_QK + 2 * RET_W + S5_W) // COL
    gs_col = gr_col + D_MODEL // COL
    gate = lambda col: pl.BlockSpec((TM_MERGE, COL), lambda i, col=col: (i, col))
    return pl.pallas_call(
        _merge_kernel,
        grid=(M_ALL // TM_MERGE,),
        in_specs=[
            pl.BlockSpec((TM_MERGE, S5_W), lambda i: (i, 0)),
            pl.BlockSpec((TM_MERGE, RET_W), lambda i: (i, 0)),
            gate(gr_col), gate(gr_col + 1), gate(gs_col), gate(gs_col + 1),
            _resident_layer_spec(l, (S5_W, S5_W)),
            _layer_row_spec(l, S5_W),
            _resident_layer_spec(l, (S5_W, D_MODEL)),
            _resident_layer_spec(l, (RET_W, D_MODEL)),
        ],
        out_specs=pl.BlockSpec((TM_MERGE, D_MODEL), lambda i: (i, 0)),
        out_shape=jax.ShapeDtypeStruct((M_ALL, D_MODEL), BF16),
        compiler_params=_cparams("parallel"),
        name="merge",
    )(z, o, proj, proj, proj, proj, glu_w, glu_b, s5_proj, ret_proj)


def _final_norm_kernel(x_ref, g_ref, o_ref):
    x = x_ref[...]
    ms = jnp.mean(x * x, axis=-1, keepdims=True)
    o_ref[...] = (x * lax.rsqrt(ms + NORM_EPS)) * g_ref[...]


def _final_norm(x, g, rows, tm, block0):
    return pl.pallas_call(
        _final_norm_kernel,
        grid=(rows // tm,),
        in_specs=[pl.BlockSpec((tm, D_MODEL), lambda i: (block0 + i, 0)),
                  pl.BlockSpec((1, D_MODEL), lambda i: (0, 0))],
        out_specs=pl.BlockSpec((tm, D_MODEL), lambda i: (i, 0)),
        out_shape=jax.ShapeDtypeStruct((rows, D_MODEL), F32),
        compiler_params=_cparams("parallel"),
        name="final_norm",
    )(x, g)


def _rope_tables():
    half = RET_DK // 2
    inv = ROPE_BASE ** (-jnp.arange(half, dtype=F32) / half)
    pos = jnp.concatenate([jnp.tile(jnp.arange(SEQ, dtype=jnp.int32), BATCH),
                           jnp.full((M_S,), PAST_LEN, jnp.int32)])
    ang = pos.astype(F32)[:, None] * inv[None, :]
    cos = jnp.cos(ang)
    sin = jnp.sin(ang)
    return jnp.concatenate([cos, cos], axis=1), jnp.concatenate([-sin, sin], axis=1)


def _retention_tables():
    c = RET_CHUNK
    log_g = jnp.log1p(-jnp.exp2(-5.0 - jnp.arange(RET_HEADS, dtype=F32)))
    i = jnp.arange(c, dtype=F32)
    diff = i[:, None] - i[None, :]
    intra = jnp.where(diff[None] >= 0,
                      jnp.exp(jnp.maximum(diff, 0.0)[None] * log_g[:, None, None]), 0.0)
    q_dec = jnp.exp((i + 1.0)[None, :] * log_g[:, None])
    k_dec = jnp.exp((c - 1.0 - i)[None, :] * log_g[:, None])
    c_dec = jnp.exp(c * log_g)
    full = lambda a: jnp.broadcast_to(a, (RET_HEADS, c, RET_DV))
    g1 = jnp.broadcast_to(jnp.exp(1.0 * log_g)[:, None, None], (RET_HEADS, 1, RET_DV))
    return (intra, full(q_dec[:, :, None]), full(k_dec[:, :, None]), full(c_dec[:, None, None])), g1


def _s5_block_weights(bbr_t, bbi_t, c_re, c_im):
    ch_group = jnp.arange(S5_BLK_CH, dtype=jnp.int32) // S5_GROUP_CH
    st_group = jnp.arange(S5_BLK_ST, dtype=jnp.int32) // S5_STATE
    drive_mask = ch_group[:, None] == st_group[None, :]

    def drive(bt):
        a = bt.reshape(DEPTH, S5_NBLK, S5_BLK_CH, 1, S5_STATE)
        a = jnp.broadcast_to(a, (DEPTH, S5_NBLK, S5_BLK_CH, S5_GB, S5_STATE))
        return jnp.where(drive_mask, a.reshape(DEPTH, S5_NBLK, S5_BLK_CH, S5_BLK_ST), 0.0)

    def readout(cm):
        a = cm.reshape(DEPTH, S5_NBLK, S5_GB, S5_GROUP_CH, S5_STATE)
        a = a.transpose(0, 1, 4, 2, 3).reshape(DEPTH, S5_NBLK, 1, S5_STATE, S5_BLK_CH)
        a = jnp.broadcast_to(a, (DEPTH, S5_NBLK, S5_GB, S5_STATE, S5_BLK_CH))
        return jnp.where(drive_mask.T, a.reshape(DEPTH, S5_NBLK, S5_BLK_ST, S5_BLK_CH), 0.0)

    w_blk = jnp.concatenate([drive(bbr_t), drive(bbi_t)], axis=3).astype(BF16)
    cw_blk = jnp.concatenate([readout(c_re), readout(-c_im)], axis=2).astype(BF16)
    return w_blk, cw_blk


def _slab_tiles(lb):
    a = lb.reshape(DEPTH, 2, S5_NSLAB, LANES).transpose(0, 2, 1, 3)
    return jnp.tile(a, (1, 1, BATCH, 1))


def kernel(x_prompt, x_sample, state_ret, state_s5_re, state_s5_im, ffn1_norm, ffn1_w1, ffn1_w3, ffn1_w2, mix_norm, w_in, ret_gn, ret_proj, s5_lam_re, s5_lam_im, s5_b_re, s5_b_im, s5_c_re, s5_c_im, s5_d, s5_log_step, glu_w, glu_b, s5_proj, w_out, ffn2_norm, ffn2_w1, ffn2_w3, ffn2_w2, final_norm):
    cos_t, sin_t = _rope_tables()
    ret_tabs, g1 = _retention_tables()
    x = jnp.concatenate([x_prompt.reshape(M_P, D_MODEL), x_sample.reshape(M_S, D_MODEL)], axis=0)

    ret_p, s5r_p, s5i_p, s5r_s, s5i_s = [], [], [], [], []
    ret_s = None
    nst = S5_GROUPS * S5_STATE
    rows3 = lambda a: a.reshape(DEPTH, 1, -1)
    ffn1_g, mix_g, ffn2_g, glu_b3 = rows3(ffn1_norm), rows3(mix_norm), rows3(ffn2_norm), rows3(glu_b)
    gn4 = ret_gn.reshape(DEPTH, RET_HEADS, 1, RET_DV)
    glu_w16, s5_proj16, ret_proj16, w_out16, ffn1_w2_16, ffn2_w2_16 = (
        a.astype(BF16) for a in (glu_w, s5_proj, ret_proj, w_out, ffn1_w2, ffn2_w2))

    lbr_x, lbi_x, bbr_t, bbi_t = _s5_discretize(s5_lam_re, s5_lam_im, s5_log_step, s5_b_re, s5_b_im)
    lbr = lbr_x[::S5_GROUP_CH].reshape(DEPTH, nst)
    lbi = lbi_x[::S5_GROUP_CH].reshape(DEPTH, nst)
    w_blk, cw_blk = _s5_block_weights(bbr_t, bbi_t, s5_c_re, s5_c_im)
    lamr_t, lami_t = _slab_tiles(lbr), _slab_tiles(lbi)
    lbr3, lbi3 = lbr.reshape(DEPTH, 1, nst), lbi.reshape(DEPTH, 1, nst)
    dvec3 = s5_d.reshape(DEPTH, 1, S5_W)
    h0r_all = state_s5_re.reshape(DEPTH, M_S, nst)
    h0i_all = state_s5_im.reshape(DEPTH, M_S, nst)

    xn = _norm(x, ffn1_g, 0)
    for l in range(DEPTH):
        hid = _ffn_up(xn, ffn1_w1, ffn1_w3, l)
        x, xn = _proj_res(hid, ffn1_w2_16, x, l, scale=0.5, tm=TM_DOWN, name="ffn_down",
                          gain=mix_g, gain_layer=l)

        proj = _inproj(xn, w_in, cos_t, sin_t, l)

        o_all, s_p = _ret_prompt(proj, ret_tabs, gn4, l)
        o_s, ret_s = _ret_sample(proj, state_ret, g1, gn4, l, ret_s)
        o_all = lax.dynamic_update_slice(o_all, o_s.astype(BF16), (M_P, 0))

        z5, hre_p, him_p = _s5_prompt(proj, w_blk, cw_blk, lamr_t, lami_t, dvec3, l)
        z_s, hre_s, him_s = _s5_sample(proj, h0r_all, h0i_all, w_blk, cw_blk, lbr3, lbi3, dvec3, l)
        z = lax.dynamic_update_slice(z5.reshape((BATCH + 1) * SEQ, S5_W), z_s, (M_P, 0))

        m = _merge(z, o_all, proj, glu_w16, glu_b3, s5_proj16, ret_proj16, l)
        x, xn = _proj_res(m, w_out16, x, l, scale=1.0, tm=TM_MERGE, name="out_proj",
                          gain=ffn2_g, gain_layer=l)

        hid = _ffn_up(xn, ffn2_w1, ffn2_w3, l)
        if l + 1 < DEPTH:
            x, xn = _proj_res(hid, ffn2_w2_16, x, l, scale=0.5, tm=TM_DOWN, name="ffn_down",
                              gain=ffn1_g, gain_layer=l + 1)
        else:
            x, = _proj_res(hid, ffn2_w2_16, x, l, scale=0.5, tm=TM_DOWN, name="ffn_down")

        unslab = lambda a: (a.transpose(1, 0, 2).reshape(BATCH, 2, S5_HALF_ST)
                            .reshape(BATCH, S5_GROUPS, S5_STATE))
        ret_p.append(s_p)
        s5r_p.append(unslab(hre_p))
        s5i_p.append(unslab(him_p))
        s5r_s.append(hre_s.reshape(M_S, S5_GROUPS, S5_STATE))
        s5i_s.append(him_s.reshape(M_S, S5_GROUPS, S5_STATE))

    g_fin = final_norm.reshape(1, D_MODEL)
    y_p = _final_norm(x, g_fin, M_P, 1024, 0)
    y_s = _final_norm(x, g_fin, M_S, M_S, M_P // M_S)
    return (y_p.reshape(BATCH, SEQ, D_MODEL), y_s.reshape(M_S, 1, D_MODEL),
            jnp.stack(ret_p), jnp.stack(s5r_p), jnp.stack(s5i_p),
            ret_s, jnp.stack(s5r_s), jnp.stack(s5i_s))
```

```python
import functools
import math

import jax
import jax.numpy as jnp
from jax import lax
from jax.experimental import pallas as pl
from jax.experimental.pallas import tpu as pltpu

F32 = jnp.float32
BF16 = jnp.bfloat16

D_MODEL = 2048
BATCH = 4
SEQ = 2048
DEPTH = 4
DEC_BATCH = 128
PAST_LEN = 16384
RET_HEADS = 8
RET_DK = 128
RET_DV = 128
RET_QK = RET_HEADS * RET_DK
RET_W = RET_HEADS * RET_DV
RET_CHUNK = 128
ROPE_BASE = 10000.0
S5_GROUPS = 64
S5_GROUP_CH = 16
S5_W = S5_GROUPS * S5_GROUP_CH
S5_STATE = 64
D_FF = 5632
IN_W = 2 * RET_QK + 2 * RET_W + S5_W + 2 * D_MODEL
NORM_EPS = 1e-6
HEAD_NORM_EPS = 1e-5

M_P = BATCH * SEQ
M_S = DEC_BATCH
M_ALL = M_P + M_S

LANES = 128
SUBLANES = 8
VMEM_LIMIT = 58 * 1024 * 1024

TM = 1040
TN_FF = 512
TM_DOWN = 416
COL = 1024
TM_BIG = 4 * TM
TN_IN = 512
IN_SUB = 256

S5_TC = 64
S5_PAD = SUBLANES // 2
S5_PITCH = S5_TC + S5_PAD
S5_GB = 16
S5_NBLK = S5_GROUPS // S5_GB
S5_BLK_CH = S5_GB * S5_GROUP_CH
S5_BLK_ST = S5_GB * S5_STATE
S5_HALF_ST = S5_GROUPS * S5_STATE // 2
S5_NSLAB = S5_HALF_ST // LANES
S5_NSEQ = BATCH * 2
S5_SLAB_GROUP = 4


def _cparams(*sem):
    return pltpu.CompilerParams(dimension_semantics=sem, vmem_limit_bytes=VMEM_LIMIT)


def _rms_to_bf16(x, g):
    ms = jnp.mean(x * x, axis=-1, keepdims=True)
    return ((x * lax.rsqrt(ms + NORM_EPS)) * g).astype(BF16)


def _layer_spec(l, k, tn):
    return pl.BlockSpec((None, k, tn), lambda i, j: (l, 0, j))


def _layer_row_spec(l, n):
    return pl.BlockSpec((None, 1, n), lambda *_: (l, 0, 0))


def _norm_kernel(x_ref, g_ref, o_ref):
    o_ref[...] = _rms_to_bf16(x_ref[...], g_ref[...])


def _norm(x, g, l):
    return pl.pallas_call(
        _norm_kernel,
        grid=(M_ALL // TM,),
        in_specs=[pl.BlockSpec((TM, D_MODEL), lambda i: (i, 0)), _layer_row_spec(l, D_MODEL)],
        out_specs=pl.BlockSpec((TM, D_MODEL), lambda i: (i, 0)),
        out_shape=jax.ShapeDtypeStruct((M_ALL, D_MODEL), BF16),
        compiler_params=_cparams("parallel"),
        name="norm",
    )(x, g)


def _resident_rows_spec():
    return pl.BlockSpec((TM_BIG, D_MODEL), lambda r, j: (r, 0), pipeline_mode=pl.Buffered(1))


def _ffn_up_kernel(xn_ref, w1_ref, w3_ref, w2_ref, o_ref, w2_16_ref):
    w1 = w1_ref[...].astype(BF16)
    w3 = w3_ref[...].astype(BF16)
    for s in range(TM_BIG // TM):
        rows = slice(s * TM, (s + 1) * TM)
        xs = xn_ref[rows, :]
        h1 = jnp.dot(xs, w1, preferred_element_type=F32)
        h3 = jnp.dot(xs, w3, preferred_element_type=F32)
        o_ref[rows, :] = ((h1 * jax.nn.sigmoid(h1)) * h3).astype(BF16)
    w2_16_ref[...] = w2_ref[...].astype(BF16)


def _ffn_up(xn, w1, w3, w2, l):
    ncol = D_FF // TN_FF
    nsteps = (M_ALL // TM_BIG) * ncol
    chunk = D_FF // nsteps
    return pl.pallas_call(
        _ffn_up_kernel,
        grid=(M_ALL // TM_BIG, ncol),
        in_specs=[
            _resident_rows_spec(),
            _layer_spec(l, D_MODEL, TN_FF),
            _layer_spec(l, D_MODEL, TN_FF),
            pl.BlockSpec((None, chunk, D_MODEL), lambda r, j: (l, r * ncol + j, 0)),
        ],
        out_specs=[
            pl.BlockSpec((TM_BIG, TN_FF), lambda r, j: (r, j)),
            pl.BlockSpec((None, chunk, D_MODEL), lambda r, j: (0, r * ncol + j, 0)),
        ],
        out_shape=[
            jax.ShapeDtypeStruct((M_ALL, D_FF), BF16),
            jax.ShapeDtypeStruct((1, D_FF, D_MODEL), BF16),
        ],
        compiler_params=_cparams("parallel", "arbitrary"),
        name="ffn_up",
    )(xn, w1, w3, w2)


def _resident_layer_spec(l, shape):
    return pl.BlockSpec((None,) + shape, lambda i: (l, 0, 0), pipeline_mode=pl.Buffered(1))


def _proj_res_kernel(a_ref, w_ref, x_ref, *rest, scale):
    o_ref = rest[-2] if len(rest) == 3 else rest[0]
    for c in range(D_MODEL // COL):
        cols = slice(c * COL, (c + 1) * COL)
        acc = jnp.dot(a_ref[...], w_ref[:, cols], preferred_element_type=F32)
        o_ref[:, cols] = x_ref[:, cols] + scale * acc
    if len(rest) == 3:
        rest[2][...] = _rms_to_bf16(o_ref[...], rest[0][...])


def _proj_res(a, w16, x, l, *, scale, tm, name, gain=None, gain_layer=None):
    k = a.shape[1]
    row_spec = lambda: pl.BlockSpec((tm, D_MODEL), lambda i: (i, 0))
    in_specs = [pl.BlockSpec((tm, k), lambda i: (i, 0)), _resident_layer_spec(l, (k, D_MODEL)),
                row_spec()]
    args = [a, w16, x]
    out_specs = [row_spec()]
    out_shape = [jax.ShapeDtypeStruct((M_ALL, D_MODEL), F32)]
    if gain is not None:
        in_specs.append(_layer_row_spec(gain_layer, D_MODEL))
        args.append(gain)
        out_specs.append(row_spec())
        out_shape.append(jax.ShapeDtypeStruct((M_ALL, D_MODEL), BF16))
    return pl.pallas_call(
        functools.partial(_proj_res_kernel, scale=scale),
        grid=(M_ALL // tm,),
        in_specs=in_specs,
        out_specs=out_specs,
        out_shape=out_shape,
        compiler_params=_cparams("parallel"),
        name=name,
    )(*args)


def _inproj_kernel(xn_ref, w_ref, cos_ref, sin_ref, *rest):
    o_ref = rest[len(rest) // 2]
    for src, dst in zip(rest[:len(rest) // 2], rest[len(rest) // 2 + 1:]):
        dst[...] = src[...].astype(BF16)

    seg = pl.program_id(1) // (COL // TN_IN)

    def project(epilogue):
        wb = [w_ref[:, c * IN_SUB:(c + 1) * IN_SUB].astype(BF16) for c in range(TN_IN // IN_SUB)]
        for s in range(TM_BIG // TM):
            rows = slice(s * TM, (s + 1) * TM)
            for c in range(TN_IN // IN_SUB):
                acc = jnp.dot(xn_ref[rows, :], wb[c], preferred_element_type=F32)
                o_ref[rows, c * IN_SUB:(c + 1) * IN_SUB] = epilogue(acc, rows)

    def rotary(acc, rows):
        scale = jnp.where(seg == 0, RET_DK ** -0.5, 1.0).astype(F32)
        cos = cos_ref[rows, :]
        sin = sin_ref[rows, :]
        heads = []
        for h in range(IN_SUB // RET_DK):
            a = acc[:, h * RET_DK:(h + 1) * RET_DK]
            heads.append((a * cos + pltpu.roll(a, RET_DK // 2, axis=1) * sin) * scale)
        return jnp.concatenate(heads, axis=1)

    @pl.when(seg <= 1)
    def _():
        project(rotary)

    @pl.when((seg == 2) | (seg == 4))
    def _():
        project(lambda acc, rows: acc)

    @pl.when(seg == 3)
    def _():
        project(lambda acc, rows: acc * jax.nn.sigmoid(acc))

    @pl.when(seg >= 5)
    def _():
        project(lambda acc, rows: jax.nn.sigmoid(acc))


CAST_STEPS = 32


def _inproj(xn, w, cos_t, sin_t, small_weights, l):
    ncol = IN_W // TN_IN
    tab_spec = pl.BlockSpec((TM_BIG, RET_DK), lambda r, j: (r, 0), pipeline_mode=pl.Buffered(1))
    chunk_idx = lambda r, j: jnp.minimum(r * ncol + j, CAST_STEPS - 1)
    cast_in, cast_out, cast_shapes = [], [], []
    for a in small_weights:
        rows, cols = a.shape[1], a.shape[2]
        blk = (None, rows // CAST_STEPS, cols)
        cast_in.append(pl.BlockSpec(blk, lambda r, j: (l, chunk_idx(r, j), 0)))
        cast_out.append(pl.BlockSpec(blk, lambda r, j: (0, chunk_idx(r, j), 0)))
        cast_shapes.append(jax.ShapeDtypeStruct((1, rows, cols), BF16))
    return pl.pallas_call(
        _inproj_kernel,
        grid=(M_ALL // TM_BIG, ncol),
        in_specs=[
            _resident_rows_spec(),
            _layer_spec(l, D_MODEL, TN_IN),
            tab_spec, tab_spec,
        ] + cast_in,
        out_specs=[pl.BlockSpec((TM_BIG, TN_IN), lambda r, j: (r, j))] + cast_out,
        out_shape=[jax.ShapeDtypeStruct((M_ALL, IN_W), F32)] + cast_shapes,
        compiler_params=_cparams("parallel", "arbitrary"),
        name="inproj",
    )(xn, w, cos_t, sin_t, *small_weights)


def _head_norm(o):
    mu = jnp.mean(o, axis=-1, keepdims=True)
    d = o - mu
    var = jnp.mean(d * d, axis=-1, keepdims=True)
    return d * lax.rsqrt(var + HEAD_NORM_EPS)


RET_RC = 4 * RET_CHUNK


def _ret_prompt_kernel(q_ref, k_ref, v_ref, rg_ref, intra_ref, qd_ref, kd_ref, cd_ref, gn_ref,
                       o_ref, s_ref, st_ref):
    step = pl.program_id(1)

    @pl.when(step == 0)
    def _():
        st_ref[...] = jnp.zeros_like(st_ref)

    nc = RET_RC // RET_CHUNK
    blk = lambda c, h: (slice(c * RET_CHUNK, (c + 1) * RET_CHUNK), slice(h * RET_DK, (h + 1) * RET_DK))

    o_intra, kv = {}, {}
    for c in range(nc):
        for h in range(RET_HEADS):
            qc = q_ref[blk(c, h)]
            kc = k_ref[blk(c, h)]
            vb = v_ref[blk(c, h)].astype(BF16)
            sc = lax.dot_general(qc.astype(BF16), kc.astype(BF16), (((1,), (1,)), ((), ())),
                                 preferred_element_type=F32) * intra_ref[h]
            o_intra[c, h] = jnp.dot(sc.astype(BF16), vb, preferred_element_type=F32)
            kt = jnp.transpose(kc * kd_ref[h]).astype(BF16)
            kv[c, h] = jnp.dot(kt, vb, preferred_element_type=F32)

    for h in range(RET_HEADS):
        s = st_ref[h]
        for c in range(nc):
            qs = jnp.dot((q_ref[blk(c, h)] * qd_ref[h]).astype(BF16), s.astype(BF16),
                         preferred_element_type=F32)
            o = o_intra[c, h] + qs
            s = s * cd_ref[h] + kv[c, h]
            on = (_head_norm(o) * gn_ref[h]) * rg_ref[blk(c, h)]
            o_ref[blk(c, h)] = on.astype(BF16)
        st_ref[h] = s

    @pl.when(step == pl.num_programs(1) - 1)
    def _():
        s_ref[...] = st_ref[...]


def _ret_prompt(proj, tabs, gn, l):
    intra, qd, kd, cd = tabs
    nsteps = SEQ // RET_RC
    tab_spec = pl.BlockSpec((RET_HEADS, RET_CHUNK, RET_CHUNK), lambda b, s: (0, 0, 0))

    def seg_spec(seg):
        return pl.BlockSpec((RET_RC, COL), lambda b, s, seg=seg: (b * nsteps + s, seg))

    return pl.pallas_call(
        _ret_prompt_kernel,
        grid=(BATCH, nsteps),
        in_specs=[
            seg_spec(0), seg_spec(1), seg_spec(2), seg_spec(3),
            tab_spec, tab_spec, tab_spec, tab_spec,
            pl.BlockSpec((None, RET_HEADS, 1, RET_DV), lambda b, s: (l, 0, 0, 0)),
        ],
        out_specs=[
            pl.BlockSpec((RET_RC, RET_W), lambda b, s: (b * nsteps + s, 0)),
            pl.BlockSpec((None, RET_HEADS, RET_DK, RET_DV), lambda b, s: (b, 0, 0, 0)),
        ],
        out_shape=[
            jax.ShapeDtypeStruct((M_ALL, RET_W), BF16),
            jax.ShapeDtypeStruct((BATCH, RET_HEADS, RET_DK, RET_DV), F32),
        ],
        scratch_shapes=[pltpu.VMEM((RET_HEADS, RET_DK, RET_DV), F32)],
        compiler_params=_cparams("parallel", "arbitrary"),
        name="ret_prompt",
    )(proj, proj, proj, proj, intra, qd, kd, cd, gn)


RET_SB = 8


def _col_bcast(row, eye, ones2):
    hi = row.astype(BF16)
    lo = (row - hi.astype(F32)).astype(BF16)
    parts = [jnp.where(eye, jnp.broadcast_to(t.astype(F32), (RET_DK, RET_DK)), 0.0).astype(BF16)
             for t in (hi, lo)]
    return jnp.dot(jnp.concatenate(parts, axis=1), ones2, preferred_element_type=F32)


def _ret_sample_kernel(q_ref, k_ref, v_ref, rg_ref, s_ref, g1_ref, gn_ref, *rest):
    o_ref, so_ref = rest[-2:]
    eye = (lax.broadcasted_iota(jnp.int32, (RET_DK, RET_DK), 0)
           == lax.broadcasted_iota(jnp.int32, (RET_DK, RET_DK), 1))
    ones2 = jnp.ones((2 * RET_DK, RET_DV), BF16)

    for b in range(RET_SB):
        row = slice(b, b + 1)
        for h in range(RET_HEADS):
            cols = slice(h * RET_DK, (h + 1) * RET_DK)
            kcol = _col_bcast(k_ref[row, cols], eye, ones2)
            qcol = _col_bcast(q_ref[row, cols], eye, ones2)
            s_new = s_ref[b, h] * g1_ref[h] + kcol * v_ref[row, cols]
            so_ref[b, h] = s_new
            o = jnp.sum(qcol * s_new, axis=0, keepdims=True)
            o_ref[row, cols] = (_head_norm(o) * gn_ref[h]) * rg_ref[row, cols]


def _ret_sample(proj, state, g1, gn, l, prev_states):
    row0 = M_P // RET_SB

    def seg_spec(seg):
        return pl.BlockSpec((RET_SB, COL), lambda i, seg=seg: (row0 + i, seg))

    st_spec = pl.BlockSpec((None, RET_SB, RET_HEADS, RET_DK, RET_DV), lambda i: (l, i, 0, 0, 0))
    in_specs = [
        seg_spec(0), seg_spec(1), seg_spec(2), seg_spec(3), st_spec,
        pl.BlockSpec((RET_HEADS, 1, RET_DV), lambda i: (0, 0, 0)),
        pl.BlockSpec((None, RET_HEADS, 1, RET_DV), lambda i: (l, 0, 0, 0)),
    ]
    args = [proj, proj, proj, proj, state, g1, gn]
    aliases = {}
    if prev_states is not None:
        in_specs.append(pl.BlockSpec(memory_space=pl.ANY))
        args.append(prev_states)
        aliases = {len(args) - 1: 1}
    return pl.pallas_call(
        _ret_sample_kernel,
        grid=(M_S // RET_SB,),
        in_specs=in_specs,
        out_specs=[pl.BlockSpec((RET_SB, RET_W), lambda i: (i, 0)), st_spec],
        out_shape=[
            jax.ShapeDtypeStruct((M_S, RET_W), F32),
            jax.ShapeDtypeStruct((DEPTH, M_S, RET_HEADS, RET_DK, RET_DV), F32),
        ],
        input_output_aliases=aliases,
        compiler_params=_cparams("parallel"),
        name="ret_sample",
    )(*args)


def _s5_disc_kernel(lr_ref, li_ref, ls_ref, br_ref, bi_ref, lbr_ref, lbi_ref, bbr_ref, bbi_ref):
    lam_r = lr_ref[...]
    lam_i = li_ref[...]
    dt = jnp.exp(ls_ref[...])
    ea = jnp.exp(lam_r * dt)
    lb_r = ea * jnp.cos(lam_i * dt)
    lb_i = ea * jnp.sin(lam_i * dt)
    lbr_ref[...] = lb_r
    lbi_ref[...] = lb_i
    nr = lb_r - 1.0
    den = lam_r * lam_r + lam_i * lam_i
    cr = (nr * lam_r + lb_i * lam_i) / den
    ci = (lb_i * lam_r - nr * lam_i) / den
    b_r = br_ref[...]
    b_i = bi_ref[...]
    bbr_ref[...] = cr * b_r - ci * b_i
    bbi_ref[...] = cr * b_i + ci * b_r


def _s5_discretize(lam_re, lam_im, log_step, b_re, b_im):
    rows = DEPTH * S5_GROUPS * S5_GROUP_CH
    rep = lambda a: jnp.repeat(a.reshape(DEPTH * S5_GROUPS, S5_STATE), S5_GROUP_CH, axis=0)
    tr = lambda a: jnp.transpose(a, (0, 1, 3, 2)).reshape(rows, S5_STATE)
    shp = jax.ShapeDtypeStruct((rows, S5_STATE), F32)
    ls = jnp.broadcast_to(log_step[:, :, None], (DEPTH, S5_GROUPS, S5_STATE))
    spec = pl.BlockSpec((rows // DEPTH, S5_STATE), lambda l: (l, 0))
    return pl.pallas_call(
        _s5_disc_kernel,
        grid=(DEPTH,),
        in_specs=[spec] * 5,
        out_specs=[spec] * 4,
        out_shape=[shp, shp, shp, shp],
        compiler_params=_cparams("parallel"),
        name="s5_disc",
    )(rep(lam_re), rep(lam_im), rep(ls), tr(b_re), tr(b_im))


def _gelu_tanh(y):
    c = math.sqrt(2.0 / math.pi)
    return 0.5 * y * (1.0 + jnp.tanh(c * (y + 0.044715 * (y * y * y))))


def _s5_prompt_kernel(u0_ref, u1_ref, u2_ref, u3_ref, w_ref, cw_ref, lamr_ref, lami_ref, d_ref,
                      z_ref, hre_ref, him_ref, bre_ref, bim_ref):
    j = pl.program_id(0)
    u_refs = (u0_ref, u1_ref, u2_ref, u3_ref)

    @pl.when(j == 0)
    def _():
        hre_ref[...] = jnp.zeros_like(hre_ref)
        him_ref[...] = jnp.zeros_like(him_ref)

    nsl = S5_BLK_ST // LANES
    zpad = jnp.zeros((S5_PAD, S5_BLK_CH), F32)

    for gh in range(2):
        seg = S5_TC + 2 * S5_PAD * gh
        for kb in range(2):
            blk = gh * 2 + kb
            c0 = blk * S5_BLK_CH
            parts = []
            for u in u_refs:
                ub = u[:, c0:c0 + S5_BLK_CH]
                parts.extend([zpad, ub, zpad] if gh else [ub])
            x = jnp.concatenate(parts, axis=0).astype(BF16)
            res = jnp.dot(x, w_ref[blk], preferred_element_type=F32)
            for b in range(BATCH):
                r0 = (b * 2 + gh) * S5_PITCH - S5_PAD * gh
                part = res[b * seg:(b + 1) * seg]
                for s in range(nsl):
                    slab = kb * nsl + s
                    bre_ref[slab, r0:r0 + seg, :] = part[:, s * LANES:(s + 1) * LANES]
                    bim_ref[slab, r0:r0 + seg, :] = part[:, S5_BLK_ST + s * LANES:S5_BLK_ST + (s + 1) * LANES]

    for sg in range(S5_NSLAB // S5_SLAB_GROUP):
        slabs = [sg * S5_SLAB_GROUP + i for i in range(S5_SLAB_GROUP)]
        lam_r = [lamr_ref[s] for s in slabs]
        lam_i = [lami_ref[s] for s in slabs]

        def step(t, carry, slabs=slabs, lam_r=lam_r, lam_i=lam_i):
            hr, hi = carry
            nr_all = []
            ni_all = []
            for i, s in enumerate(slabs):
                idx = pl.ds(t, S5_NSEQ, stride=S5_PITCH)
                br = bre_ref[s, idx, :]
                bi = bim_ref[s, idx, :]
                nr = lam_r[i] * hr[i] - lam_i[i] * hi[i] + br
                ni = lam_r[i] * hi[i] + lam_i[i] * hr[i] + bi
                bre_ref[s, idx, :] = nr
                bim_ref[s, idx, :] = ni
                nr_all.append(nr)
                ni_all.append(ni)
            return tuple(nr_all), tuple(ni_all)

        hr, hi = (tuple(hre_ref[s] for s in slabs), tuple(him_ref[s] for s in slabs))
        for t in range(S5_TC):
            hr, hi = step(t, (hr, hi))
        for i, s in enumerate(slabs):
            hre_ref[s] = hr[i]
            him_ref[s] = hi[i]

    for gh in range(2):
        seg = S5_TC + 2 * S5_PAD * gh
        for kb in range(2):
            blk = gh * 2 + kb
            c0 = blk * S5_BLK_CH
            pieces = []
            for src in (bre_ref, bim_ref):
                for s in range(nsl):
                    slab = kb * nsl + s
                    starts = [(b * 2 + gh) * S5_PITCH - S5_PAD * gh for b in range(BATCH)]
                    pieces.append(jnp.concatenate(
                        [src[slab, r0:r0 + seg, :] for r0 in starts], axis=0))
            hcat = jnp.concatenate(pieces, axis=1).astype(BF16)
            y = jnp.dot(hcat, cw_ref[blk], preferred_element_type=F32)
            dvec = d_ref[:, c0:c0 + S5_BLK_CH]
            for b in range(BATCH):
                lo = b * seg + S5_PAD * gh
                yb = y[lo:lo + S5_TC] + dvec * u_refs[b][:, c0:c0 + S5_BLK_CH]
                z_ref[b, :, c0:c0 + S5_BLK_CH] = _gelu_tanh(yb)


def _s5_prompt(proj, w_blk, cw_blk, lamr_t, lami_t, dvec, l):
    nsteps = SEQ // S5_TC
    su_col = (2 * RET_QK + 2 * RET_W) // S5_W

    def u_spec(b):
        return pl.BlockSpec((S5_TC, S5_W), lambda j, b=b: (b * nsteps + j, su_col))

    layer4 = lambda shape: pl.BlockSpec((None,) + shape, lambda j: (l, 0, 0, 0))
    full3 = lambda shape: pl.BlockSpec(shape, lambda j: (0, 0, 0))
    st_shape = (S5_NSLAB, S5_NSEQ, LANES)
    return pl.pallas_call(
        _s5_prompt_kernel,
        grid=(nsteps,),
        in_specs=[
            u_spec(0), u_spec(1), u_spec(2), u_spec(3),
            layer4((S5_NBLK, S5_BLK_CH, 2 * S5_BLK_ST)),
            layer4((S5_NBLK, 2 * S5_BLK_ST, S5_BLK_CH)),
            layer4(st_shape), layer4(st_shape),
            _layer_row_spec(l, S5_W),
        ],
        out_specs=[
            pl.BlockSpec((BATCH, S5_TC, S5_W), lambda j: (0, j, 0)),
            full3(st_shape), full3(st_shape),
        ],
        out_shape=[
            jax.ShapeDtypeStruct((BATCH + 1, SEQ, S5_W), F32),
            jax.ShapeDtypeStruct(st_shape, F32),
            jax.ShapeDtypeStruct(st_shape, F32),
        ],
        scratch_shapes=[
            pltpu.VMEM((S5_NSLAB, S5_NSEQ * S5_PITCH, LANES), F32),
            pltpu.VMEM((S5_NSLAB, S5_NSEQ * S5_PITCH, LANES), F32),
        ],
        compiler_params=_cparams("arbitrary"),
        name="s5_prompt",
    )(proj, proj, proj, proj, w_blk, cw_blk, lamr_t, lami_t, dvec)


def _s5_sample_kernel(u_ref, h0r_ref, h0i_ref, w_ref, cw_ref, lbr_ref, lbi_ref, d_ref,
                      z_ref, hr_ref, hi_ref):
    for blk in range(S5_NBLK):
        c0 = blk * S5_BLK_CH
        s0 = blk * S5_BLK_ST
        u = u_ref[:, c0:c0 + S5_BLK_CH]
        res = jnp.dot(u.astype(BF16), w_ref[blk], preferred_element_type=F32)
        lr = lbr_ref[:, s0:s0 + S5_BLK_ST]
        li = lbi_ref[:, s0:s0 + S5_BLK_ST]
        h0r = h0r_ref[:, s0:s0 + S5_BLK_ST]
        h0i = h0i_ref[:, s0:s0 + S5_BLK_ST]
        hr = lr * h0r - li * h0i + res[:, :S5_BLK_ST]
        hi = lr * h0i + li * h0r + res[:, S5_BLK_ST:]
        hr_ref[:, s0:s0 + S5_BLK_ST] = hr
        hi_ref[:, s0:s0 + S5_BLK_ST] = hi
        hcat = jnp.concatenate([hr, hi], axis=1).astype(BF16)
        y = jnp.dot(hcat, cw_ref[blk], preferred_element_type=F32) + d_ref[:, c0:c0 + S5_BLK_CH] * u
        z_ref[:, c0:c0 + S5_BLK_CH] = _gelu_tanh(y)


def _s5_sample(proj, h0r, h0i, w_blk, cw_blk, lbr, lbi, dvec, l):
    nst = S5_GROUPS * S5_STATE
    su_col = (2 * RET_QK + 2 * RET_W) // S5_W
    layer3 = lambda shape: pl.BlockSpec((None,) + shape, lambda i: (l, 0, 0))
    layer4 = lambda shape: pl.BlockSpec((None,) + shape, lambda i: (l, 0, 0, 0))
    whole = lambda shape: pl.BlockSpec(shape, lambda i: (0, 0))
    return pl.pallas_call(
        _s5_sample_kernel,
        grid=(1,),
        in_specs=[
            pl.BlockSpec((M_S, S5_W), lambda i: (M_P // M_S, su_col)),
            layer3((M_S, nst)), layer3((M_S, nst)),
            layer4((S5_NBLK, S5_BLK_CH, 2 * S5_BLK_ST)),
            layer4((S5_NBLK, 2 * S5_BLK_ST, S5_BLK_CH)),
            layer3((1, nst)), layer3((1, nst)),
            _layer_row_spec(l, S5_W),
        ],
        out_specs=[whole((M_S, S5_W)), whole((M_S, nst)), whole((M_S, nst))],
        out_shape=[
            jax.ShapeDtypeStruct((M_S, S5_W), F32),
            jax.ShapeDtypeStruct((M_S, nst), F32),
            jax.ShapeDtypeStruct((M_S, nst), F32),
        ],
        compiler_params=_cparams("arbitrary"),
        name="s5_sample",
    )(proj, h0r, h0i, w_blk, cw_blk, lbr, lbi, dvec)


TM_MERGE = 520


def _merge_kernel(z_ref, o_ref, gr0_ref, gr1_ref, gs0_ref, gs1_ref, gluw_ref, glub_ref, s5p_ref,
                  retp_ref, m_ref):
    z = z_ref[...]
    t = jnp.dot(z.astype(BF16), gluw_ref[...], preferred_element_type=F32) + glub_ref[...]
    z2 = (z * jax.nn.sigmoid(t)).astype(BF16)
    o = o_ref[...]
    for c, (gr_ref, gs_ref) in enumerate(((gr0_ref, gs0_ref), (gr1_ref, gs1_ref))):
        cols = slice(c * COL, (c + 1) * COL)
        b_s5 = jnp.dot(z2, s5p_ref[:, cols], preferred_element_type=F32)
        b_ret = jnp.dot(o, retp_ref[:, cols], preferred_element_type=F32)
        m_ref[:, cols] = (gr_ref[...] * b_ret + gs_ref[...] * b_s5).astype(BF16)


def _merge(z, o, proj, glu_w, glu_b, s5_proj, ret_proj, l):
    gr_col = (2 * RET_QK + 2 * RET_W + S5_W) // COL
    gs_col = gr_col + D_MODEL // COL
    gate = lambda col: pl.BlockSpec((TM_MERGE, COL), lambda i, col=col: (i, col))
    return pl.pallas_call(
        _merge_kernel,
        grid=(M_ALL // TM_MERGE,),
        in_specs=[
            pl.BlockSpec((TM_MERGE, S5_W), lambda i: (i, 0)),
            pl.BlockSpec((TM_MERGE, RET_W), lambda i: (i, 0)),
            gate(gr_col), gate(gr_col + 1), gate(gs_col), gate(gs_col + 1),
            _resident_layer_spec(0, (S5_W, S5_W)),
            _layer_row_spec(l, S5_W),
            _resident_layer_spec(0, (S5_W, D_MODEL)),
            _resident_layer_spec(0, (RET_W, D_MODEL)),
        ],
        out_specs=pl.BlockSpec((TM_MERGE, D_MODEL), lambda i: (i, 0)),
        out_shape=jax.ShapeDtypeStruct((M_ALL, D_MODEL), BF16),
        compiler_params=_cparams("parallel"),
        name="merge",
    )(z, o, proj, proj, proj, proj, glu_w, glu_b, s5_proj, ret_proj)


def _final_norm_kernel(x_ref, g_ref, o_ref):
    x = x_ref[...]
    ms = jnp.mean(x * x, axis=-1, keepdims=True)
    o_ref[...] = (x * lax.rsqrt(ms + NORM_EPS)) * g_ref[...]


def _final_norm(x, g, rows, tm, block0):
    return pl.pallas_call(
        _final_norm_kernel,
        grid=(rows // tm,),
        in_specs=[pl.BlockSpec((tm, D_MODEL), lambda i: (block0 + i, 0)),
                  pl.BlockSpec((1, D_MODEL), lambda i: (0, 0))],
        out_specs=pl.BlockSpec((tm, D_MODEL), lambda i: (i, 0)),
        out_shape=jax.ShapeDtypeStruct((rows, D_MODEL), F32),
        compiler_params=_cparams("parallel"),
        name="final_norm",
    )(x, g)


def _rope_tables():
    half = RET_DK // 2
    inv = ROPE_BASE ** (-jnp.arange(half, dtype=F32) / half)
    pos = jnp.concatenate([jnp.tile(jnp.arange(SEQ, dtype=jnp.int32), BATCH),
                           jnp.full((M_S,), PAST_LEN, jnp.int32)])
    ang = pos.astype(F32)[:, None] * inv[None, :]
    cos = jnp.cos(ang)
    sin = jnp.sin(ang)
    return jnp.concatenate([cos, cos], axis=1), jnp.concatenate([-sin, sin], axis=1)


def _retention_tables():
    c = RET_CHUNK
    log_g = jnp.log1p(-jnp.exp2(-5.0 - jnp.arange(RET_HEADS, dtype=F32)))
    i = jnp.arange(c, dtype=F32)
    diff = i[:, None] - i[None, :]
    intra = jnp.where(diff[None] >= 0,
                      jnp.exp(jnp.maximum(diff, 0.0)[None] * log_g[:, None, None]), 0.0)
    q_dec = jnp.exp((i + 1.0)[None, :] * log_g[:, None])
    k_dec = jnp.exp((c - 1.0 - i)[None, :] * log_g[:, None])
    c_dec = jnp.exp(c * log_g)
    full = lambda a: jnp.broadcast_to(a, (RET_HEADS, c, RET_DV))
    g1 = jnp.broadcast_to(jnp.exp(1.0 * log_g)[:, None, None], (RET_HEADS, 1, RET_DV))
    return (intra, full(q_dec[:, :, None]), full(k_dec[:, :, None]), full(c_dec[:, None, None])), g1


def _s5_block_weights(bbr_t, bbi_t, c_re, c_im):
    ch_group = jnp.arange(S5_BLK_CH, dtype=jnp.int32) // S5_GROUP_CH
    st_group = jnp.arange(S5_BLK_ST, dtype=jnp.int32) // S5_STATE
    drive_mask = ch_group[:, None] == st_group[None, :]

    def drive(bt):
        a = bt.reshape(DEPTH, S5_NBLK, S5_BLK_CH, 1, S5_STATE)
        a = jnp.broadcast_to(a, (DEPTH, S5_NBLK, S5_BLK_CH, S5_GB, S5_STATE))
        return jnp.where(drive_mask, a.reshape(DEPTH, S5_NBLK, S5_BLK_CH, S5_BLK_ST), 0.0)

    def readout(cm):
        a = cm.reshape(DEPTH, S5_NBLK, S5_GB, S5_GROUP_CH, S5_STATE)
        a = a.transpose(0, 1, 4, 2, 3).reshape(DEPTH, S5_NBLK, 1, S5_STATE, S5_BLK_CH)
        a = jnp.broadcast_to(a, (DEPTH, S5_NBLK, S5_GB, S5_STATE, S5_BLK_CH))
        return jnp.where(drive_mask.T, a.reshape(DEPTH, S5_NBLK, S5_BLK_ST, S5_BLK_CH), 0.0)

    w_blk = jnp.concatenate([drive(bbr_t), drive(bbi_t)], axis=3).astype(BF16)
    cw_blk = jnp.concatenate([readout(c_re), readout(-c_im)], axis=2).astype(BF16)
    return w_blk, cw_blk


def _slab_tiles(lb):
    a = lb.reshape(DEPTH, 2, S5_NSLAB, LANES).transpose(0, 2, 1, 3)
    return jnp.tile(a, (1, 1, BATCH, 1))


def kernel(x_prompt, x_sample, state_ret, state_s5_re, state_s5_im, ffn1_norm, ffn1_w1, ffn1_w3, ffn1_w2, mix_norm, w_in, ret_gn, ret_proj, s5_lam_re, s5_lam_im, s5_b_re, s5_b_im, s5_c_re, s5_c_im, s5_d, s5_log_step, glu_w, glu_b, s5_proj, w_out, ffn2_norm, ffn2_w1, ffn2_w3, ffn2_w2, final_norm):
    cos_t, sin_t = _rope_tables()
    ret_tabs, g1 = _retention_tables()
    x = jnp.concatenate([x_prompt.reshape(M_P, D_MODEL), x_sample.reshape(M_S, D_MODEL)], axis=0)

    ret_p, s5r_p, s5i_p, s5r_s, s5i_s = [], [], [], [], []
    ret_s = None
    nst = S5_GROUPS * S5_STATE
    rows3 = lambda a: a.reshape(DEPTH, 1, -1)
    ffn1_g, mix_g, ffn2_g, glu_b3 = rows3(ffn1_norm), rows3(mix_norm), rows3(ffn2_norm), rows3(glu_b)
    gn4 = ret_gn.reshape(DEPTH, RET_HEADS, 1, RET_DV)

    lbr_x, lbi_x, bbr_t, bbi_t = _s5_discretize(s5_lam_re, s5_lam_im, s5_log_step, s5_b_re, s5_b_im)
    lbr = lbr_x[::S5_GROUP_CH].reshape(DEPTH, nst)
    lbi = lbi_x[::S5_GROUP_CH].reshape(DEPTH, nst)
    w_blk, cw_blk = _s5_block_weights(bbr_t, bbi_t, s5_c_re, s5_c_im)
    lamr_t, lami_t = _slab_tiles(lbr), _slab_tiles(lbi)
    lbr3, lbi3 = lbr.reshape(DEPTH, 1, nst), lbi.reshape(DEPTH, 1, nst)
    dvec3 = s5_d.reshape(DEPTH, 1, S5_W)
    h0r_all = state_s5_re.reshape(DEPTH, M_S, nst)
    h0i_all = state_s5_im.reshape(DEPTH, M_S, nst)

    xn = _norm(x, ffn1_g, 0)
    for l in range(DEPTH):
        hid, w2_16 = _ffn_up(xn, ffn1_w1, ffn1_w3, ffn1_w2, l)
        x, xn = _proj_res(hid, w2_16, x, 0, scale=0.5, tm=TM_DOWN, name="ffn_down",
                          gain=mix_g, gain_layer=l)

        proj, glu_w16, s5_proj16, ret_proj16, w_out16 = _inproj(
            xn, w_in, cos_t, sin_t, (glu_w, s5_proj, ret_proj, w_out), l)

        o_all, s_p = _ret_prompt(proj, ret_tabs, gn4, l)
        o_s, ret_s = _ret_sample(proj, state_ret, g1, gn4, l, ret_s)
        o_all = lax.dynamic_update_slice(o_all, o_s.astype(BF16), (M_P, 0))

        z5, hre_p, him_p = _s5_prompt(proj, w_blk, cw_blk, lamr_t, lami_t, dvec3, l)
        z_s, hre_s, him_s = _s5_sample(proj, h0r_all, h0i_all, w_blk, cw_blk, lbr3, lbi3, dvec3, l)
        z = lax.dynamic_update_slice(z5.reshape((BATCH + 1) * SEQ, S5_W), z_s, (M_P, 0))

        m = _merge(z, o_all, proj, glu_w16, glu_b3, s5_proj16, ret_proj16, l)
        x, xn = _proj_res(m, w_out16, x, 0, scale=1.0, tm=TM_MERGE, name="out_proj",
                          gain=ffn2_g, gain_layer=l)

        hid, w2_16 = _ffn_up(xn, ffn2_w1, ffn2_w3, ffn2_w2, l)
        if l + 1 < DEPTH:
            x, xn = _proj_res(hid, w2_16, x, 0, scale=0.5, tm=TM_DOWN, name="ffn_down",
                              gain=ffn1_g, gain_layer=l + 1)
        else:
            x, = _proj_res(hid, w2_16, x, 0, scale=0.5, tm=TM_DOWN, name="ffn_down")

        unslab = lambda a: (a.transpose(1, 0, 2).reshape(BATCH, 2, S5_HALF_ST)
                            .reshape(BATCH, S5_GROUPS, S5_STATE))
        ret_p.append(s_p)
        s5r_p.append(unslab(hre_p))
        s5i_p.append(unslab(him_p))
        s5r_s.append(hre_s.reshape(M_S, S5_GROUPS, S5_STATE))
        s5i_s.append(him_s.reshape(M_S, S5_GROUPS, S5_STATE))

    g_fin = final_norm.reshape(1, D_MODEL)
    y_p = _final_norm(x, g_fin, M_P, 1024, 0)
    y_s = _final_norm(x, g_fin, M_S, M_S, M_P // M_S)
    return (y_p.reshape(BATCH, SEQ, D_MODEL), y_s.reshape(M_S, 1, D_MODEL),
            jnp.stack(ret_p), jnp.stack(s5r_p), jnp.stack(s5i_p),
            ret_s, jnp.stack(s5r_s), jnp.stack(s5i_s))
```

```python
import functools
import math

import jax
import jax.numpy as jnp
from jax import lax
from jax.experimental import pallas as pl
from jax.experimental.pallas import tpu as pltpu

F32 = jnp.float32
BF16 = jnp.bfloat16

D_MODEL = 2048
BATCH = 4
SEQ = 2048
DEPTH = 4
DEC_BATCH = 128
PAST_LEN = 16384
RET_HEADS = 8
RET_DK = 128
RET_DV = 128
RET_QK = RET_HEADS * RET_DK
RET_W = RET_HEADS * RET_DV
RET_CHUNK = 128
ROPE_BASE = 10000.0
S5_GROUPS = 64
S5_GROUP_CH = 16
S5_W = S5_GROUPS * S5_GROUP_CH
S5_STATE = 64
D_FF = 5632
IN_W = 2 * RET_QK + 2 * RET_W + S5_W + 2 * D_MODEL
NORM_EPS = 1e-6
HEAD_NORM_EPS = 1e-5

M_P = BATCH * SEQ
M_S = DEC_BATCH
M_ALL = M_P + M_S

LANES = 128
SUBLANES = 8
VMEM_LIMIT = 58 * 1024 * 1024

TM = 1040
TN_FF = 512
TM_DOWN = 416
COL = 1024
TM_BIG = 4 * TM
TN_IN = 512
IN_SUB = 256

S5_TC = 64
S5_PAD = SUBLANES // 2
S5_PITCH = S5_TC + S5_PAD
S5_GB = 16
S5_NBLK = S5_GROUPS // S5_GB
S5_BLK_CH = S5_GB * S5_GROUP_CH
S5_BLK_ST = S5_GB * S5_STATE
S5_HALF_ST = S5_GROUPS * S5_STATE // 2
S5_NSLAB = S5_HALF_ST // LANES
S5_NSEQ = BATCH * 2
S5_SLAB_GROUP = 4


def _cparams(*sem):
    return pltpu.CompilerParams(dimension_semantics=sem, vmem_limit_bytes=VMEM_LIMIT)


def _rms_to_bf16(x, g):
    ms = jnp.mean(x * x, axis=-1, keepdims=True)
    return ((x * lax.rsqrt(ms + NORM_EPS)) * g).astype(BF16)


def _layer_spec(l, k, tn):
    return pl.BlockSpec((None, k, tn), lambda i, j: (l, 0, j))


def _layer_row_spec(l, n):
    return pl.BlockSpec((None, 1, n), lambda *_: (l, 0, 0))


def _norm_kernel(x_ref, g_ref, o_ref):
    o_ref[...] = _rms_to_bf16(x_ref[...], g_ref[...])


def _norm(x, g, l):
    return pl.pallas_call(
        _norm_kernel,
        grid=(M_ALL // TM,),
        in_specs=[pl.BlockSpec((TM, D_MODEL), lambda i: (i, 0)), _layer_row_spec(l, D_MODEL)],
        out_specs=pl.BlockSpec((TM, D_MODEL), lambda i: (i, 0)),
        out_shape=jax.ShapeDtypeStruct((M_ALL, D_MODEL), BF16),
        compiler_params=_cparams("parallel"),
        name="norm",
    )(x, g)


def _resident_rows_spec():
    return pl.BlockSpec((TM_BIG, D_MODEL), lambda r, j: (r, 0), pipeline_mode=pl.Buffered(1))


def _ffn_up_kernel(xn_ref, w1_ref, w3_ref, w2_ref, o_ref, w2_16_ref):
    w1 = w1_ref[...].astype(BF16)
    w3 = w3_ref[...].astype(BF16)
    for s in range(TM_BIG // TM):
        rows = slice(s * TM, (s + 1) * TM)
        xs = xn_ref[rows, :]
        h1 = jnp.dot(xs, w1, preferred_element_type=F32)
        h3 = jnp.dot(xs, w3, preferred_element_type=F32)
        o_ref[rows, :] = ((h1 * jax.nn.sigmoid(h1)) * h3).astype(BF16)
    w2_16_ref[...] = w2_ref[...].astype(BF16)


def _ffn_up(xn, w1, w3, w2, l):
    ncol = D_FF // TN_FF
    nsteps = (M_ALL // TM_BIG) * ncol
    chunk = D_FF // nsteps
    return pl.pallas_call(
        _ffn_up_kernel,
        grid=(M_ALL // TM_BIG, ncol),
        in_specs=[
            _resident_rows_spec(),
            _layer_spec(l, D_MODEL, TN_FF),
            _layer_spec(l, D_MODEL, TN_FF),
            pl.BlockSpec((None, chunk, D_MODEL), lambda r, j: (l, r * ncol + j, 0)),
        ],
        out_specs=[
            pl.BlockSpec((TM_BIG, TN_FF), lambda r, j: (r, j)),
            pl.BlockSpec((None, chunk, D_MODEL), lambda r, j: (0, r * ncol + j, 0)),
        ],
        out_shape=[
            jax.ShapeDtypeStruct((M_ALL, D_FF), BF16),
            jax.ShapeDtypeStruct((1, D_FF, D_MODEL), BF16),
        ],
        compiler_params=_cparams("parallel", "arbitrary"),
        name="ffn_up",
    )(xn, w1, w3, w2)


def _resident_layer_spec(l, shape):
    return pl.BlockSpec((None,) + shape, lambda i: (l, 0, 0), pipeline_mode=pl.Buffered(1))


def _proj_res_kernel(a_ref, w_ref, x_ref, *rest, scale):
    o_ref = rest[-2] if len(rest) == 3 else rest[0]
    for c in range(D_MODEL // COL):
        cols = slice(c * COL, (c + 1) * COL)
        acc = jnp.dot(a_ref[...], w_ref[:, cols], preferred_element_type=F32)
        o_ref[:, cols] = x_ref[:, cols] + scale * acc
    if len(rest) == 3:
        rest[2][...] = _rms_to_bf16(o_ref[...], rest[0][...])


def _proj_res(a, w16, x, l, *, scale, tm, name, gain=None, gain_layer=None):
    k = a.shape[1]
    row_spec = lambda: pl.BlockSpec((tm, D_MODEL), lambda i: (i, 0))
    in_specs = [pl.BlockSpec((tm, k), lambda i: (i, 0)), _resident_layer_spec(l, (k, D_MODEL)),
                row_spec()]
    args = [a, w16, x]
    out_specs = [row_spec()]
    out_shape = [jax.ShapeDtypeStruct((M_ALL, D_MODEL), F32)]
    if gain is not None:
        in_specs.append(_layer_row_spec(gain_layer, D_MODEL))
        args.append(gain)
        out_specs.append(row_spec())
        out_shape.append(jax.ShapeDtypeStruct((M_ALL, D_MODEL), BF16))
    return pl.pallas_call(
        functools.partial(_proj_res_kernel, scale=scale),
        grid=(M_ALL // tm,),
        in_specs=in_specs,
        out_specs=out_specs,
        out_shape=out_shape,
        compiler_params=_cparams("parallel"),
        name=name,
    )(*args)


def _inproj_kernel(xn_ref, w_ref, cos_ref, sin_ref, *rest):
    o_ref = rest[len(rest) // 2]
    for src, dst in zip(rest[:len(rest) // 2], rest[len(rest) // 2 + 1:]):
        dst[...] = src[...].astype(BF16)

    seg = pl.program_id(1) // (COL // TN_IN)

    def project(epilogue):
        wb = [w_ref[:, c * IN_SUB:(c + 1) * IN_SUB].astype(BF16) for c in range(TN_IN // IN_SUB)]
        for s in range(TM_BIG // TM):
            rows = slice(s * TM, (s + 1) * TM)
            for c in range(TN_IN // IN_SUB):
                acc = jnp.dot(xn_ref[rows, :], wb[c], preferred_element_type=F32)
                o_ref[rows, c * IN_SUB:(c + 1) * IN_SUB] = epilogue(acc, rows)

    def rotary(acc, rows):
        scale = jnp.where(seg == 0, RET_DK ** -0.5, 1.0).astype(F32)
        cos = cos_ref[rows, :]
        sin = sin_ref[rows, :]
        heads = []
        for h in range(IN_SUB // RET_DK):
            a = acc[:, h * RET_DK:(h + 1) * RET_DK]
            heads.append((a * cos + pltpu.roll(a, RET_DK // 2, axis=1) * sin) * scale)
        return jnp.concatenate(heads, axis=1)

    @pl.when(seg <= 1)
    def _():
        project(rotary)

    @pl.when((seg == 2) | (seg == 4))
    def _():
        project(lambda acc, rows: acc)

    @pl.when(seg == 3)
    def _():
        project(lambda acc, rows: acc * jax.nn.sigmoid(acc))

    @pl.when(seg >= 5)
    def _():
        project(lambda acc, rows: jax.nn.sigmoid(acc))


CAST_STEPS = 32


def _inproj(xn, w, cos_t, sin_t, small_weights, l):
    ncol = IN_W // TN_IN
    tab_spec = pl.BlockSpec((TM_BIG, RET_DK), lambda r, j: (r, 0), pipeline_mode=pl.Buffered(1))
    chunk_idx = lambda r, j: jnp.minimum(r * ncol + j, CAST_STEPS - 1)
    cast_in, cast_out, cast_shapes = [], [], []
    for a in small_weights:
        rows, cols = a.shape[1], a.shape[2]
        blk = (None, rows // CAST_STEPS, cols)
        cast_in.append(pl.BlockSpec(blk, lambda r, j: (l, chunk_idx(r, j), 0)))
        cast_out.append(pl.BlockSpec(blk, lambda r, j: (0, chunk_idx(r, j), 0)))
        cast_shapes.append(jax.ShapeDtypeStruct((1, rows, cols), BF16))
    return pl.pallas_call(
        _inproj_kernel,
        grid=(M_ALL // TM_BIG, ncol),
        in_specs=[
            _resident_rows_spec(),
            _layer_spec(l, D_MODEL, TN_IN),
            tab_spec, tab_spec,
        ] + cast_in,
        out_specs=[pl.BlockSpec((TM_BIG, TN_IN), lambda r, j: (r, j))] + cast_out,
        out_shape=[jax.ShapeDtypeStruct((M_ALL, IN_W), F32)] + cast_shapes,
        compiler_params=_cparams("parallel", "arbitrary"),
        name="inproj",
    )(xn, w, cos_t, sin_t, *small_weights)


def _head_norm(o):
    mu = jnp.mean(o, axis=-1, keepdims=True)
    d = o - mu
    var = jnp.mean(d * d, axis=-1, keepdims=True)
    return d * lax.rsqrt(var + HEAD_NORM_EPS)


RET_RC = 2 * RET_CHUNK


def _ret_prompt_kernel(q_ref, k_ref, v_ref, rg_ref, intra_ref, qd_ref, kd_ref, cd_ref, gn_ref,
                       o_ref, st_ref):
    nc = RET_RC // RET_CHUNK
    blk = lambda c, h: (slice(c * RET_CHUNK, (c + 1) * RET_CHUNK), slice(h * RET_DK, (h + 1) * RET_DK))

    o_intra, kv = {}, {}
    for c in range(nc):
        for h in range(RET_HEADS):
            qc = q_ref[blk(c, h)]
            kc = k_ref[blk(c, h)]
            vb = v_ref[blk(c, h)].astype(BF16)
            sc = lax.dot_general(qc.astype(BF16), kc.astype(BF16), (((1,), (1,)), ((), ())),
                                 preferred_element_type=F32) * intra_ref[h]
            o_intra[c, h] = jnp.dot(sc.astype(BF16), vb, preferred_element_type=F32)
            kt = jnp.transpose(kc * kd_ref[h]).astype(BF16)
            kv[c, h] = jnp.dot(kt, vb, preferred_element_type=F32)

    for h in range(RET_HEADS):
        s = st_ref[h]
        for c in range(nc):
            qs = jnp.dot((q_ref[blk(c, h)] * qd_ref[h]).astype(BF16), s.astype(BF16),
                         preferred_element_type=F32)
            o = o_intra[c, h] + qs
            s = s * cd_ref[h] + kv[c, h]
            on = (_head_norm(o) * gn_ref[h]) * rg_ref[blk(c, h)]
            o_ref[blk(c, h)] = on.astype(BF16)
        st_ref[h] = s


RET_SB = 8


def _col_bcast(row, eye, ones2):
    hi = row.astype(BF16)
    lo = (row - hi.astype(F32)).astype(BF16)
    parts = [jnp.where(eye, jnp.broadcast_to(t.astype(F32), (RET_DK, RET_DK)), 0.0).astype(BF16)
             for t in (hi, lo)]
    return jnp.dot(jnp.concatenate(parts, axis=1), ones2, preferred_element_type=F32)


def _ret_sample_kernel(q_ref, k_ref, v_ref, rg_ref, s_ref, g1_ref, gn_ref, *rest):
    o_ref, so_ref = rest[-2:]
    eye = (lax.broadcasted_iota(jnp.int32, (RET_DK, RET_DK), 0)
           == lax.broadcasted_iota(jnp.int32, (RET_DK, RET_DK), 1))
    ones2 = jnp.ones((2 * RET_DK, RET_DV), BF16)

    for b in range(RET_SB):
        row = slice(b, b + 1)
        for h in range(RET_HEADS):
            cols = slice(h * RET_DK, (h + 1) * RET_DK)
            kcol = _col_bcast(k_ref[row, cols], eye, ones2)
            qcol = _col_bcast(q_ref[row, cols], eye, ones2)
            s_new = s_ref[b, h] * g1_ref[h] + kcol * v_ref[row, cols]
            so_ref[b, h] = s_new
            o = jnp.sum(qcol * s_new, axis=0, keepdims=True)
            o_ref[row, cols] = (_head_norm(o) * gn_ref[h]) * rg_ref[row, cols]


def _ret_sample(proj, state, g1, gn, l, prev_states):
    row0 = M_P // RET_SB

    def seg_spec(seg):
        return pl.BlockSpec((RET_SB, COL), lambda i, seg=seg: (row0 + i, seg))

    st_spec = pl.BlockSpec((None, RET_SB, RET_HEADS, RET_DK, RET_DV), lambda i: (l, i, 0, 0, 0))
    in_specs = [
        seg_spec(0), seg_spec(1), seg_spec(2), seg_spec(3), st_spec,
        pl.BlockSpec((RET_HEADS, 1, RET_DV), lambda i: (0, 0, 0)),
        pl.BlockSpec((None, RET_HEADS, 1, RET_DV), lambda i: (l, 0, 0, 0)),
    ]
    args = [proj, proj, proj, proj, state, g1, gn]
    aliases = {}
    if prev_states is not None:
        in_specs.append(pl.BlockSpec(memory_space=pl.ANY))
        args.append(prev_states)
        aliases = {len(args) - 1: 1}
    return pl.pallas_call(
        _ret_sample_kernel,
        grid=(M_S // RET_SB,),
        in_specs=in_specs,
        out_specs=[pl.BlockSpec((RET_SB, RET_W), lambda i: (i, 0)), st_spec],
        out_shape=[
            jax.ShapeDtypeStruct((M_S, RET_W), F32),
            jax.ShapeDtypeStruct((DEPTH, M_S, RET_HEADS, RET_DK, RET_DV), F32),
        ],
        input_output_aliases=aliases,
        compiler_params=_cparams("parallel"),
        name="ret_sample",
    )(*args)


def _s5_disc_kernel(lr_ref, li_ref, ls_ref, br_ref, bi_ref, lbr_ref, lbi_ref, bbr_ref, bbi_ref):
    lam_r = lr_ref[...]
    lam_i = li_ref[...]
    dt = jnp.exp(ls_ref[...])
    ea = jnp.exp(lam_r * dt)
    lb_r = ea * jnp.cos(lam_i * dt)
    lb_i = ea * jnp.sin(lam_i * dt)
    lbr_ref[...] = lb_r
    lbi_ref[...] = lb_i
    nr = lb_r - 1.0
    den = lam_r * lam_r + lam_i * lam_i
    cr = (nr * lam_r + lb_i * lam_i) / den
    ci = (lb_i * lam_r - nr * lam_i) / den
    b_r = br_ref[...]
    b_i = bi_ref[...]
    bbr_ref[...] = cr * b_r - ci * b_i
    bbi_ref[...] = cr * b_i + ci * b_r


def _s5_discretize(lam_re, lam_im, log_step, b_re, b_im):
    rows = DEPTH * S5_GROUPS * S5_GROUP_CH
    rep = lambda a: jnp.repeat(a.reshape(DEPTH * S5_GROUPS, S5_STATE), S5_GROUP_CH, axis=0)
    tr = lambda a: jnp.transpose(a, (0, 1, 3, 2)).reshape(rows, S5_STATE)
    shp = jax.ShapeDtypeStruct((rows, S5_STATE), F32)
    ls = jnp.broadcast_to(log_step[:, :, None], (DEPTH, S5_GROUPS, S5_STATE))
    spec = pl.BlockSpec((rows // DEPTH, S5_STATE), lambda l: (l, 0))
    return pl.pallas_call(
        _s5_disc_kernel,
        grid=(DEPTH,),
        in_specs=[spec] * 5,
        out_specs=[spec] * 4,
        out_shape=[shp, shp, shp, shp],
        compiler_params=_cparams("parallel"),
        name="s5_disc",
    )(rep(lam_re), rep(lam_im), rep(ls), tr(b_re), tr(b_im))


def _gelu_tanh(y):
    c = math.sqrt(2.0 / math.pi)
    return 0.5 * y * (1.0 + jnp.tanh(c * (y + 0.044715 * (y * y * y))))


def _s5_prompt_kernel(u0_ref, u1_ref, u2_ref, u3_ref, w_ref, cw_ref, lamr_ref, lami_ref, d_ref,
                      z_ref, hre_ref, him_ref, bre_ref, bim_ref):
    @pl.when(pl.program_id(0) == 0)
    def _():
        hre_ref[...] = jnp.zeros_like(hre_ref)
        him_ref[...] = jnp.zeros_like(him_ref)

    _s5_prompt_body(u0_ref, u1_ref, u2_ref, u3_ref, w_ref, cw_ref, lamr_ref, lami_ref, d_ref,
                    z_ref, hre_ref, him_ref, bre_ref, bim_ref)


def _s5_prompt_body(u0_ref, u1_ref, u2_ref, u3_ref, w_ref, cw_ref, lamr_ref, lami_ref, d_ref,
                    z_ref, hre_ref, him_ref, bre_ref, bim_ref):
    u_refs = (u0_ref, u1_ref, u2_ref, u3_ref)

    nsl = S5_BLK_ST // LANES
    zpad = jnp.zeros((S5_PAD, S5_BLK_CH), F32)

    for gh in range(2):
        seg = S5_TC + 2 * S5_PAD * gh
        for kb in range(2):
            blk = gh * 2 + kb
            c0 = blk * S5_BLK_CH
            parts = []
            for u in u_refs:
                ub = u[:, c0:c0 + S5_BLK_CH]
                parts.extend([zpad, ub, zpad] if gh else [ub])
            x = jnp.concatenate(parts, axis=0).astype(BF16)
            res = jnp.dot(x, w_ref[blk], preferred_element_type=F32)
            for b in range(BATCH):
                r0 = (b * 2 + gh) * S5_PITCH - S5_PAD * gh
                part = res[b * seg:(b + 1) * seg]
                for s in range(nsl):
                    slab = kb * nsl + s
                    bre_ref[slab, r0:r0 + seg, :] = part[:, s * LANES:(s + 1) * LANES]
                    bim_ref[slab, r0:r0 + seg, :] = part[:, S5_BLK_ST + s * LANES:S5_BLK_ST + (s + 1) * LANES]

    for sg in range(S5_NSLAB // S5_SLAB_GROUP):
        slabs = [sg * S5_SLAB_GROUP + i for i in range(S5_SLAB_GROUP)]
        lam_r = [lamr_ref[s] for s in slabs]
        lam_i = [lami_ref[s] for s in slabs]

        def step(t, carry, slabs=slabs, lam_r=lam_r, lam_i=lam_i):
            hr, hi = carry
            nr_all = []
            ni_all = []
            for i, s in enumerate(slabs):
                idx = pl.ds(t, S5_NSEQ, stride=S5_PITCH)
                br = bre_ref[s, idx, :]
                bi = bim_ref[s, idx, :]
                nr = lam_r[i] * hr[i] - lam_i[i] * hi[i] + br
                ni = lam_r[i] * hi[i] + lam_i[i] * hr[i] + bi
                bre_ref[s, idx, :] = nr
                bim_ref[s, idx, :] = ni
                nr_all.append(nr)
                ni_all.append(ni)
            return tuple(nr_all), tuple(ni_all)

        hr, hi = (tuple(hre_ref[s] for s in slabs), tuple(him_ref[s] for s in slabs))
        for t in range(S5_TC):
            hr, hi = step(t, (hr, hi))
        for i, s in enumerate(slabs):
            hre_ref[s] = hr[i]
            him_ref[s] = hi[i]

    for gh in range(2):
        seg = S5_TC + 2 * S5_PAD * gh
        for kb in range(2):
            blk = gh * 2 + kb
            c0 = blk * S5_BLK_CH
            pieces = []
            for src in (bre_ref, bim_ref):
                for s in range(nsl):
                    slab = kb * nsl + s
                    starts = [(b * 2 + gh) * S5_PITCH - S5_PAD * gh for b in range(BATCH)]
                    pieces.append(jnp.concatenate(
                        [src[slab, r0:r0 + seg, :] for r0 in starts], axis=0))
            hcat = jnp.concatenate(pieces, axis=1).astype(BF16)
            y = jnp.dot(hcat, cw_ref[blk], preferred_element_type=F32)
            dvec = d_ref[:, c0:c0 + S5_BLK_CH]
            for b in range(BATCH):
                lo = b * seg + S5_PAD * gh
                yb = y[lo:lo + S5_TC] + dvec * u_refs[b][:, c0:c0 + S5_BLK_CH]
                z_ref[b, :, c0:c0 + S5_BLK_CH] = _gelu_tanh(yb)


RET_STEPS = SEQ // RET_RC
assert BATCH * RET_STEPS == SEQ // S5_TC


def _mixer_prompt_kernel(u0_ref, u1_ref, u2_ref, u3_ref, w_ref, cw_ref, lamr_ref, lami_ref, d_ref,
                         q_ref, k_ref, v_ref, rg_ref, intra_ref, qd_ref, kd_ref, cd_ref, gn_ref,
                         z_ref, hre_ref, him_ref, o_ref, s_ref, bre_ref, bim_ref, st_ref):
    j = pl.program_id(0)
    rb = j % RET_STEPS

    @pl.when(j == 0)
    def _():
        hre_ref[...] = jnp.zeros_like(hre_ref)
        him_ref[...] = jnp.zeros_like(him_ref)

    @pl.when(rb == 0)
    def _():
        st_ref[...] = jnp.zeros_like(st_ref)

    _s5_prompt_body(u0_ref, u1_ref, u2_ref, u3_ref, w_ref, cw_ref, lamr_ref, lami_ref, d_ref,
                    z_ref, hre_ref, him_ref, bre_ref, bim_ref)
    _ret_prompt_kernel(q_ref, k_ref, v_ref, rg_ref, intra_ref, qd_ref, kd_ref, cd_ref, gn_ref,
                       o_ref, st_ref)

    @pl.when(rb == RET_STEPS - 1)
    def _():
        s_ref[...] = st_ref[...]


def _mixer_prompt(proj, w_blk, cw_blk, lamr_t, lami_t, dvec, ret_tabs, gn, l):
    nsteps = SEQ // S5_TC
    su_col = (2 * RET_QK + 2 * RET_W) // S5_W
    intra, qd, kd, cd = ret_tabs

    def u_spec(b):
        return pl.BlockSpec((S5_TC, S5_W), lambda j, b=b: (b * nsteps + j, su_col))

    def seg_spec(seg):
        return pl.BlockSpec((RET_RC, COL), lambda j, seg=seg: (j, seg))

    layer4 = lambda shape: pl.BlockSpec((None,) + shape, lambda j: (l, 0, 0, 0))
    full3 = lambda shape: pl.BlockSpec(shape, lambda j: (0, 0, 0))
    st_shape = (S5_NSLAB, S5_NSEQ, LANES)
    tab_spec = full3((RET_HEADS, RET_CHUNK, RET_CHUNK))
    return pl.pallas_call(
        _mixer_prompt_kernel,
        grid=(nsteps,),
        in_specs=[
            u_spec(0), u_spec(1), u_spec(2), u_spec(3),
            layer4((S5_NBLK, S5_BLK_CH, 2 * S5_BLK_ST)),
            layer4((S5_NBLK, 2 * S5_BLK_ST, S5_BLK_CH)),
            layer4(st_shape), layer4(st_shape),
            _layer_row_spec(l, S5_W),
            seg_spec(0), seg_spec(1), seg_spec(2), seg_spec(3),
            tab_spec, tab_spec, tab_spec, tab_spec,
            layer4((RET_HEADS, 1, RET_DV)),
        ],
        out_specs=[
            pl.BlockSpec((BATCH, S5_TC, S5_W), lambda j: (0, j, 0)),
            full3(st_shape), full3(st_shape),
            pl.BlockSpec((RET_RC, RET_W), lambda j: (j, 0)),
            pl.BlockSpec((None, RET_HEADS, RET_DK, RET_DV), lambda j: (j // RET_STEPS, 0, 0, 0)),
        ],
        out_shape=[
            jax.ShapeDtypeStruct((BATCH + 1, SEQ, S5_W), F32),
            jax.ShapeDtypeStruct(st_shape, F32),
            jax.ShapeDtypeStruct(st_shape, F32),
            jax.ShapeDtypeStruct((M_ALL, RET_W), BF16),
            jax.ShapeDtypeStruct((BATCH, RET_HEADS, RET_DK, RET_DV), F32),
        ],
        scratch_shapes=[
            pltpu.VMEM((S5_NSLAB, S5_NSEQ * S5_PITCH, LANES), F32),
            pltpu.VMEM((S5_NSLAB, S5_NSEQ * S5_PITCH, LANES), F32),
            pltpu.VMEM((RET_HEADS, RET_DK, RET_DV), F32),
        ],
        compiler_params=_cparams("arbitrary"),
        name="mixer_prompt",
    )(proj, proj, proj, proj, w_blk, cw_blk, lamr_t, lami_t, dvec,
      proj, proj, proj, proj, intra, qd, kd, cd, gn)


def _s5_sample_kernel(u_ref, h0r_ref, h0i_ref, w_ref, cw_ref, lbr_ref, lbi_ref, d_ref,
                      z_ref, hr_ref, hi_ref):
    for blk in range(S5_NBLK):
        c0 = blk * S5_BLK_CH
        s0 = blk * S5_BLK_ST
        u = u_ref[:, c0:c0 + S5_BLK_CH]
        res = jnp.dot(u.astype(BF16), w_ref[blk], preferred_element_type=F32)
        lr = lbr_ref[:, s0:s0 + S5_BLK_ST]
        li = lbi_ref[:, s0:s0 + S5_BLK_ST]
        h0r = h0r_ref[:, s0:s0 + S5_BLK_ST]
        h0i = h0i_ref[:, s0:s0 + S5_BLK_ST]
        hr = lr * h0r - li * h0i + res[:, :S5_BLK_ST]
        hi = lr * h0i + li * h0r + res[:, S5_BLK_ST:]
        hr_ref[:, s0:s0 + S5_BLK_ST] = hr
        hi_ref[:, s0:s0 + S5_BLK_ST] = hi
        hcat = jnp.concatenate([hr, hi], axis=1).astype(BF16)
        y = jnp.dot(hcat, cw_ref[blk], preferred_element_type=F32) + d_ref[:, c0:c0 + S5_BLK_CH] * u
        z_ref[:, c0:c0 + S5_BLK_CH] = _gelu_tanh(y)


def _s5_sample(proj, h0r, h0i, w_blk, cw_blk, lbr, lbi, dvec, l):
    nst = S5_GROUPS * S5_STATE
    su_col = (2 * RET_QK + 2 * RET_W) // S5_W
    layer3 = lambda shape: pl.BlockSpec((None,) + shape, lambda i: (l, 0, 0))
    layer4 = lambda shape: pl.BlockSpec((None,) + shape, lambda i: (l, 0, 0, 0))
    whole = lambda shape: pl.BlockSpec(shape, lambda i: (0, 0))
    return pl.pallas_call(
        _s5_sample_kernel,
        grid=(1,),
        in_specs=[
            pl.BlockSpec((M_S, S5_W), lambda i: (M_P // M_S, su_col)),
            layer3((M_S, nst)), layer3((M_S, nst)),
            layer4((S5_NBLK, S5_BLK_CH, 2 * S5_BLK_ST)),
            layer4((S5_NBLK, 2 * S5_BLK_ST, S5_BLK_CH)),
            layer3((1, nst)), layer3((1, nst)),
            _layer_row_spec(l, S5_W),
        ],
        out_specs=[whole((M_S, S5_W)), whole((M_S, nst)), whole((M_S, nst))],
        out_shape=[
            jax.ShapeDtypeStruct((M_S, S5_W), F32),
            jax.ShapeDtypeStruct((M_S, nst), F32),
            jax.ShapeDtypeStruct((M_S, nst), F32),
        ],
        compiler_params=_cparams("arbitrary"),
        name="s5_sample",
    )(proj, h0r, h0i, w_blk, cw_blk, lbr, lbi, dvec)


TM_MERGE = 520


def _merge_kernel(z_ref, o_ref, gr0_ref, gr1_ref, gs0_ref, gs1_ref, gluw_ref, glub_ref, s5p_ref,
                  retp_ref, m_ref):
    z = z_ref[...]
    t = jnp.dot(z.astype(BF16), gluw_ref[...], preferred_element_type=F32) + glub_ref[...]
    z2 = (z * jax.nn.sigmoid(t)).astype(BF16)
    o = o_ref[...]
    for c, (gr_ref, gs_ref) in enumerate(((gr0_ref, gs0_ref), (gr1_ref, gs1_ref))):
        cols = slice(c * COL, (c + 1) * COL)
        b_s5 = jnp.dot(z2, s5p_ref[:, cols], preferred_element_type=F32)
        b_ret = jnp.dot(o, retp_ref[:, cols], preferred_element_type=F32)
        m_ref[:, cols] = (gr_ref[...] * b_ret + gs_ref[...] * b_s5).astype(BF16)


def _merge(z, o, proj, glu_w, glu_b, s5_proj, ret_proj, l):
    gr_col = (2 * RET_QK + 2 * RET_W + S5_W) // COL
    gs_col = gr_col + D_MODEL // COL
    gate = lambda col: pl.BlockSpec((TM_MERGE, COL), lambda i, col=col: (i, col))
    return pl.pallas_call(
        _merge_kernel,
        grid=(M_ALL // TM_MERGE,),
        in_specs=[
            pl.BlockSpec((TM_MERGE, S5_W), lambda i: (i, 0)),
            pl.BlockSpec((TM_MERGE, RET_W), lambda i: (i, 0)),
            gate(gr_col), gate(gr_col + 1), gate(gs_col), gate(gs_col + 1),
            _resident_layer_spec(0, (S5_W, S5_W)),
            _layer_row_spec(l, S5_W),
            _resident_layer_spec(0, (S5_W, D_MODEL)),
            _resident_layer_spec(0, (RET_W, D_MODEL)),
        ],
        out_specs=pl.BlockSpec((TM_MERGE, D_MODEL), lambda i: (i, 0)),
        out_shape=jax.ShapeDtypeStruct((M_ALL, D_MODEL), BF16),
        compiler_params=_cparams("parallel"),
        name="merge",
    )(z, o, proj, proj, proj, proj, glu_w, glu_b, s5_proj, ret_proj)


def _final_norm_kernel(x_ref, g_ref, o_ref):
    x = x_ref[...]
    ms = jnp.mean(x * x, axis=-1, keepdims=True)
    o_ref[...] = (x * lax.rsqrt(ms + NORM_EPS)) * g_ref[...]


def _final_norm(x, g, rows, tm, block0):
    return pl.pallas_call(
        _final_norm_kernel,
        grid=(rows // tm,),
        in_specs=[pl.BlockSpec((tm, D_MODEL), lambda i: (block0 + i, 0)),
                  pl.BlockSpec((1, D_MODEL), lambda i: (0, 0))],
        out_specs=pl.BlockSpec((tm, D_MODEL), lambda i: (i, 0)),
        out_shape=jax.ShapeDtypeStruct((rows, D_MODEL), F32),
        compiler_params=_cparams("parallel"),
        name="final_norm",
    )(x, g)


def _rope_tables():
    half = RET_DK // 2
    inv = ROPE_BASE ** (-jnp.arange(half, dtype=F32) / half)
    pos = jnp.concatenate([jnp.tile(jnp.arange(SEQ, dtype=jnp.int32), BATCH),
                           jnp.full((M_S,), PAST_LEN, jnp.int32)])
    ang = pos.astype(F32)[:, None] * inv[None, :]
    cos = jnp.cos(ang)
    sin = jnp.sin(ang)
    return jnp.concatenate([cos, cos], axis=1), jnp.concatenate([-sin, sin], axis=1)


def _retention_tables():
    c = RET_CHUNK
    log_g = jnp.log1p(-jnp.exp2(-5.0 - jnp.arange(RET_HEADS, dtype=F32)))
    i = jnp.arange(c, dtype=F32)
    diff = i[:, None] - i[None, :]
    intra = jnp.where(diff[None] >= 0,
                      jnp.exp(jnp.maximum(diff, 0.0)[None] * log_g[:, None, None]), 0.0)
    q_dec = jnp.exp((i + 1.0)[None, :] * log_g[:, None])
    k_dec = jnp.exp((c - 1.0 - i)[None, :] * log_g[:, None])
    c_dec = jnp.exp(c * log_g)
    full = lambda a: jnp.broadcast_to(a, (RET_HEADS, c, RET_DV))
    g1 = jnp.broadcast_to(jnp.exp(1.0 * log_g)[:, None, None], (RET_HEADS, 1, RET_DV))
    return (intra, full(q_dec[:, :, None]), full(k_dec[:, :, None]), full(c_dec[:, None, None])), g1


def _s5_block_weights(bbr_t, bbi_t, c_re, c_im):
    ch_group = jnp.arange(S5_BLK_CH, dtype=jnp.int32) // S5_GROUP_CH
    st_group = jnp.arange(S5_BLK_ST, dtype=jnp.int32) // S5_STATE
    drive_mask = ch_group[:, None] == st_group[None, :]

    def drive(bt):
        a = bt.reshape(DEPTH, S5_NBLK, S5_BLK_CH, 1, S5_STATE)
        a = jnp.broadcast_to(a, (DEPTH, S5_NBLK, S5_BLK_CH, S5_GB, S5_STATE))
        return jnp.where(drive_mask, a.reshape(DEPTH, S5_NBLK, S5_BLK_CH, S5_BLK_ST), 0.0)

    def readout(cm):
        a = cm.reshape(DEPTH, S5_NBLK, S5_GB, S5_GROUP_CH, S5_STATE)
        a = a.transpose(0, 1, 4, 2, 3).reshape(DEPTH, S5_NBLK, 1, S5_STATE, S5_BLK_CH)
        a = jnp.broadcast_to(a, (DEPTH, S5_NBLK, S5_GB, S5_STATE, S5_BLK_CH))
        return jnp.where(drive_mask.T, a.reshape(DEPTH, S5_NBLK, S5_BLK_ST, S5_BLK_CH), 0.0)

    w_blk = jnp.concatenate([drive(bbr_t), drive(bbi_t)], axis=3).astype(BF16)
    cw_blk = jnp.concatenate([readout(c_re), readout(-c_im)], axis=2).astype(BF16)
    return w_blk, cw_blk


def _slab_tiles(lb):
    a = lb.reshape(DEPTH, 2, S5_NSLAB, LANES).transpose(0, 2, 1, 3)
    return jnp.tile(a, (1, 1, BATCH, 1))


def kernel(x_prompt, x_sample, state_ret, state_s5_re, state_s5_im, ffn1_norm, ffn1_w1, ffn1_w3, ffn1_w2, mix_norm, w_in, ret_gn, ret_proj, s5_lam_re, s5_lam_im, s5_b_re, s5_b_im, s5_c_re, s5_c_im, s5_d, s5_log_step, glu_w, glu_b, s5_proj, w_out, ffn2_norm, ffn2_w1, ffn2_w3, ffn2_w2, final_norm):
    cos_t, sin_t = _rope_tables()
    ret_tabs, g1 = _retention_tables()
    x = jnp.concatenate([x_prompt.reshape(M_P, D_MODEL), x_sample.reshape(M_S, D_MODEL)], axis=0)

    ret_p, s5r_p, s5i_p, s5r_s, s5i_s = [], [], [], [], []
    ret_s = None
    nst = S5_GROUPS * S5_STATE
    rows3 = lambda a: a.reshape(DEPTH, 1, -1)
    ffn1_g, mix_g, ffn2_g, glu_b3 = rows3(ffn1_norm), rows3(mix_norm), rows3(ffn2_norm), rows3(glu_b)
    gn4 = ret_gn.reshape(DEPTH, RET_HEADS, 1, RET_DV)

    lbr_x, lbi_x, bbr_t, bbi_t = _s5_discretize(s5_lam_re, s5_lam_im, s5_log_step, s5_b_re, s5_b_im)
    lbr = lbr_x[::S5_GROUP_CH].reshape(DEPTH, nst)
    lbi = lbi_x[::S5_GROUP_CH].reshape(DEPTH, nst)
    w_blk, cw_blk = _s5_block_weights(bbr_t, bbi_t, s5_c_re, s5_c_im)
    lamr_t, lami_t = _slab_tiles(lbr), _slab_tiles(lbi)
    lbr3, lbi3 = lbr.reshape(DEPTH, 1, nst), lbi.reshape(DEPTH, 1, nst)
    dvec3 = s5_d.reshape(DEPTH, 1, S5_W)
    h0r_all = state_s5_re.reshape(DEPTH, M_S, nst)
    h0i_all = state_s5_im.reshape(DEPTH, M_S, nst)

    xn = _norm(x, ffn1_g, 0)
    for l in range(DEPTH):
        hid, w2_16 = _ffn_up(xn, ffn1_w1, ffn1_w3, ffn1_w2, l)
        x, xn = _proj_res(hid, w2_16, x, 0, scale=0.5, tm=TM_DOWN, name="ffn_down",
                          gain=mix_g, gain_layer=l)

        proj, glu_w16, s5_proj16, ret_proj16, w_out16 = _inproj(
            xn, w_in, cos_t, sin_t, (glu_w, s5_proj, ret_proj, w_out), l)

        z5, hre_p, him_p, o_all, s_p = _mixer_prompt(proj, w_blk, cw_blk, lamr_t, lami_t, dvec3,
                                                      ret_tabs, gn4, l)
        o_s, ret_s = _ret_sample(proj, state_ret, g1, gn4, l, ret_s)
        o_all = lax.dynamic_update_slice(o_all, o_s.astype(BF16), (M_P, 0))

        z_s, hre_s, him_s = _s5_sample(proj, h0r_all, h0i_all, w_blk, cw_blk, lbr3, lbi3, dvec3, l)
        z = lax.dynamic_update_slice(z5.reshape((BATCH + 1) * SEQ, S5_W), z_s, (M_P, 0))

        m = _merge(z, o_all, proj, glu_w16, glu_b3, s5_proj16, ret_proj16, l)
        x, xn = _proj_res(m, w_out16, x, 0, scale=1.0, tm=TM_MERGE, name="out_proj",
                          gain=ffn2_g, gain_layer=l)

        hid, w2_16 = _ffn_up(xn, ffn2_w1, ffn2_w3, ffn2_w2, l)
        if l + 1 < DEPTH:
            x, xn = _proj_res(hid, w2_16, x, 0, scale=0.5, tm=TM_DOWN, name="ffn_down",
                              gain=ffn1_g, gain_layer=l + 1)
        else:
            x, = _proj_res(hid, w2_16, x, 0, scale=0.5, tm=TM_DOWN, name="ffn_down")

        unslab = lambda a: (a.transpose(1, 0, 2).reshape(BATCH, 2, S5_HALF_ST)
                            .reshape(BATCH, S5_GROUPS, S5_STATE))
        ret_p.append(s_p)
        s5r_p.append(unslab(hre_p))
        s5i_p.append(unslab(him_p))
        s5r_s.append(hre_s.reshape(M_S, S5_GROUPS, S5_STATE))
        s5i_s.append(him_s.reshape(M_S, S5_GROUPS, S5_STATE))

    g_fin = final_norm.reshape(1, D_MODEL)
    y_p = _final_norm(x, g_fin, M_P, 1024, 0)
    y_s = _final_norm(x, g_fin, M_S, M_S, M_P // M_S)
    return (y_p.reshape(BATCH, SEQ, D_MODEL), y_s.reshape(M_S, 1, D_MODEL),
            jnp.stack(ret_p), jnp.stack(s5r_p), jnp.stack(s5i_p),
            ret_s, jnp.stack(s5r_s), jnp.stack(s5i_s))
```

```python
import functools
import math

import jax
import jax.numpy as jnp
from jax import lax
from jax.experimental import pallas as pl
from jax.experimental.pallas import tpu as pltpu

F32 = jnp.float32
BF16 = jnp.bfloat16

D_MODEL = 2048
BATCH = 4
SEQ = 2048
DEPTH = 4
DEC_BATCH = 128
PAST_LEN = 16384
RET_HEADS = 8
RET_DK = 128
RET_DV = 128
RET_QK = RET_HEADS * RET_DK
RET_W = RET_HEADS * RET_DV
RET_CHUNK = 128
ROPE_BASE = 10000.0
S5_GROUPS = 64
S5_GROUP_CH = 16
S5_W = S5_GROUPS * S5_GROUP_CH
S5_STATE = 64
D_FF = 5632
IN_W = 2 * RET_QK + 2 * RET_W + S5_W + 2 * D_MODEL
NORM_EPS = 1e-6
HEAD_NORM_EPS = 1e-5

M_P = BATCH * SEQ
M_S = DEC_BATCH
M_ALL = M_P + M_S

LANES = 128
SUBLANES = 8
VMEM_LIMIT = 58 * 1024 * 1024

TM = 1040
TN_FF = 512
TM_DOWN = 416
COL = 1024
TM_BIG = 4 * TM
TN_IN = 512
IN_SUB = 256

S5_TC = 64
S5_PAD = SUBLANES // 2
S5_PITCH = S5_TC + S5_PAD
S5_GB = 16
S5_NBLK = S5_GROUPS // S5_GB
S5_BLK_CH = S5_GB * S5_GROUP_CH
S5_BLK_ST = S5_GB * S5_STATE
S5_HALF_ST = S5_GROUPS * S5_STATE // 2
S5_NSLAB = S5_HALF_ST // LANES
S5_NSEQ = BATCH * 2
S5_SLAB_GROUP = 4


def _cparams(*sem):
    return pltpu.CompilerParams(dimension_semantics=sem, vmem_limit_bytes=VMEM_LIMIT)


def _rms_to_bf16(x, g):
    ms = jnp.mean(x * x, axis=-1, keepdims=True)
    return ((x * lax.rsqrt(ms + NORM_EPS)) * g).astype(BF16)


def _layer_spec(l, k, tn):
    return pl.BlockSpec((None, k, tn), lambda i, j: (l, 0, j))


def _layer_row_spec(l, n):
    return pl.BlockSpec((None, 1, n), lambda *_: (l, 0, 0))


def _norm_kernel(x_ref, g_ref, o_ref):
    o_ref[...] = _rms_to_bf16(x_ref[...], g_ref[...])


def _norm(x, g, l):
    return pl.pallas_call(
        _norm_kernel,
        grid=(M_ALL // TM,),
        in_specs=[pl.BlockSpec((TM, D_MODEL), lambda i: (i, 0)), _layer_row_spec(l, D_MODEL)],
        out_specs=pl.BlockSpec((TM, D_MODEL), lambda i: (i, 0)),
        out_shape=jax.ShapeDtypeStruct((M_ALL, D_MODEL), BF16),
        compiler_params=_cparams("parallel"),
        name="norm",
    )(x, g)


def _resident_rows_spec():
    return pl.BlockSpec((TM_BIG, D_MODEL), lambda r, j: (r, 0), pipeline_mode=pl.Buffered(1))


def _ffn_up_kernel(xn_ref, w1_ref, w3_ref, w2_ref, o_ref, w2_16_ref):
    w1 = w1_ref[...].astype(BF16)
    w3 = w3_ref[...].astype(BF16)
    for s in range(TM_BIG // TM):
        rows = slice(s * TM, (s + 1) * TM)
        xs = xn_ref[rows, :]
        for c in range(TN_FF // IN_SUB):
            cols = slice(c * IN_SUB, (c + 1) * IN_SUB)
            h1 = jnp.dot(xs, w1[:, cols], preferred_element_type=F32)
            h3 = jnp.dot(xs, w3[:, cols], preferred_element_type=F32)
            o_ref[rows, cols] = ((h1 * jax.nn.sigmoid(h1)) * h3).astype(BF16)
    w2_16_ref[...] = w2_ref[...].astype(BF16)


def _ffn_up(xn, w1, w3, w2, l):
    ncol = D_FF // TN_FF
    nsteps = (M_ALL // TM_BIG) * ncol
    chunk = D_FF // nsteps
    return pl.pallas_call(
        _ffn_up_kernel,
        grid=(M_ALL // TM_BIG, ncol),
        in_specs=[
            _resident_rows_spec(),
            _layer_spec(l, D_MODEL, TN_FF),
            _layer_spec(l, D_MODEL, TN_FF),
            pl.BlockSpec((None, chunk, D_MODEL), lambda r, j: (l, r * ncol + j, 0)),
        ],
        out_specs=[
            pl.BlockSpec((TM_BIG, TN_FF), lambda r, j: (r, j)),
            pl.BlockSpec((None, chunk, D_MODEL), lambda r, j: (0, r * ncol + j, 0)),
        ],
        out_shape=[
            jax.ShapeDtypeStruct((M_ALL, D_FF), BF16),
            jax.ShapeDtypeStruct((1, D_FF, D_MODEL), BF16),
        ],
        compiler_params=_cparams("parallel", "arbitrary"),
        name="ffn_up",
    )(xn, w1, w3, w2)


def _resident_layer_spec(l, shape):
    return pl.BlockSpec((None,) + shape, lambda i: (l, 0, 0), pipeline_mode=pl.Buffered(1))


def _proj_res_kernel(a_ref, w_ref, x_ref, *rest, scale):
    o_ref = rest[-2] if len(rest) == 3 else rest[0]
    for c in range(D_MODEL // COL):
        cols = slice(c * COL, (c + 1) * COL)
        acc = jnp.dot(a_ref[...], w_ref[:, cols], preferred_element_type=F32)
        o_ref[:, cols] = x_ref[:, cols] + scale * acc
    if len(rest) == 3:
        rest[2][...] = _rms_to_bf16(o_ref[...], rest[0][...])


def _proj_res(a, w16, x, l, *, scale, tm, name, gain=None, gain_layer=None):
    k = a.shape[1]
    row_spec = lambda: pl.BlockSpec((tm, D_MODEL), lambda i: (i, 0))
    in_specs = [pl.BlockSpec((tm, k), lambda i: (i, 0)), _resident_layer_spec(l, (k, D_MODEL)),
                row_spec()]
    args = [a, w16, x]
    out_specs = [row_spec()]
    out_shape = [jax.ShapeDtypeStruct((M_ALL, D_MODEL), F32)]
    if gain is not None:
        in_specs.append(_layer_row_spec(gain_layer, D_MODEL))
        args.append(gain)
        out_specs.append(row_spec())
        out_shape.append(jax.ShapeDtypeStruct((M_ALL, D_MODEL), BF16))
    return pl.pallas_call(
        functools.partial(_proj_res_kernel, scale=scale),
        grid=(M_ALL // tm,),
        in_specs=in_specs,
        out_specs=out_specs,
        out_shape=out_shape,
        compiler_params=_cparams("parallel"),
        name=name,
    )(*args)


def _inproj_kernel(xn_ref, w_ref, cos_ref, sin_ref, *rest):
    o_ref = rest[len(rest) // 2]
    for src, dst in zip(rest[:len(rest) // 2], rest[len(rest) // 2 + 1:]):
        dst[...] = src[...].astype(BF16)

    seg = pl.program_id(1) // (COL // TN_IN)

    def project(epilogue):
        wb = [w_ref[:, c * IN_SUB:(c + 1) * IN_SUB].astype(BF16) for c in range(TN_IN // IN_SUB)]
        for s in range(TM_BIG // TM):
            rows = slice(s * TM, (s + 1) * TM)
            for c in range(TN_IN // IN_SUB):
                acc = jnp.dot(xn_ref[rows, :], wb[c], preferred_element_type=F32)
                o_ref[rows, c * IN_SUB:(c + 1) * IN_SUB] = epilogue(acc, rows)

    def rotary(acc, rows):
        scale = jnp.where(seg == 0, RET_DK ** -0.5, 1.0).astype(F32)
        cos = cos_ref[rows, :]
        sin = sin_ref[rows, :]
        heads = []
        for h in range(IN_SUB // RET_DK):
            a = acc[:, h * RET_DK:(h + 1) * RET_DK]
            heads.append((a * cos + pltpu.roll(a, RET_DK // 2, axis=1) * sin) * scale)
        return jnp.concatenate(heads, axis=1)

    @pl.when(seg <= 1)
    def _():
        project(rotary)

    @pl.when((seg == 2) | (seg == 4))
    def _():
        project(lambda acc, rows: acc)

    @pl.when(seg == 3)
    def _():
        project(lambda acc, rows: acc * jax.nn.sigmoid(acc))

    @pl.when(seg >= 5)
    def _():
        project(lambda acc, rows: jax.nn.sigmoid(acc))


CAST_STEPS = 32


def _inproj(xn, w, cos_t, sin_t, small_weights, l):
    ncol = IN_W // TN_IN
    tab_spec = pl.BlockSpec((TM_BIG, RET_DK), lambda r, j: (r, 0), pipeline_mode=pl.Buffered(1))
    chunk_idx = lambda r, j: jnp.minimum(r * ncol + j, CAST_STEPS - 1)
    cast_in, cast_out, cast_shapes = [], [], []
    for a in small_weights:
        rows, cols = a.shape[1], a.shape[2]
        blk = (None, rows // CAST_STEPS, cols)
        cast_in.append(pl.BlockSpec(blk, lambda r, j: (l, chunk_idx(r, j), 0)))
        cast_out.append(pl.BlockSpec(blk, lambda r, j: (0, chunk_idx(r, j), 0)))
        cast_shapes.append(jax.ShapeDtypeStruct((1, rows, cols), BF16))
    return pl.pallas_call(
        _inproj_kernel,
        grid=(M_ALL // TM_BIG, ncol),
        in_specs=[
            _resident_rows_spec(),
            _layer_spec(l, D_MODEL, TN_IN),
            tab_spec, tab_spec,
        ] + cast_in,
        out_specs=[pl.BlockSpec((TM_BIG, TN_IN), lambda r, j: (r, j))] + cast_out,
        out_shape=[jax.ShapeDtypeStruct((M_ALL, IN_W), F32)] + cast_shapes,
        compiler_params=_cparams("parallel", "arbitrary"),
        name="inproj",
    )(xn, w, cos_t, sin_t, *small_weights)


def _head_norm(o):
    mu = jnp.mean(o, axis=-1, keepdims=True)
    d = o - mu
    var = jnp.mean(d * d, axis=-1, keepdims=True)
    return d * lax.rsqrt(var + HEAD_NORM_EPS)


RET_RC = 2 * RET_CHUNK


def _ret_prompt_body(q_ref, k_ref, v_ref, rg_ref, intra_ref, qd_ref, kd_ref, cd_ref, gn_ref,
                       o_ref, st_ref):
    nc = RET_RC // RET_CHUNK
    blk = lambda c, h: (slice(c * RET_CHUNK, (c + 1) * RET_CHUNK), slice(h * RET_DK, (h + 1) * RET_DK))

    o_intra, kv = {}, {}
    for c in range(nc):
        for h in range(RET_HEADS):
            qc = q_ref[blk(c, h)]
            kc = k_ref[blk(c, h)]
            vb = v_ref[blk(c, h)].astype(BF16)
            sc = lax.dot_general(qc.astype(BF16), kc.astype(BF16), (((1,), (1,)), ((), ())),
                                 preferred_element_type=F32) * intra_ref[h]
            o_intra[c, h] = jnp.dot(sc.astype(BF16), vb, preferred_element_type=F32)
            kt = jnp.transpose(kc * kd_ref[h]).astype(BF16)
            kv[c, h] = jnp.dot(kt, vb, preferred_element_type=F32)

    for h in range(RET_HEADS):
        s = st_ref[h]
        for c in range(nc):
            qs = jnp.dot((q_ref[blk(c, h)] * qd_ref[h]).astype(BF16), s.astype(BF16),
                         preferred_element_type=F32)
            o = o_intra[c, h] + qs
            s = s * cd_ref[h] + kv[c, h]
            on = (_head_norm(o) * gn_ref[h]) * rg_ref[blk(c, h)]
            o_ref[blk(c, h)] = on.astype(BF16)
        st_ref[h] = s


RET_SB = 8


def _col_bcast(row, eye, ones2):
    hi = row.astype(BF16)
    lo = (row - hi.astype(F32)).astype(BF16)
    parts = [jnp.where(eye, jnp.broadcast_to(t.astype(F32), (RET_DK, RET_DK)), 0.0).astype(BF16)
             for t in (hi, lo)]
    return jnp.dot(jnp.concatenate(parts, axis=1), ones2, preferred_element_type=F32)


def _ret_sample_kernel(q_ref, k_ref, v_ref, rg_ref, s_ref, g1_ref, gn_ref, *rest):
    o_ref, so_ref = rest[-2:]
    eye = (lax.broadcasted_iota(jnp.int32, (RET_DK, RET_DK), 0)
           == lax.broadcasted_iota(jnp.int32, (RET_DK, RET_DK), 1))
    ones2 = jnp.ones((2 * RET_DK, RET_DV), BF16)

    for b in range(RET_SB):
        row = slice(b, b + 1)
        for h in range(RET_HEADS):
            cols = slice(h * RET_DK, (h + 1) * RET_DK)
            kcol = _col_bcast(k_ref[row, cols], eye, ones2)
            qcol = _col_bcast(q_ref[row, cols], eye, ones2)
            s_new = s_ref[b, h] * g1_ref[h] + kcol * v_ref[row, cols]
            so_ref[b, h] = s_new
            o = jnp.sum(qcol * s_new, axis=0, keepdims=True)
            o_ref[row, cols] = (_head_norm(o) * gn_ref[h]) * rg_ref[row, cols]


def _ret_sample(proj, state, g1, gn, l, prev_states):
    row0 = M_P // RET_SB

    def seg_spec(seg):
        return pl.BlockSpec((RET_SB, COL), lambda i, seg=seg: (row0 + i, seg))

    st_spec = pl.BlockSpec((None, RET_SB, RET_HEADS, RET_DK, RET_DV), lambda i: (l, i, 0, 0, 0))
    in_specs = [
        seg_spec(0), seg_spec(1), seg_spec(2), seg_spec(3), st_spec,
        pl.BlockSpec((RET_HEADS, 1, RET_DV), lambda i: (0, 0, 0)),
        pl.BlockSpec((None, RET_HEADS, 1, RET_DV), lambda i: (l, 0, 0, 0)),
    ]
    args = [proj, proj, proj, proj, state, g1, gn]
    aliases = {}
    if prev_states is not None:
        in_specs.append(pl.BlockSpec(memory_space=pl.ANY))
        args.append(prev_states)
        aliases = {len(args) - 1: 1}
    return pl.pallas_call(
        _ret_sample_kernel,
        grid=(M_S // RET_SB,),
        in_specs=in_specs,
        out_specs=[pl.BlockSpec((RET_SB, RET_W), lambda i: (i, 0)), st_spec],
        out_shape=[
            jax.ShapeDtypeStruct((M_S, RET_W), F32),
            jax.ShapeDtypeStruct((DEPTH, M_S, RET_HEADS, RET_DK, RET_DV), F32),
        ],
        input_output_aliases=aliases,
        compiler_params=_cparams("parallel"),
        name="ret_sample",
    )(*args)


def _s5_disc_kernel(lr_ref, li_ref, ls_ref, br_ref, bi_ref, lbr_ref, lbi_ref, bbr_ref, bbi_ref):
    lam_r = lr_ref[...]
    lam_i = li_ref[...]
    dt = jnp.exp(ls_ref[...])
    ea = jnp.exp(lam_r * dt)
    lb_r = ea * jnp.cos(lam_i * dt)
    lb_i = ea * jnp.sin(lam_i * dt)
    lbr_ref[...] = lb_r
    lbi_ref[...] = lb_i
    nr = lb_r - 1.0
    den = lam_r * lam_r + lam_i * lam_i
    cr = (nr * lam_r + lb_i * lam_i) / den
    ci = (lb_i * lam_r - nr * lam_i) / den
    b_r = br_ref[...]
    b_i = bi_ref[...]
    bbr_ref[...] = cr * b_r - ci * b_i
    bbi_ref[...] = cr * b_i + ci * b_r


def _s5_discretize(lam_re, lam_im, log_step, b_re, b_im):
    rows = DEPTH * S5_GROUPS * S5_GROUP_CH
    rep = lambda a: jnp.repeat(a.reshape(DEPTH * S5_GROUPS, S5_STATE), S5_GROUP_CH, axis=0)
    tr = lambda a: jnp.transpose(a, (0, 1, 3, 2)).reshape(rows, S5_STATE)
    shp = jax.ShapeDtypeStruct((rows, S5_STATE), F32)
    ls = jnp.broadcast_to(log_step[:, :, None], (DEPTH, S5_GROUPS, S5_STATE))
    spec = pl.BlockSpec((rows // DEPTH, S5_STATE), lambda l: (l, 0))
    return pl.pallas_call(
        _s5_disc_kernel,
        grid=(DEPTH,),
        in_specs=[spec] * 5,
        out_specs=[spec] * 4,
        out_shape=[shp, shp, shp, shp],
        compiler_params=_cparams("parallel"),
        name="s5_disc",
    )(rep(lam_re), rep(lam_im), rep(ls), tr(b_re), tr(b_im))


def _gelu_tanh(y):
    c = math.sqrt(2.0 / math.pi)
    return 0.5 * y * (1.0 + jnp.tanh(c * (y + 0.044715 * (y * y * y))))


def _s5_prompt_body(u0_ref, u1_ref, u2_ref, u3_ref, w_ref, cw_ref, lamr_ref, lami_ref, d_ref,
                    z_ref, hre_ref, him_ref, bre_ref, bim_ref):
    u_refs = (u0_ref, u1_ref, u2_ref, u3_ref)

    nsl = S5_BLK_ST // LANES
    zpad = jnp.zeros((S5_PAD, S5_BLK_CH), F32)

    for gh in range(2):
        seg = S5_TC + 2 * S5_PAD * gh
        for kb in range(2):
            blk = gh * 2 + kb
            c0 = blk * S5_BLK_CH
            parts = []
            for u in u_refs:
                ub = u[:, c0:c0 + S5_BLK_CH]
                parts.extend([zpad, ub, zpad] if gh else [ub])
            x = jnp.concatenate(parts, axis=0).astype(BF16)
            res = jnp.dot(x, w_ref[blk], preferred_element_type=F32)
            for b in range(BATCH):
                r0 = (b * 2 + gh) * S5_PITCH - S5_PAD * gh
                part = res[b * seg:(b + 1) * seg]
                for s in range(nsl):
                    slab = kb * nsl + s
                    bre_ref[slab, r0:r0 + seg, :] = part[:, s * LANES:(s + 1) * LANES]
                    bim_ref[slab, r0:r0 + seg, :] = part[:, S5_BLK_ST + s * LANES:S5_BLK_ST + (s + 1) * LANES]

    for sg in range(S5_NSLAB // S5_SLAB_GROUP):
        slabs = [sg * S5_SLAB_GROUP + i for i in range(S5_SLAB_GROUP)]
        lam_r = [lamr_ref[s] for s in slabs]
        lam_i = [lami_ref[s] for s in slabs]

        def step(t, carry, slabs=slabs, lam_r=lam_r, lam_i=lam_i):
            hr, hi = carry
            nr_all = []
            ni_all = []
            for i, s in enumerate(slabs):
                idx = pl.ds(t, S5_NSEQ, stride=S5_PITCH)
                br = bre_ref[s, idx, :]
                bi = bim_ref[s, idx, :]
                nr = lam_r[i] * hr[i] - lam_i[i] * hi[i] + br
                ni = lam_r[i] * hi[i] + lam_i[i] * hr[i] + bi
                bre_ref[s, idx, :] = nr
                bim_ref[s, idx, :] = ni
                nr_all.append(nr)
                ni_all.append(ni)
            return tuple(nr_all), tuple(ni_all)

        hr, hi = (tuple(hre_ref[s] for s in slabs), tuple(him_ref[s] for s in slabs))
        for t in range(S5_TC):
            hr, hi = step(t, (hr, hi))
        for i, s in enumerate(slabs):
            hre_ref[s] = hr[i]
            him_ref[s] = hi[i]

    for gh in range(2):
        seg = S5_TC + 2 * S5_PAD * gh
        for kb in range(2):
            blk = gh * 2 + kb
            c0 = blk * S5_BLK_CH
            pieces = []
            for src in (bre_ref, bim_ref):
                for s in range(nsl):
                    slab = kb * nsl + s
                    starts = [(b * 2 + gh) * S5_PITCH - S5_PAD * gh for b in range(BATCH)]
                    pieces.append(jnp.concatenate(
                        [src[slab, r0:r0 + seg, :] for r0 in starts], axis=0))
            hcat = jnp.concatenate(pieces, axis=1).astype(BF16)
            y = jnp.dot(hcat, cw_ref[blk], preferred_element_type=F32)
            dvec = d_ref[:, c0:c0 + S5_BLK_CH]
            for b in range(BATCH):
                lo = b * seg + S5_PAD * gh
                yb = y[lo:lo + S5_TC] + dvec * u_refs[b][:, c0:c0 + S5_BLK_CH]
                z_ref[b, :, c0:c0 + S5_BLK_CH] = _gelu_tanh(yb)


RET_STEPS = SEQ // RET_RC
assert BATCH * RET_STEPS == SEQ // S5_TC


def _mixer_prompt_kernel(u0_ref, u1_ref, u2_ref, u3_ref, w_ref, cw_ref, lamr_ref, lami_ref, d_ref,
                         q_ref, k_ref, v_ref, rg_ref, intra_ref, qd_ref, kd_ref, cd_ref, gn_ref,
                         z_ref, hre_ref, him_ref, o_ref, s_ref, bre_ref, bim_ref, st_ref):
    j = pl.program_id(0)
    rb = j % RET_STEPS

    @pl.when(j == 0)
    def _():
        hre_ref[...] = jnp.zeros_like(hre_ref)
        him_ref[...] = jnp.zeros_like(him_ref)

    @pl.when(rb == 0)
    def _():
        st_ref[...] = jnp.zeros_like(st_ref)

    _s5_prompt_body(u0_ref, u1_ref, u2_ref, u3_ref, w_ref, cw_ref, lamr_ref, lami_ref, d_ref,
                    z_ref, hre_ref, him_ref, bre_ref, bim_ref)
    _ret_prompt_body(q_ref, k_ref, v_ref, rg_ref, intra_ref, qd_ref, kd_ref, cd_ref, gn_ref,
                       o_ref, st_ref)

    @pl.when(rb == RET_STEPS - 1)
    def _():
        s_ref[...] = st_ref[...]


def _mixer_prompt(proj, w_blk, cw_blk, lamr_t, lami_t, dvec, ret_tabs, gn, l):
    nsteps = SEQ // S5_TC
    su_col = (2 * RET_QK + 2 * RET_W) // S5_W
    intra, qd, kd, cd = ret_tabs

    def u_spec(b):
        return pl.BlockSpec((S5_TC, S5_W), lambda j, b=b: (b * nsteps + j, su_col))

    def seg_spec(seg):
        return pl.BlockSpec((RET_RC, COL), lambda j, seg=seg: (j, seg))

    layer4 = lambda shape: pl.BlockSpec((None,) + shape, lambda j: (l, 0, 0, 0))
    full3 = lambda shape: pl.BlockSpec(shape, lambda j: (0, 0, 0))
    st_shape = (S5_NSLAB, S5_NSEQ, LANES)
    tab_spec = full3((RET_HEADS, RET_CHUNK, RET_CHUNK))
    return pl.pallas_call(
        _mixer_prompt_kernel,
        grid=(nsteps,),
        in_specs=[
            u_spec(0), u_spec(1), u_spec(2), u_spec(3),
            layer4((S5_NBLK, S5_BLK_CH, 2 * S5_BLK_ST)),
            layer4((S5_NBLK, 2 * S5_BLK_ST, S5_BLK_CH)),
            layer4(st_shape), layer4(st_shape),
            _layer_row_spec(l, S5_W),
            seg_spec(0), seg_spec(1), seg_spec(2), seg_spec(3),
            tab_spec, tab_spec, tab_spec, tab_spec,
            layer4((RET_HEADS, 1, RET_DV)),
        ],
        out_specs=[
            pl.BlockSpec((BATCH, S5_TC, S5_W), lambda j: (0, j, 0)),
            full3(st_shape), full3(st_shape),
            pl.BlockSpec((RET_RC, RET_W), lambda j: (j, 0)),
            pl.BlockSpec((None, RET_HEADS, RET_DK, RET_DV), lambda j: (j // RET_STEPS, 0, 0, 0)),
        ],
        out_shape=[
            jax.ShapeDtypeStruct((BATCH + 1, SEQ, S5_W), F32),
            jax.ShapeDtypeStruct(st_shape, F32),
            jax.ShapeDtypeStruct(st_shape, F32),
            jax.ShapeDtypeStruct((M_ALL, RET_W), BF16),
            jax.ShapeDtypeStruct((BATCH, RET_HEADS, RET_DK, RET_DV), F32),
        ],
        scratch_shapes=[
            pltpu.VMEM((S5_NSLAB, S5_NSEQ * S5_PITCH, LANES), F32),
            pltpu.VMEM((S5_NSLAB, S5_NSEQ * S5_PITCH, LANES), F32),
            pltpu.VMEM((RET_HEADS, RET_DK, RET_DV), F32),
        ],
        compiler_params=_cparams("arbitrary"),
        name="mixer_prompt",
    )(proj, proj, proj, proj, w_blk, cw_blk, lamr_t, lami_t, dvec,
      proj, proj, proj, proj, intra, qd, kd, cd, gn)


def _s5_sample_kernel(u_ref, h0r_ref, h0i_ref, w_ref, cw_ref, lbr_ref, lbi_ref, d_ref,
                      z_ref, hr_ref, hi_ref):
    for blk in range(S5_NBLK):
        c0 = blk * S5_BLK_CH
        s0 = blk * S5_BLK_ST
        u = u_ref[:, c0:c0 + S5_BLK_CH]
        res = jnp.dot(u.astype(BF16), w_ref[blk], preferred_element_type=F32)
        lr = lbr_ref[:, s0:s0 + S5_BLK_ST]
        li = lbi_ref[:, s0:s0 + S5_BLK_ST]
        h0r = h0r_ref[:, s0:s0 + S5_BLK_ST]
        h0i = h0i_ref[:, s0:s0 + S5_BLK_ST]
        hr = lr * h0r - li * h0i + res[:, :S5_BLK_ST]
        hi = lr * h0i + li * h0r + res[:, S5_BLK_ST:]
        hr_ref[:, s0:s0 + S5_BLK_ST] = hr
        hi_ref[:, s0:s0 + S5_BLK_ST] = hi
        hcat = jnp.concatenate([hr, hi], axis=1).astype(BF16)
        y = jnp.dot(hcat, cw_ref[blk], preferred_element_type=F32) + d_ref[:, c0:c0 + S5_BLK_CH] * u
        z_ref[:, c0:c0 + S5_BLK_CH] = _gelu_tanh(y)


def _s5_sample(proj, h0r, h0i, w_blk, cw_blk, lbr, lbi, dvec, l):
    nst = S5_GROUPS * S5_STATE
    su_col = (2 * RET_QK + 2 * RET_W) // S5_W
    layer3 = lambda shape: pl.BlockSpec((None,) + shape, lambda i: (l, 0, 0))
    layer4 = lambda shape: pl.BlockSpec((None,) + shape, lambda i: (l, 0, 0, 0))
    whole = lambda shape: pl.BlockSpec(shape, lambda i: (0, 0))
    return pl.pallas_call(
        _s5_sample_kernel,
        grid=(1,),
        in_specs=[
            pl.BlockSpec((M_S, S5_W), lambda i: (M_P // M_S, su_col)),
            layer3((M_S, nst)), layer3((M_S, nst)),
            layer4((S5_NBLK, S5_BLK_CH, 2 * S5_BLK_ST)),
            layer4((S5_NBLK, 2 * S5_BLK_ST, S5_BLK_CH)),
            layer3((1, nst)), layer3((1, nst)),
            _layer_row_spec(l, S5_W),
        ],
        out_specs=[whole((M_S, S5_W)), whole((M_S, nst)), whole((M_S, nst))],
        out_shape=[
            jax.ShapeDtypeStruct((M_S, S5_W), F32),
            jax.ShapeDtypeStruct((M_S, nst), F32),
            jax.ShapeDtypeStruct((M_S, nst), F32),
        ],
        compiler_params=_cparams("arbitrary"),
        name="s5_sample",
    )(proj, h0r, h0i, w_blk, cw_blk, lbr, lbi, dvec)


TM_MERGE = 520


def _merge_kernel(z_ref, o_ref, gr0_ref, gr1_ref, gs0_ref, gs1_ref, gluw_ref, glub_ref, s5p_ref,
                  retp_ref, m_ref):
    z = z_ref[...]
    t = jnp.dot(z.astype(BF16), gluw_ref[...], preferred_element_type=F32) + glub_ref[...]
    z2 = (z * jax.nn.sigmoid(t)).astype(BF16)
    o = o_ref[...]
    for c, (gr_ref, gs_ref) in enumerate(((gr0_ref, gs0_ref), (gr1_ref, gs1_ref))):
        cols = slice(c * COL, (c + 1) * COL)
        b_s5 = jnp.dot(z2, s5p_ref[:, cols], preferred_element_type=F32)
        b_ret = jnp.dot(o, retp_ref[:, cols], preferred_element_type=F32)
        m_ref[:, cols] = (gr_ref[...] * b_ret + gs_ref[...] * b_s5).astype(BF16)


def _merge(z, o, proj, glu_w, glu_b, s5_proj, ret_proj, l):
    gr_col = (2 * RET_QK + 2 * RET_W + S5_W) // COL
    gs_col = gr_col + D_MODEL // COL
    gate = lambda col: pl.BlockSpec((TM_MERGE, COL), lambda i, col=col: (i, col))
    return pl.pallas_call(
        _merge_kernel,
        grid=(M_ALL // TM_MERGE,),
        in_specs=[
            pl.BlockSpec((TM_MERGE, S5_W), lambda i: (i, 0)),
            pl.BlockSpec((TM_MERGE, RET_W), lambda i: (i, 0)),
            gate(gr_col), gate(gr_col + 1), gate(gs_col), gate(gs_col + 1),
            _resident_layer_spec(0, (S5_W, S5_W)),
            _layer_row_spec(l, S5_W),
            _resident_layer_spec(0, (S5_W, D_MODEL)),
            _resident_layer_spec(0, (RET_W, D_MODEL)),
        ],
        out_specs=pl.BlockSpec((TM_MERGE, D_MODEL), lambda i: (i, 0)),
        out_shape=jax.ShapeDtypeStruct((M_ALL, D_MODEL), BF16),
        compiler_params=_cparams("parallel"),
        name="merge",
    )(z, o, proj, proj, proj, proj, glu_w, glu_b, s5_proj, ret_proj)


def _final_norm_kernel(x_ref, g_ref, o_ref):
    x = x_ref[...]
    ms = jnp.mean(x * x, axis=-1, keepdims=True)
    o_ref[...] = (x * lax.rsqrt(ms + NORM_EPS)) * g_ref[...]


def _final_norm(x, g, rows, tm, block0):
    return pl.pallas_call(
        _final_norm_kernel,
        grid=(rows // tm,),
        in_specs=[pl.BlockSpec((tm, D_MODEL), lambda i: (block0 + i, 0)),
                  pl.BlockSpec((1, D_MODEL), lambda i: (0, 0))],
        out_specs=pl.BlockSpec((tm, D_MODEL), lambda i: (i, 0)),
        out_shape=jax.ShapeDtypeStruct((rows, D_MODEL), F32),
        compiler_params=_cparams("parallel"),
        name="final_norm",
    )(x, g)


def _rope_tables():
    half = RET_DK // 2
    inv = ROPE_BASE ** (-jnp.arange(half, dtype=F32) / half)
    pos = jnp.concatenate([jnp.tile(jnp.arange(SEQ, dtype=jnp.int32), BATCH),
                           jnp.full((M_S,), PAST_LEN, jnp.int32)])
    ang = pos.astype(F32)[:, None] * inv[None, :]
    cos = jnp.cos(ang)
    sin = jnp.sin(ang)
    return jnp.concatenate([cos, cos], axis=1), jnp.concatenate([-sin, sin], axis=1)


def _retention_tables():
    c = RET_CHUNK
    log_g = jnp.log1p(-jnp.exp2(-5.0 - jnp.arange(RET_HEADS, dtype=F32)))
    i = jnp.arange(c, dtype=F32)
    diff = i[:, None] - i[None, :]
    intra = jnp.where(diff[None] >= 0,
                      jnp.exp(jnp.maximum(diff, 0.0)[None] * log_g[:, None, None]), 0.0)
    q_dec = jnp.exp((i + 1.0)[None, :] * log_g[:, None])
    k_dec = jnp.exp((c - 1.0 - i)[None, :] * log_g[:, None])
    c_dec = jnp.exp(c * log_g)
    full = lambda a: jnp.broadcast_to(a, (RET_HEADS, c, RET_DV))
    g1 = jnp.broadcast_to(jnp.exp(1.0 * log_g)[:, None, None], (RET_HEADS, 1, RET_DV))
    return (intra, full(q_dec[:, :, None]), full(k_dec[:, :, None]), full(c_dec[:, None, None])), g1


def _s5_block_weights(bbr_t, bbi_t, c_re, c_im):
    ch_group = jnp.arange(S5_BLK_CH, dtype=jnp.int32) // S5_GROUP_CH
    st_group = jnp.arange(S5_BLK_ST, dtype=jnp.int32) // S5_STATE
    drive_mask = ch_group[:, None] == st_group[None, :]

    def drive(bt):
        a = bt.reshape(DEPTH, S5_NBLK, S5_BLK_CH, 1, S5_STATE)
        a = jnp.broadcast_to(a, (DEPTH, S5_NBLK, S5_BLK_CH, S5_GB, S5_STATE))
        return jnp.where(drive_mask, a.reshape(DEPTH, S5_NBLK, S5_BLK_CH, S5_BLK_ST), 0.0)

    def readout(cm):
        a = cm.reshape(DEPTH, S5_NBLK, S5_GB, S5_GROUP_CH, S5_STATE)
        a = a.transpose(0, 1, 4, 2, 3).reshape(DEPTH, S5_NBLK, 1, S5_STATE, S5_BLK_CH)
        a = jnp.broadcast_to(a, (DEPTH, S5_NBLK, S5_GB, S5_STATE, S5_BLK_CH))
        return jnp.where(drive_mask.T, a.reshape(DEPTH, S5_NBLK, S5_BLK_ST, S5_BLK_CH), 0.0)

    w_blk = jnp.concatenate([drive(bbr_t), drive(bbi_t)], axis=3).astype(BF16)
    cw_blk = jnp.concatenate([readout(c_re), readout(-c_im)], axis=2).astype(BF16)
    return w_blk, cw_blk


def _slab_tiles(lb):
    a = lb.reshape(DEPTH, 2, S5_NSLAB, LANES).transpose(0, 2, 1, 3)
    return jnp.tile(a, (1, 1, BATCH, 1))


def kernel(x_prompt, x_sample, state_ret, state_s5_re, state_s5_im, ffn1_norm, ffn1_w1, ffn1_w3, ffn1_w2, mix_norm, w_in, ret_gn, ret_proj, s5_lam_re, s5_lam_im, s5_b_re, s5_b_im, s5_c_re, s5_c_im, s5_d, s5_log_step, glu_w, glu_b, s5_proj, w_out, ffn2_norm, ffn2_w1, ffn2_w3, ffn2_w2, final_norm):
    cos_t, sin_t = _rope_tables()
    ret_tabs, g1 = _retention_tables()
    x = jnp.concatenate([x_prompt.reshape(M_P, D_MODEL), x_sample.reshape(M_S, D_MODEL)], axis=0)

    ret_p, s5r_p, s5i_p, s5r_s, s5i_s = [], [], [], [], []
    ret_s = None
    nst = S5_GROUPS * S5_STATE
    rows3 = lambda a: a.reshape(DEPTH, 1, -1)
    ffn1_g, mix_g, ffn2_g, glu_b3 = rows3(ffn1_norm), rows3(mix_norm), rows3(ffn2_norm), rows3(glu_b)
    gn4 = ret_gn.reshape(DEPTH, RET_HEADS, 1, RET_DV)

    lbr_x, lbi_x, bbr_t, bbi_t = _s5_discretize(s5_lam_re, s5_lam_im, s5_log_step, s5_b_re, s5_b_im)
    lbr = lbr_x[::S5_GROUP_CH].reshape(DEPTH, nst)
    lbi = lbi_x[::S5_GROUP_CH].reshape(DEPTH, nst)
    w_blk, cw_blk = _s5_block_weights(bbr_t, bbi_t, s5_c_re, s5_c_im)
    lamr_t, lami_t = _slab_tiles(lbr), _slab_tiles(lbi)
    lbr3, lbi3 = lbr.reshape(DEPTH, 1, nst), lbi.reshape(DEPTH, 1, nst)
    dvec3 = s5_d.reshape(DEPTH, 1, S5_W)
    h0r_all = state_s5_re.reshape(DEPTH, M_S, nst)
    h0i_all = state_s5_im.reshape(DEPTH, M_S, nst)

    xn = _norm(x, ffn1_g, 0)
    for l in range(DEPTH):
        hid, w2_16 = _ffn_up(xn, ffn1_w1, ffn1_w3, ffn1_w2, l)
        x, xn = _proj_res(hid, w2_16, x, 0, scale=0.5, tm=TM_DOWN, name="ffn_down",
                          gain=mix_g, gain_layer=l)

        proj, glu_w16, s5_proj16, ret_proj16, w_out16 = _inproj(
            xn, w_in, cos_t, sin_t, (glu_w, s5_proj, ret_proj, w_out), l)

        z5, hre_p, him_p, o_all, s_p = _mixer_prompt(proj, w_blk, cw_blk, lamr_t, lami_t, dvec3,
                                                      ret_tabs, gn4, l)
        o_s, ret_s = _ret_sample(proj, state_ret, g1, gn4, l, ret_s)
        o_all = lax.dynamic_update_slice(o_all, o_s.astype(BF16), (M_P, 0))

        z_s, hre_s, him_s = _s5_sample(proj, h0r_all, h0i_all, w_blk, cw_blk, lbr3, lbi3, dvec3, l)
        z = lax.dynamic_update_slice(z5.reshape((BATCH + 1) * SEQ, S5_W), z_s, (M_P, 0))

        m = _merge(z, o_all, proj, glu_w16, glu_b3, s5_proj16, ret_proj16, l)
        x, xn = _proj_res(m, w_out16, x, 0, scale=1.0, tm=TM_MERGE, name="out_proj",
                          gain=ffn2_g, gain_layer=l)

        hid, w2_16 = _ffn_up(xn, ffn2_w1, ffn2_w3, ffn2_w2, l)
        if l + 1 < DEPTH:
            x, xn = _proj_res(hid, w2_16, x, 0, scale=0.5, tm=TM_DOWN, name="ffn_down",
                              gain=ffn1_g, gain_layer=l + 1)
        else:
            x, = _proj_res(hid, w2_16, x, 0, scale=0.5, tm=TM_DOWN, name="ffn_down")

        unslab = lambda a: (a.transpose(1, 0, 2).reshape(BATCH, 2, S5_HALF_ST)
                            .reshape(BATCH, S5_GROUPS, S5_STATE))
        ret_p.append(s_p)
        s5r_p.append(unslab(hre_p))
        s5i_p.append(unslab(him_p))
        s5r_s.append(hre_s.reshape(M_S, S5_GROUPS, S5_STATE))
        s5i_s.append(him_s.reshape(M_S, S5_GROUPS, S5_STATE))

    g_fin = final_norm.reshape(1, D_MODEL)
    y_p = _final_norm(x, g_fin, M_P, 1024, 0)
    y_s = _final_norm(x, g_fin, M_S, M_S, M_P // M_S)
    return (y_p.reshape(BATCH, SEQ, D_MODEL), y_s.reshape(M_S, 1, D_MODEL),
            jnp.stack(ret_p), jnp.stack(s5r_p), jnp.stack(s5i_p),
            ret_s, jnp.stack(s5r_s), jnp.stack(s5i_s))
```

```python
import functools
import math

import jax
import jax.numpy as jnp
from jax import lax
from jax.experimental import pallas as pl
from jax.experimental.pallas import tpu as pltpu

F32 = jnp.float32
BF16 = jnp.bfloat16

D_MODEL = 2048
BATCH = 4
SEQ = 2048
DEPTH = 4
DEC_BATCH = 128
PAST_LEN = 16384
RET_HEADS = 8
RET_DK = 128
RET_DV = 128
RET_QK = RET_HEADS * RET_DK
RET_W = RET_HEADS * RET_DV
RET_CHUNK = 128
ROPE_BASE = 10000.0
S5_GROUPS = 64
S5_GROUP_CH = 16
S5_W = S5_GROUPS * S5_GROUP_CH
S5_STATE = 64
D_FF = 5632
IN_W = 2 * RET_QK + 2 * RET_W + S5_W + 2 * D_MODEL
NORM_EPS = 1e-6
HEAD_NORM_EPS = 1e-5

M_P = BATCH * SEQ
M_S = DEC_BATCH
M_ALL = M_P + M_S

LANES = 128
SUBLANES = 8
VMEM_LIMIT = 58 * 1024 * 1024

TM = 1040
TN_FF = 512
TM_DOWN = 416
COL = 1024
TM_BIG = 4 * TM
TN_IN = 512
IN_SUB = 256

S5_TC = 64
S5_PAD = SUBLANES // 2
S5_PITCH = S5_TC + S5_PAD
S5_GB = 16
S5_NBLK = S5_GROUPS // S5_GB
S5_BLK_CH = S5_GB * S5_GROUP_CH
S5_BLK_ST = S5_GB * S5_STATE
S5_HALF_ST = S5_GROUPS * S5_STATE // 2
S5_NSLAB = S5_HALF_ST // LANES
S5_NSEQ = BATCH * 2
S5_SLAB_GROUP = 4


def _cparams(*sem):
    return pltpu.CompilerParams(dimension_semantics=sem, vmem_limit_bytes=VMEM_LIMIT)


def _rms_to_bf16(x, g):
    ms = jnp.mean(x * x, axis=-1, keepdims=True)
    return ((x * lax.rsqrt(ms + NORM_EPS)) * g).astype(BF16)


def _layer_spec(l, k, tn):
    return pl.BlockSpec((None, k, tn), lambda i, j: (l, 0, j))


def _layer_row_spec(l, n):
    return pl.BlockSpec((None, 1, n), lambda *_: (l, 0, 0))


def _stream_init_kernel(xp_ref, xs_ref, g_ref, x_ref, xn_ref):
    is_sample = pl.program_id(0) == M_P // M_S

    @pl.when(jnp.logical_not(is_sample))
    def _():
        x = xp_ref[...]
        x_ref[...] = x
        xn_ref[...] = _rms_to_bf16(x, g_ref[...])

    @pl.when(is_sample)
    def _():
        x = xs_ref[...]
        x_ref[...] = x
        xn_ref[...] = _rms_to_bf16(x, g_ref[...])


def _stream_init(x_prompt, x_sample, g):
    nblk = M_P // M_S
    row_spec = lambda: pl.BlockSpec((M_S, D_MODEL), lambda i: (i, 0))
    return pl.pallas_call(
        _stream_init_kernel,
        grid=(nblk + 1,),
        in_specs=[
            pl.BlockSpec((M_S, D_MODEL), lambda i: (jnp.minimum(i, nblk - 1), 0)),
            pl.BlockSpec((M_S, D_MODEL), lambda i: (0, 0)),
            _layer_row_spec(0, D_MODEL),
        ],
        out_specs=[row_spec(), row_spec()],
        out_shape=[jax.ShapeDtypeStruct((M_ALL, D_MODEL), F32),
                   jax.ShapeDtypeStruct((M_ALL, D_MODEL), BF16)],
        compiler_params=_cparams("arbitrary"),
        name="stream_init",
    )(x_prompt, x_sample, g)


def _resident_rows_spec():
    return pl.BlockSpec((TM_BIG, D_MODEL), lambda r, j: (r, 0), pipeline_mode=pl.Buffered(1))


def _ffn_up_kernel(xn_ref, w1_ref, w3_ref, w2_ref, o_ref, w2_16_ref):
    w1 = w1_ref[...].astype(BF16)
    w3 = w3_ref[...].astype(BF16)
    for s in range(TM_BIG // TM):
        rows = slice(s * TM, (s + 1) * TM)
        xs = xn_ref[rows, :]
        for c in range(TN_FF // IN_SUB):
            cols = slice(c * IN_SUB, (c + 1) * IN_SUB)
            h1 = jnp.dot(xs, w1[:, cols], preferred_element_type=F32)
            h3 = jnp.dot(xs, w3[:, cols], preferred_element_type=F32)
            o_ref[rows, cols] = ((h1 * jax.nn.sigmoid(h1)) * h3).astype(BF16)
    w2_16_ref[...] = w2_ref[...].astype(BF16)


def _ffn_up(xn, w1, w3, w2, l):
    ncol = D_FF // TN_FF
    nsteps = (M_ALL // TM_BIG) * ncol
    chunk = D_FF // nsteps
    return pl.pallas_call(
        _ffn_up_kernel,
        grid=(M_ALL // TM_BIG, ncol),
        in_specs=[
            _resident_rows_spec(),
            _layer_spec(l, D_MODEL, TN_FF),
            _layer_spec(l, D_MODEL, TN_FF),
            pl.BlockSpec((None, chunk, D_MODEL), lambda r, j: (l, r * ncol + j, 0)),
        ],
        out_specs=[
            pl.BlockSpec((TM_BIG, TN_FF), lambda r, j: (r, j)),
            pl.BlockSpec((None, chunk, D_MODEL), lambda r, j: (0, r * ncol + j, 0)),
        ],
        out_shape=[
            jax.ShapeDtypeStruct((M_ALL, D_FF), BF16),
            jax.ShapeDtypeStruct((1, D_FF, D_MODEL), BF16),
        ],
        compiler_params=_cparams("parallel", "arbitrary"),
        name="ffn_up",
    )(xn, w1, w3, w2)


def _resident_layer_spec(l, shape):
    return pl.BlockSpec((None,) + shape, lambda i: (l, 0, 0), pipeline_mode=pl.Buffered(1))


def _proj_res_kernel(a_ref, w_ref, x_ref, *rest, scale):
    o_ref = rest[-2] if len(rest) == 3 else rest[0]
    for c in range(D_MODEL // COL):
        cols = slice(c * COL, (c + 1) * COL)
        acc = jnp.dot(a_ref[...], w_ref[:, cols], preferred_element_type=F32)
        o_ref[:, cols] = x_ref[:, cols] + scale * acc
    if len(rest) == 3:
        rest[2][...] = _rms_to_bf16(o_ref[...], rest[0][...])


def _proj_res(a, w16, x, l, *, scale, tm, name, gain=None, gain_layer=None):
    k = a.shape[1]
    row_spec = lambda: pl.BlockSpec((tm, D_MODEL), lambda i: (i, 0))
    in_specs = [pl.BlockSpec((tm, k), lambda i: (i, 0)), _resident_layer_spec(l, (k, D_MODEL)),
                row_spec()]
    args = [a, w16, x]
    out_specs = [row_spec()]
    out_shape = [jax.ShapeDtypeStruct((M_ALL, D_MODEL), F32)]
    if gain is not None:
        in_specs.append(_layer_row_spec(gain_layer, D_MODEL))
        args.append(gain)
        out_specs.append(row_spec())
        out_shape.append(jax.ShapeDtypeStruct((M_ALL, D_MODEL), BF16))
    return pl.pallas_call(
        functools.partial(_proj_res_kernel, scale=scale),
        grid=(M_ALL // tm,),
        in_specs=in_specs,
        out_specs=out_specs,
        out_shape=out_shape,
        compiler_params=_cparams("parallel"),
        name=name,
    )(*args)


def _inproj_kernel(xn_ref, w_ref, cos_ref, sin_ref, *rest):
    o_ref = rest[len(rest) // 2]
    for src, dst in zip(rest[:len(rest) // 2], rest[len(rest) // 2 + 1:]):
        dst[...] = src[...].astype(BF16)

    seg = pl.program_id(1) // (COL // TN_IN)

    def project(epilogue):
        wb = [w_ref[:, c * IN_SUB:(c + 1) * IN_SUB].astype(BF16) for c in range(TN_IN // IN_SUB)]
        for s in range(TM_BIG // TM):
            rows = slice(s * TM, (s + 1) * TM)
            for c in range(TN_IN // IN_SUB):
                acc = jnp.dot(xn_ref[rows, :], wb[c], preferred_element_type=F32)
                o_ref[rows, c * IN_SUB:(c + 1) * IN_SUB] = epilogue(acc, rows)

    def rotary(acc, rows):
        scale = jnp.where(seg == 0, RET_DK ** -0.5, 1.0).astype(F32)
        cos = cos_ref[rows, :]
        sin = sin_ref[rows, :]
        heads = []
        for h in range(IN_SUB // RET_DK):
            a = acc[:, h * RET_DK:(h + 1) * RET_DK]
            heads.append((a * cos + pltpu.roll(a, RET_DK // 2, axis=1) * sin) * scale)
        return jnp.concatenate(heads, axis=1)

    @pl.when(seg <= 1)
    def _():
        project(rotary)

    @pl.when((seg == 2) | (seg == 4))
    def _():
        project(lambda acc, rows: acc)

    @pl.when(seg == 3)
    def _():
        project(lambda acc, rows: acc * jax.nn.sigmoid(acc))

    @pl.when(seg >= 5)
    def _():
        project(lambda acc, rows: jax.nn.sigmoid(acc))


CAST_STEPS = 32


def _inproj(xn, w, cos_t, sin_t, small_weights, l):
    ncol = IN_W // TN_IN
    tab_spec = pl.BlockSpec((TM_BIG, RET_DK), lambda r, j: (r, 0), pipeline_mode=pl.Buffered(1))
    chunk_idx = lambda r, j: jnp.minimum(r * ncol + j, CAST_STEPS - 1)
    cast_in, cast_out, cast_shapes = [], [], []
    for a in small_weights:
        rows, cols = a.shape[1], a.shape[2]
        blk = (None, rows // CAST_STEPS, cols)
        cast_in.append(pl.BlockSpec(blk, lambda r, j: (l, chunk_idx(r, j), 0)))
        cast_out.append(pl.BlockSpec(blk, lambda r, j: (0, chunk_idx(r, j), 0)))
        cast_shapes.append(jax.ShapeDtypeStruct((1, rows, cols), BF16))
    return pl.pallas_call(
        _inproj_kernel,
        grid=(M_ALL // TM_BIG, ncol),
        in_specs=[
            _resident_rows_spec(),
            _layer_spec(l, D_MODEL, TN_IN),
            tab_spec, tab_spec,
        ] + cast_in,
        out_specs=[pl.BlockSpec((TM_BIG, TN_IN), lambda r, j: (r, j))] + cast_out,
        out_shape=[jax.ShapeDtypeStruct((M_ALL, IN_W), F32)] + cast_shapes,
        compiler_params=_cparams("parallel", "arbitrary"),
        name="inproj",
    )(xn, w, cos_t, sin_t, *small_weights)


def _head_norm(o):
    mu = jnp.mean(o, axis=-1, keepdims=True)
    d = o - mu
    var = jnp.mean(d * d, axis=-1, keepdims=True)
    return d * lax.rsqrt(var + HEAD_NORM_EPS)


RET_RC = 2 * RET_CHUNK


def _ret_prompt_body(q_ref, k_ref, v_ref, rg_ref, intra_ref, qd_ref, kd_ref, cd_ref, gn_ref,
                       o_ref, st_ref):
    nc = RET_RC // RET_CHUNK
    blk = lambda c, h: (slice(c * RET_CHUNK, (c + 1) * RET_CHUNK), slice(h * RET_DK, (h + 1) * RET_DK))

    o_intra, kv = {}, {}
    for c in range(nc):
        for h in range(RET_HEADS):
            qc = q_ref[blk(c, h)]
            kc = k_ref[blk(c, h)]
            vb = v_ref[blk(c, h)].astype(BF16)
            sc = lax.dot_general(qc.astype(BF16), kc.astype(BF16), (((1,), (1,)), ((), ())),
                                 preferred_element_type=F32) * intra_ref[h]
            o_intra[c, h] = jnp.dot(sc.astype(BF16), vb, preferred_element_type=F32)
            kt = jnp.transpose(kc * kd_ref[h]).astype(BF16)
            kv[c, h] = jnp.dot(kt, vb, preferred_element_type=F32)

    for h in range(RET_HEADS):
        s = st_ref[h]
        for c in range(nc):
            qs = jnp.dot((q_ref[blk(c, h)] * qd_ref[h]).astype(BF16), s.astype(BF16),
                         preferred_element_type=F32)
            o = o_intra[c, h] + qs
            s = s * cd_ref[h] + kv[c, h]
            on = (_head_norm(o) * gn_ref[h]) * rg_ref[blk(c, h)]
            o_ref[blk(c, h)] = on.astype(BF16)
        st_ref[h] = s


RET_SB = 8


def _col_bcast(row, eye, ones2):
    hi = row.astype(BF16)
    lo = (row - hi.astype(F32)).astype(BF16)
    parts = [jnp.where(eye, jnp.broadcast_to(t.astype(F32), (RET_DK, RET_DK)), 0.0).astype(BF16)
             for t in (hi, lo)]
    return jnp.dot(jnp.concatenate(parts, axis=1), ones2, preferred_element_type=F32)


def _ret_sample_kernel(q_ref, k_ref, v_ref, rg_ref, s_ref, g1_ref, gn_ref, *rest):
    o_ref, so_ref = rest[-2:]
    eye = (lax.broadcasted_iota(jnp.int32, (RET_DK, RET_DK), 0)
           == lax.broadcasted_iota(jnp.int32, (RET_DK, RET_DK), 1))
    ones2 = jnp.ones((2 * RET_DK, RET_DV), BF16)

    for b in range(RET_SB):
        row = slice(b, b + 1)
        for h in range(RET_HEADS):
            cols = slice(h * RET_DK, (h + 1) * RET_DK)
            kcol = _col_bcast(k_ref[row, cols], eye, ones2)
            qcol = _col_bcast(q_ref[row, cols], eye, ones2)
            s_new = s_ref[b, h] * g1_ref[h] + kcol * v_ref[row, cols]
            so_ref[b, h] = s_new
            o = jnp.sum(qcol * s_new, axis=0, keepdims=True)
            o_ref[row, cols] = (_head_norm(o) * gn_ref[h]) * rg_ref[row, cols]


def _ret_sample(proj, state, g1, gn, l, prev_states):
    row0 = M_P // RET_SB

    def seg_spec(seg):
        return pl.BlockSpec((RET_SB, COL), lambda i, seg=seg: (row0 + i, seg))

    st_spec = pl.BlockSpec((None, RET_SB, RET_HEADS, RET_DK, RET_DV), lambda i: (l, i, 0, 0, 0))
    in_specs = [
        seg_spec(0), seg_spec(1), seg_spec(2), seg_spec(3), st_spec,
        pl.BlockSpec((RET_HEADS, 1, RET_DV), lambda i: (0, 0, 0)),
        pl.BlockSpec((None, RET_HEADS, 1, RET_DV), lambda i: (l, 0, 0, 0)),
    ]
    args = [proj, proj, proj, proj, state, g1, gn]
    aliases = {}
    if prev_states is not None:
        in_specs.append(pl.BlockSpec(memory_space=pl.ANY))
        args.append(prev_states)
        aliases = {len(args) - 1: 1}
    return pl.pallas_call(
        _ret_sample_kernel,
        grid=(M_S // RET_SB,),
        in_specs=in_specs,
        out_specs=[pl.BlockSpec((RET_SB, RET_W), lambda i: (i, 0)), st_spec],
        out_shape=[
            jax.ShapeDtypeStruct((M_S, RET_W), F32),
            jax.ShapeDtypeStruct((DEPTH, M_S, RET_HEADS, RET_DK, RET_DV), F32),
        ],
        input_output_aliases=aliases,
        compiler_params=_cparams("parallel"),
        name="ret_sample",
    )(*args)


def _s5_disc_kernel(lr_ref, li_ref, ls_ref, br_ref, bi_ref, lbr_ref, lbi_ref, bbr_ref, bbi_ref):
    lam_r = lr_ref[...]
    lam_i = li_ref[...]
    dt = jnp.exp(ls_ref[...])
    ea = jnp.exp(lam_r * dt)
    lb_r = ea * jnp.cos(lam_i * dt)
    lb_i = ea * jnp.sin(lam_i * dt)
    lbr_ref[...] = lb_r
    lbi_ref[...] = lb_i
    nr = lb_r - 1.0
    den = lam_r * lam_r + lam_i * lam_i
    cr = (nr * lam_r + lb_i * lam_i) / den
    ci = (lb_i * lam_r - nr * lam_i) / den
    b_r = br_ref[...]
    b_i = bi_ref[...]
    bbr_ref[...] = cr * b_r - ci * b_i
    bbi_ref[...] = cr * b_i + ci * b_r


def _s5_discretize(lam_re, lam_im, log_step, b_re, b_im):
    rows = DEPTH * S5_GROUPS * S5_GROUP_CH
    rep = lambda a: jnp.repeat(a.reshape(DEPTH * S5_GROUPS, S5_STATE), S5_GROUP_CH, axis=0)
    tr = lambda a: jnp.transpose(a, (0, 1, 3, 2)).reshape(rows, S5_STATE)
    shp = jax.ShapeDtypeStruct((rows, S5_STATE), F32)
    ls = jnp.broadcast_to(log_step[:, :, None], (DEPTH, S5_GROUPS, S5_STATE))
    spec = pl.BlockSpec((rows // DEPTH, S5_STATE), lambda l: (l, 0))
    return pl.pallas_call(
        _s5_disc_kernel,
        grid=(DEPTH,),
        in_specs=[spec] * 5,
        out_specs=[spec] * 4,
        out_shape=[shp, shp, shp, shp],
        compiler_params=_cparams("parallel"),
        name="s5_disc",
    )(rep(lam_re), rep(lam_im), rep(ls), tr(b_re), tr(b_im))


def _gelu_tanh(y):
    c = math.sqrt(2.0 / math.pi)
    return 0.5 * y * (1.0 + jnp.tanh(c * (y + 0.044715 * (y * y * y))))


def _s5_prompt_body(u0_ref, u1_ref, u2_ref, u3_ref, w_ref, cw_ref, lamr_ref, lami_ref, d_ref,
                    z_ref, hre_ref, him_ref, bre_ref, bim_ref):
    u_refs = (u0_ref, u1_ref, u2_ref, u3_ref)

    nsl = S5_BLK_ST // LANES
    zpad = jnp.zeros((S5_PAD, S5_BLK_CH), F32)

    for gh in range(2):
        seg = S5_TC + 2 * S5_PAD * gh
        for kb in range(2):
            blk = gh * 2 + kb
            c0 = blk * S5_BLK_CH
            parts = []
            for u in u_refs:
                ub = u[:, c0:c0 + S5_BLK_CH]
                parts.extend([zpad, ub, zpad] if gh else [ub])
            x = jnp.concatenate(parts, axis=0).astype(BF16)
            res = jnp.dot(x, w_ref[blk], preferred_element_type=F32)
            for b in range(BATCH):
                r0 = (b * 2 + gh) * S5_PITCH - S5_PAD * gh
                part = res[b * seg:(b + 1) * seg]
                for s in range(nsl):
                    slab = kb * nsl + s
                    bre_ref[slab, r0:r0 + seg, :] = part[:, s * LANES:(s + 1) * LANES]
                    bim_ref[slab, r0:r0 + seg, :] = part[:, S5_BLK_ST + s * LANES:S5_BLK_ST + (s + 1) * LANES]

    for sg in range(S5_NSLAB // S5_SLAB_GROUP):
        slabs = [sg * S5_SLAB_GROUP + i for i in range(S5_SLAB_GROUP)]
        lam_r = [lamr_ref[s] for s in slabs]
        lam_i = [lami_ref[s] for s in slabs]

        def step(t, carry, slabs=slabs, lam_r=lam_r, lam_i=lam_i):
            hr, hi = carry
            nr_all = []
            ni_all = []
            for i, s in enumerate(slabs):
                idx = pl.ds(t, S5_NSEQ, stride=S5_PITCH)
                br = bre_ref[s, idx, :]
                bi = bim_ref[s, idx, :]
                nr = lam_r[i] * hr[i] - lam_i[i] * hi[i] + br
                ni = lam_r[i] * hi[i] + lam_i[i] * hr[i] + bi
                bre_ref[s, idx, :] = nr
                bim_ref[s, idx, :] = ni
                nr_all.append(nr)
                ni_all.append(ni)
            return tuple(nr_all), tuple(ni_all)

        hr, hi = (tuple(hre_ref[s] for s in slabs), tuple(him_ref[s] for s in slabs))
        for t in range(S5_TC):
            hr, hi = step(t, (hr, hi))
        for i, s in enumerate(slabs):
            hre_ref[s] = hr[i]
            him_ref[s] = hi[i]

    for gh in range(2):
        seg = S5_TC + 2 * S5_PAD * gh
        for kb in range(2):
            blk = gh * 2 + kb
            c0 = blk * S5_BLK_CH
            pieces = []
            for src in (bre_ref, bim_ref):
                for s in range(nsl):
                    slab = kb * nsl + s
                    starts = [(b * 2 + gh) * S5_PITCH - S5_PAD * gh for b in range(BATCH)]
                    pieces.append(jnp.concatenate(
                        [src[slab, r0:r0 + seg, :] for r0 in starts], axis=0))
            hcat = jnp.concatenate(pieces, axis=1).astype(BF16)
            y = jnp.dot(hcat, cw_ref[blk], preferred_element_type=F32)
            dvec = d_ref[:, c0:c0 + S5_BLK_CH]
            for b in range(BATCH):
                lo = b * seg + S5_PAD * gh
                yb = y[lo:lo + S5_TC] + dvec * u_refs[b][:, c0:c0 + S5_BLK_CH]
                z_ref[b, :, c0:c0 + S5_BLK_CH] = _gelu_tanh(yb)


RET_STEPS = SEQ // RET_RC
assert BATCH * RET_STEPS == SEQ // S5_TC


def _mixer_prompt_kernel(u0_ref, u1_ref, u2_ref, u3_ref, w_ref, cw_ref, lamr_ref, lami_ref, d_ref,
                         q_ref, k_ref, v_ref, rg_ref, intra_ref, qd_ref, kd_ref, cd_ref, gn_ref,
                         z_ref, hre_ref, him_ref, o_ref, s_ref, bre_ref, bim_ref, st_ref):
    j = pl.program_id(0)
    rb = j % RET_STEPS

    @pl.when(j == 0)
    def _():
        hre_ref[...] = jnp.zeros_like(hre_ref)
        him_ref[...] = jnp.zeros_like(him_ref)

    @pl.when(rb == 0)
    def _():
        st_ref[...] = jnp.zeros_like(st_ref)

    _s5_prompt_body(u0_ref, u1_ref, u2_ref, u3_ref, w_ref, cw_ref, lamr_ref, lami_ref, d_ref,
                    z_ref, hre_ref, him_ref, bre_ref, bim_ref)
    _ret_prompt_body(q_ref, k_ref, v_ref, rg_ref, intra_ref, qd_ref, kd_ref, cd_ref, gn_ref,
                       o_ref, st_ref)

    @pl.when(rb == RET_STEPS - 1)
    def _():
        s_ref[...] = st_ref[...]


def _mixer_prompt(proj, w_blk, cw_blk, lamr_t, lami_t, dvec, ret_tabs, gn, l):
    nsteps = SEQ // S5_TC
    su_col = (2 * RET_QK + 2 * RET_W) // S5_W
    intra, qd, kd, cd = ret_tabs

    def u_spec(b):
        return pl.BlockSpec((S5_TC, S5_W), lambda j, b=b: (b * nsteps + j, su_col))

    def seg_spec(seg):
        return pl.BlockSpec((RET_RC, COL), lambda j, seg=seg: (j, seg))

    layer4 = lambda shape: pl.BlockSpec((None,) + shape, lambda j: (l, 0, 0, 0))
    full3 = lambda shape: pl.BlockSpec(shape, lambda j: (0, 0, 0))
    st_shape = (S5_NSLAB, S5_NSEQ, LANES)
    tab_spec = full3((RET_HEADS, RET_CHUNK, RET_CHUNK))
    return pl.pallas_call(
        _mixer_prompt_kernel,
        grid=(nsteps,),
        in_specs=[
            u_spec(0), u_spec(1), u_spec(2), u_spec(3),
            layer4((S5_NBLK, S5_BLK_CH, 2 * S5_BLK_ST)),
            layer4((S5_NBLK, 2 * S5_BLK_ST, S5_BLK_CH)),
            layer4(st_shape), layer4(st_shape),
            _layer_row_spec(l, S5_W),
            seg_spec(0), seg_spec(1), seg_spec(2), seg_spec(3),
            tab_spec, tab_spec, tab_spec, tab_spec,
            layer4((RET_HEADS, 1, RET_DV)),
        ],
        out_specs=[
            pl.BlockSpec((BATCH, S5_TC, S5_W), lambda j: (0, j, 0)),
            full3(st_shape), full3(st_shape),
            pl.BlockSpec((RET_RC, RET_W), lambda j: (j, 0)),
            pl.BlockSpec((None, RET_HEADS, RET_DK, RET_DV), lambda j: (j // RET_STEPS, 0, 0, 0)),
        ],
        out_shape=[
            jax.ShapeDtypeStruct((BATCH + 1, SEQ, S5_W), F32),
            jax.ShapeDtypeStruct(st_shape, F32),
            jax.ShapeDtypeStruct(st_shape, F32),
            jax.ShapeDtypeStruct((M_ALL, RET_W), BF16),
            jax.ShapeDtypeStruct((BATCH, RET_HEADS, RET_DK, RET_DV), F32),
        ],
        scratch_shapes=[
            pltpu.VMEM((S5_NSLAB, S5_NSEQ * S5_PITCH, LANES), F32),
            pltpu.VMEM((S5_NSLAB, S5_NSEQ * S5_PITCH, LANES), F32),
            pltpu.VMEM((RET_HEADS, RET_DK, RET_DV), F32),
        ],
        compiler_params=_cparams("arbitrary"),
        name="mixer_prompt",
    )(proj, proj, proj, proj, w_blk, cw_blk, lamr_t, lami_t, dvec,
      proj, proj, proj, proj, intra, qd, kd, cd, gn)


def _s5_sample_kernel(u_ref, h0r_ref, h0i_ref, w_ref, cw_ref, lbr_ref, lbi_ref, d_ref,
                      z_ref, hr_ref, hi_ref):
    for blk in range(S5_NBLK):
        c0 = blk * S5_BLK_CH
        s0 = blk * S5_BLK_ST
        u = u_ref[:, c0:c0 + S5_BLK_CH]
        res = jnp.dot(u.astype(BF16), w_ref[blk], preferred_element_type=F32)
        lr = lbr_ref[:, s0:s0 + S5_BLK_ST]
        li = lbi_ref[:, s0:s0 + S5_BLK_ST]
        h0r = h0r_ref[:, s0:s0 + S5_BLK_ST]
        h0i = h0i_ref[:, s0:s0 + S5_BLK_ST]
        hr = lr * h0r - li * h0i + res[:, :S5_BLK_ST]
        hi = lr * h0i + li * h0r + res[:, S5_BLK_ST:]
        hr_ref[:, s0:s0 + S5_BLK_ST] = hr
        hi_ref[:, s0:s0 + S5_BLK_ST] = hi
        hcat = jnp.concatenate([hr, hi], axis=1).astype(BF16)
        y = jnp.dot(hcat, cw_ref[blk], preferred_element_type=F32) + d_ref[:, c0:c0 + S5_BLK_CH] * u
        z_ref[:, c0:c0 + S5_BLK_CH] = _gelu_tanh(y)


def _s5_sample(proj, h0r, h0i, w_blk, cw_blk, lbr, lbi, dvec, l):
    nst = S5_GROUPS * S5_STATE
    su_col = (2 * RET_QK + 2 * RET_W) // S5_W
    layer3 = lambda shape: pl.BlockSpec((None,) + shape, lambda i: (l, 0, 0))
    layer4 = lambda shape: pl.BlockSpec((None,) + shape, lambda i: (l, 0, 0, 0))
    whole = lambda shape: pl.BlockSpec(shape, lambda i: (0, 0))
    return pl.pallas_call(
        _s5_sample_kernel,
        grid=(1,),
        in_specs=[
            pl.BlockSpec((M_S, S5_W), lambda i: (M_P // M_S, su_col)),
            layer3((M_S, nst)), layer3((M_S, nst)),
            layer4((S5_NBLK, S5_BLK_CH, 2 * S5_BLK_ST)),
            layer4((S5_NBLK, 2 * S5_BLK_ST, S5_BLK_CH)),
            layer3((1, nst)), layer3((1, nst)),
            _layer_row_spec(l, S5_W),
        ],
        out_specs=[whole((M_S, S5_W)), whole((M_S, nst)), whole((M_S, nst))],
        out_shape=[
            jax.ShapeDtypeStruct((M_S, S5_W), F32),
            jax.ShapeDtypeStruct((M_S, nst), F32),
            jax.ShapeDtypeStruct((M_S, nst), F32),
        ],
        compiler_params=_cparams("arbitrary"),
        name="s5_sample",
    )(proj, h0r, h0i, w_blk, cw_blk, lbr, lbi, dvec)


TM_MERGE = 520


def _merge_kernel(z_ref, o_ref, gr0_ref, gr1_ref, gs0_ref, gs1_ref, gluw_ref, glub_ref, s5p_ref,
                  retp_ref, m_ref):
    z = z_ref[...]
    t = jnp.dot(z.astype(BF16), gluw_ref[...], preferred_element_type=F32) + glub_ref[...]
    z2 = (z * jax.nn.sigmoid(t)).astype(BF16)
    o = o_ref[...]
    for c, (gr_ref, gs_ref) in enumerate(((gr0_ref, gs0_ref), (gr1_ref, gs1_ref))):
        cols = slice(c * COL, (c + 1) * COL)
        b_s5 = jnp.dot(z2, s5p_ref[:, cols], preferred_element_type=F32)
        b_ret = jnp.dot(o, retp_ref[:, cols], preferred_element_type=F32)
        m_ref[:, cols] = (gr_ref[...] * b_ret + gs_ref[...] * b_s5).astype(BF16)


def _merge(z, o, proj, glu_w, glu_b, s5_proj, ret_proj, l):
    gr_col = (2 * RET_QK + 2 * RET_W + S5_W) // COL
    gs_col = gr_col + D_MODEL // COL
    gate = lambda col: pl.BlockSpec((TM_MERGE, COL), lambda i, col=col: (i, col))
    return pl.pallas_call(
        _merge_kernel,
        grid=(M_ALL // TM_MERGE,),
        in_specs=[
            pl.BlockSpec((TM_MERGE, S5_W), lambda i: (i, 0)),
            pl.BlockSpec((TM_MERGE, RET_W), lambda i: (i, 0)),
            gate(gr_col), gate(gr_col + 1), gate(gs_col), gate(gs_col + 1),
            _resident_layer_spec(0, (S5_W, S5_W)),
            _layer_row_spec(l, S5_W),
            _resident_layer_spec(0, (S5_W, D_MODEL)),
            _resident_layer_spec(0, (RET_W, D_MODEL)),
        ],
        out_specs=pl.BlockSpec((TM_MERGE, D_MODEL), lambda i: (i, 0)),
        out_shape=jax.ShapeDtypeStruct((M_ALL, D_MODEL), BF16),
        compiler_params=_cparams("parallel"),
        name="merge",
    )(z, o, proj, proj, proj, proj, glu_w, glu_b, s5_proj, ret_proj)


def _final_norm_kernel(x_ref, g_ref, o_ref):
    x = x_ref[...]
    ms = jnp.mean(x * x, axis=-1, keepdims=True)
    o_ref[...] = (x * lax.rsqrt(ms + NORM_EPS)) * g_ref[...]


def _final_norm(x, g, rows, tm, block0):
    return pl.pallas_call(
        _final_norm_kernel,
        grid=(rows // tm,),
        in_specs=[pl.BlockSpec((tm, D_MODEL), lambda i: (block0 + i, 0)),
                  pl.BlockSpec((1, D_MODEL), lambda i: (0, 0))],
        out_specs=pl.BlockSpec((tm, D_MODEL), lambda i: (i, 0)),
        out_shape=jax.ShapeDtypeStruct((rows, D_MODEL), F32),
        compiler_params=_cparams("parallel"),
        name="final_norm",
    )(x, g)


def _rope_tables():
    half = RET_DK // 2
    inv = ROPE_BASE ** (-jnp.arange(half, dtype=F32) / half)
    pos = jnp.concatenate([jnp.tile(jnp.arange(SEQ, dtype=jnp.int32), BATCH),
                           jnp.full((M_S,), PAST_LEN, jnp.int32)])
    ang = pos.astype(F32)[:, None] * inv[None, :]
    cos = jnp.cos(ang)
    sin = jnp.sin(ang)
    return jnp.concatenate([cos, cos], axis=1), jnp.concatenate([-sin, sin], axis=1)


def _retention_tables():
    c = RET_CHUNK
    log_g = jnp.log1p(-jnp.exp2(-5.0 - jnp.arange(RET_HEADS, dtype=F32)))
    i = jnp.arange(c, dtype=F32)
    diff = i[:, None] - i[None, :]
    intra = jnp.where(diff[None] >= 0,
                      jnp.exp(jnp.maximum(diff, 0.0)[None] * log_g[:, None, None]), 0.0)
    q_dec = jnp.exp((i + 1.0)[None, :] * log_g[:, None])
    k_dec = jnp.exp((c - 1.0 - i)[None, :] * log_g[:, None])
    c_dec = jnp.exp(c * log_g)
    full = lambda a: jnp.broadcast_to(a, (RET_HEADS, c, RET_DV))
    g1 = jnp.broadcast_to(jnp.exp(1.0 * log_g)[:, None, None], (RET_HEADS, 1, RET_DV))
    return (intra, full(q_dec[:, :, None]), full(k_dec[:, :, None]), full(c_dec[:, None, None])), g1


def _s5_block_weights(bbr_t, bbi_t, c_re, c_im):
    ch_group = jnp.arange(S5_BLK_CH, dtype=jnp.int32) // S5_GROUP_CH
    st_group = jnp.arange(S5_BLK_ST, dtype=jnp.int32) // S5_STATE
    drive_mask = ch_group[:, None] == st_group[None, :]

    def drive(bt):
        a = bt.reshape(DEPTH, S5_NBLK, S5_BLK_CH, 1, S5_STATE)
        a = jnp.broadcast_to(a, (DEPTH, S5_NBLK, S5_BLK_CH, S5_GB, S5_STATE))
        return jnp.where(drive_mask, a.reshape(DEPTH, S5_NBLK, S5_BLK_CH, S5_BLK_ST), 0.0)

    def readout(cm):
        a = cm.reshape(DEPTH, S5_NBLK, S5_GB, S5_GROUP_CH, S5_STATE)
        a = a.transpose(0, 1, 4, 2, 3).reshape(DEPTH, S5_NBLK, 1, S5_STATE, S5_BLK_CH)
        a = jnp.broadcast_to(a, (DEPTH, S5_NBLK, S5_GB, S5_STATE, S5_BLK_CH))
        return jnp.where(drive_mask.T, a.reshape(DEPTH, S5_NBLK, S5_BLK_ST, S5_BLK_CH), 0.0)

    w_blk = jnp.concatenate([drive(bbr_t), drive(bbi_t)], axis=3).astype(BF16)
    cw_blk = jnp.concatenate([readout(c_re), readout(-c_im)], axis=2).astype(BF16)
    return w_blk, cw_blk


def _slab_tiles(lb):
    a = lb.reshape(DEPTH, 2, S5_NSLAB, LANES).transpose(0, 2, 1, 3)
    return jnp.tile(a, (1, 1, BATCH, 1))


def kernel(x_prompt, x_sample, state_ret, state_s5_re, state_s5_im, ffn1_norm, ffn1_w1, ffn1_w3, ffn1_w2, mix_norm, w_in, ret_gn, ret_proj, s5_lam_re, s5_lam_im, s5_b_re, s5_b_im, s5_c_re, s5_c_im, s5_d, s5_log_step, glu_w, glu_b, s5_proj, w_out, ffn2_norm, ffn2_w1, ffn2_w3, ffn2_w2, final_norm):
    cos_t, sin_t = _rope_tables()
    ret_tabs, g1 = _retention_tables()

    ret_p, s5r_p, s5i_p, s5r_s, s5i_s = [], [], [], [], []
    ret_s = None
    nst = S5_GROUPS * S5_STATE
    rows3 = lambda a: a.reshape(DEPTH, 1, -1)
    ffn1_g, mix_g, ffn2_g, glu_b3 = rows3(ffn1_norm), rows3(mix_norm), rows3(ffn2_norm), rows3(glu_b)
    gn4 = ret_gn.reshape(DEPTH, RET_HEADS, 1, RET_DV)

    lbr_x, lbi_x, bbr_t, bbi_t = _s5_discretize(s5_lam_re, s5_lam_im, s5_log_step, s5_b_re, s5_b_im)
    lbr = lbr_x[::S5_GROUP_CH].reshape(DEPTH, nst)
    lbi = lbi_x[::S5_GROUP_CH].reshape(DEPTH, nst)
    w_blk, cw_blk = _s5_block_weights(bbr_t, bbi_t, s5_c_re, s5_c_im)
    lamr_t, lami_t = _slab_tiles(lbr), _slab_tiles(lbi)
    lbr3, lbi3 = lbr.reshape(DEPTH, 1, nst), lbi.reshape(DEPTH, 1, nst)
    dvec3 = s5_d.reshape(DEPTH, 1, S5_W)
    h0r_all = state_s5_re.reshape(DEPTH, M_S, nst)
    h0i_all = state_s5_im.reshape(DEPTH, M_S, nst)

    x, xn = _stream_init(x_prompt.reshape(M_P, D_MODEL), x_sample.reshape(M_S, D_MODEL), ffn1_g)
    for l in range(DEPTH):
        hid, w2_16 = _ffn_up(xn, ffn1_w1, ffn1_w3, ffn1_w2, l)
        x, xn = _proj_res(hid, w2_16, x, 0, scale=0.5, tm=TM_DOWN, name="ffn_down",
                          gain=mix_g, gain_layer=l)

        proj, glu_w16, s5_proj16, ret_proj16, w_out16 = _inproj(
            xn, w_in, cos_t, sin_t, (glu_w, s5_proj, ret_proj, w_out), l)

        z5, hre_p, him_p, o_all, s_p = _mixer_prompt(proj, w_blk, cw_blk, lamr_t, lami_t, dvec3,
                                                      ret_tabs, gn4, l)
        o_s, ret_s = _ret_sample(proj, state_ret, g1, gn4, l, ret_s)
        o_all = lax.dynamic_update_slice(o_all, o_s.astype(BF16), (M_P, 0))

        z_s, hre_s, him_s = _s5_sample(proj, h0r_all, h0i_all, w_blk, cw_blk, lbr3, lbi3, dvec3, l)
        z = lax.dynamic_update_slice(z5.reshape((BATCH + 1) * SEQ, S5_W), z_s, (M_P, 0))

        m = _merge(z, o_all, proj, glu_w16, glu_b3, s5_proj16, ret_proj16, l)
        x, xn = _proj_res(m, w_out16, x, 0, scale=1.0, tm=TM_MERGE, name="out_proj",
                          gain=ffn2_g, gain_layer=l)

        hid, w2_16 = _ffn_up(xn, ffn2_w1, ffn2_w3, ffn2_w2, l)
        if l + 1 < DEPTH:
            x, xn = _proj_res(hid, w2_16, x, 0, scale=0.5, tm=TM_DOWN, name="ffn_down",
                              gain=ffn1_g, gain_layer=l + 1)
        else:
            x, = _proj_res(hid, w2_16, x, 0, scale=0.5, tm=TM_DOWN, name="ffn_down")

        unslab = lambda a: (a.transpose(1, 0, 2).reshape(BATCH, 2, S5_HALF_ST)
                            .reshape(BATCH, S5_GROUPS, S5_STATE))
        ret_p.append(s_p)
        s5r_p.append(unslab(hre_p))
        s5i_p.append(unslab(him_p))
        s5r_s.append(hre_s.reshape(M_S, S5_GROUPS, S5_STATE))
        s5i_s.append(him_s.reshape(M_S, S5_GROUPS, S5_STATE))

    g_fin = final_norm.reshape(1, D_MODEL)
    y_p = _final_norm(x, g_fin, M_P, 1024, 0)
    y_s = _final_norm(x, g_fin, M_S, M_S, M_P // M_S)
    return (y_p.reshape(BATCH, SEQ, D_MODEL), y_s.reshape(M_S, 1, D_MODEL),
            jnp.stack(ret_p), jnp.stack(s5r_p), jnp.stack(s5i_p),
            ret_s, jnp.stack(s5r_s), jnp.stack(s5i_s))
```
